```python
import math
import jax
import jax.numpy as jnp
from jax import lax
import numpy as np

D_MODEL = 1024
BATCH = 2
SEQ = 8192
DEPTH = 2
DEC_BATCH = 128
DEC_SEQ = 4
PAST_LEN = 8192
PAGE_SIZE = 128

HEAD_DIM = 64
NSA_HEADS = D_MODEL // 128
NSA_KV_HEADS = NSA_HEADS // 4
NSA_GROUP = NSA_HEADS // NSA_KV_HEADS
CMP_STRIDE = 16
CMP_LEN = 2 * CMP_STRIDE
SEL_BLOCK = 64
N_SELECT = 16
NSA_WINDOW = 512
DIFF_HEADS = D_MODEL // 256
DIFF_VDIM = 2 * HEAD_DIM
SWA_HEADS = D_MODEL // 128
SWA_KV_HEADS = SWA_HEADS // 4
SWA_GROUP = SWA_HEADS // SWA_KV_HEADS
SWA_WINDOW = 128
RWKV_N = 64
RWKV_WIDTH = D_MODEL // 2
RWKV_HEADS = RWKV_WIDTH // RWKV_N
DECAY_LORA = max(32, int(round(1.8 * RWKV_WIDTH ** 0.5 / 32)) * 32)
AAA_LORA = max(32, int(round(1.8 * RWKV_WIDTH ** 0.5 / 32)) * 32)
GATE_LORA = max(32, int(round(0.6 * RWKV_WIDTH ** 0.8 / 32)) * 32)
RWKV_SIZES = (RWKV_WIDTH, RWKV_WIDTH, RWKV_WIDTH, DECAY_LORA, AAA_LORA, GATE_LORA)
RWKV_PROJ = 3 * RWKV_WIDTH + DECAY_LORA + AAA_LORA + GATE_LORA
RWKV_GN_EPS = 64e-5
D_FF = -(-(8 * D_MODEL // 3) // 256) * 256
N_BUCKETS = 32
MAX_DISTANCE = 128
BIAS_HEADS = NSA_HEADS + DIFF_HEADS
EVEN_SIZES = (NSA_HEADS * HEAD_DIM, 4 * NSA_KV_HEADS * HEAD_DIM, 2 * NSA_KV_HEADS * HEAD_DIM, 3 * NSA_HEADS, 2 * DIFF_HEADS * HEAD_DIM, 2 * DIFF_HEADS * DIFF_VDIM)
EVEN_IN = sum(EVEN_SIZES)
EVEN_MIX = NSA_HEADS * HEAD_DIM + DIFF_HEADS * DIFF_VDIM
ODD_SIZES = (SWA_HEADS * HEAD_DIM, 2 * SWA_KV_HEADS * HEAD_DIM, RWKV_PROJ)
ODD_IN = sum(ODD_SIZES)
ODD_MIX = SWA_HEADS * HEAD_DIM + RWKV_WIDTH
N_EVEN = (DEPTH + 1) // 2
N_ODD = DEPTH // 2
Q_BLOCK = 128
NORM_EPS = 1e-6
SUBLN_EPS = 1e-5
NEG = -1e30
FORCE = 1e6

kernel_name = 'hybrid_nsa_diff_swa_rwkv7_step'


def rmsnorm(x, g, eps=NORM_EPS):
    xf = x.astype(jnp.float32)
    y = xf * lax.rsqrt(jnp.mean(xf * xf, axis=-1, keepdims=True) + eps)
    return (y * g.astype(jnp.float32)).astype(x.dtype)


def swiglu(h, w_gate, w_up, w_down):
    return (jax.nn.silu(h @ w_gate) * (h @ w_up)) @ w_down


def split_cols(u, sizes):
    offs = np.cumsum([0] + list(sizes))
    return [u[..., int(offs[i]):int(offs[i + 1])] for i in range(len(sizes))]


def t5_bucket(dist):
    n = jnp.maximum(dist, 0)
    max_exact = N_BUCKETS // 2
    nf = jnp.maximum(n, 1).astype(jnp.float32)
    large = max_exact + (jnp.log(nf / max_exact) / math.log(MAX_DISTANCE / max_exact) * (N_BUCKETS - max_exact)).astype(jnp.int32)
    large = jnp.minimum(large, N_BUCKETS - 1)
    return jnp.where(n < max_exact, n, large)


def masked_softmax(s, mask, sink=None):
    s = jnp.where(mask, s, NEG)
    m = jnp.max(s, axis=-1, keepdims=True)
    if sink is not None:
        m = jnp.maximum(m, sink)
    p = jnp.where(mask, jnp.exp(s - m), 0.0)
    den = jnp.sum(p, axis=-1, keepdims=True)
    if sink is not None:
        den = den + jnp.exp(sink - m)
    return p / jnp.maximum(den, 1e-30)


def dense_probs(q, k, q_pos, k_pos, tab, window=None, sink=None):
    G, Hg = q.shape[2], q.shape[3]
    s = jnp.einsum('bqghd,bkgd->bghqk', q.astype(jnp.float32), k.astype(jnp.float32)) * (q.shape[-1] ** -0.5)
    dist = q_pos[:, None] - k_pos[None, :]
    mask = (dist >= 0) & (k_pos[None, :] >= 0)
    if window is not None:
        mask = mask & (dist < window)
    bias = jnp.moveaxis(tab[t5_bucket(dist)].astype(jnp.float32), -1, 0).reshape(G, Hg, dist.shape[0], dist.shape[1])
    return masked_softmax(s + bias, mask, sink)


def apply_probs(p, v):
    return jnp.einsum('bghqk,bkgd->bqghd', p, v.astype(jnp.float32))


def compress(k, w):
    B, T, G, D = k.shape
    c = k.astype(jnp.float32).reshape(B, T // CMP_STRIDE, CMP_STRIDE, G, D)
    wt = jax.nn.softmax(w.astype(jnp.float32), axis=-1)
    return (jnp.einsum('bnsgd,gs->bngd', c[:, :-1], wt[:, :CMP_STRIDE])
            + jnp.einsum('bnsgd,gs->bngd', c[:, 1:], wt[:, CMP_STRIDE:]))


def cmp_to_sel_matrix(n_cmp, n_sel):
    i = np.arange(n_cmp)[:, None]
    j = np.arange(n_sel)[None, :]
    m = (i * CMP_STRIDE < (j + 1) * SEL_BLOCK) & (i * CMP_STRIDE + CMP_LEN > j * SEL_BLOCK)
    return jnp.asarray(m, jnp.float32)


def nsa_context(kvn, cmp_w):
    B, Tk = kvn.shape[:2]
    ck = compress(kvn[:, :, 0], cmp_w[0])
    cv = compress(kvn[:, :, 1], cmp_w[1])
    n_cmp = Tk // CMP_STRIDE - 1
    n_sel = Tk // SEL_BLOCK
    c_end = jnp.arange(n_cmp, dtype=jnp.int32) * CMP_STRIDE + (CMP_LEN - 1)

    def to_sel_blocks(a):
        return a.reshape(B, n_sel, SEL_BLOCK, NSA_KV_HEADS, HEAD_DIM).transpose(0, 3, 1, 2, 4)

    return (ck, cv, c_end, to_sel_blocks(kvn[:, :, 2]), to_sel_blocks(kvn[:, :, 3]), cmp_to_sel_matrix(n_cmp, n_sel))


def nsa_selected(qn, p_cmp, ks_b, vs_b, c2s, q_pos, tab_n):
    B, Tq, G, Hg, D = qn.shape
    n_sel = ks_b.shape[2]
    n_top = min(N_SELECT, n_sel)
    imp = jnp.einsum('bghqc,cs->bgqs', p_cmp, c2s)
    j = jnp.arange(n_sel, dtype=jnp.int32)[None, :]
    cur = (q_pos // SEL_BLOCK)[:, None]
    avail = j * SEL_BLOCK <= q_pos[:, None]
    forced = (j == 0) | (j == cur) | (j == cur - 1)
    score = jnp.where(avail, jnp.where(forced, FORCE, imp), NEG)
    top, idx = lax.top_k(score, n_top)
    valid = top > 0.5 * NEG
    take = jax.vmap(jax.vmap(lambda blocks, ix: blocks[ix]))
    kb = take(ks_b, idx).astype(jnp.float32)
    vb = take(vs_b, idx).astype(jnp.float32)
    pos = idx[..., None] * SEL_BLOCK + jnp.arange(SEL_BLOCK, dtype=jnp.int32)
    dist = q_pos[:, None, None] - pos
    mask = valid[..., None] & (dist >= 0)
    s = jnp.einsum('bqghd,bgqnjd->bghqnj', qn.astype(jnp.float32), kb) * (D ** -0.5)
    tab_g = tab_n.reshape(N_BUCKETS, G, Hg).astype(jnp.float32)
    bias = tab_g[t5_bucket(dist), jnp.arange(G)[:, None, None, None]]
    s = s + jnp.moveaxis(bias, -1, 2)
    m = n_top * SEL_BLOCK
    p = masked_softmax(s.reshape(B, G, Hg, Tq, m), mask.reshape(B, G, 1, Tq, m))
    return jnp.einsum('bghqm,bgqmd->bqghd', p, vb.reshape(B, G, Tq, m, D))


def split_even(u):
    B, T = u.shape[:2]
    q, kv, kvw, gl, dq, dkv = split_cols(u, EVEN_SIZES)
    return (q.reshape(B, T, NSA_KV_HEADS, NSA_GROUP, HEAD_DIM),
            kv.reshape(B, T, 4, NSA_KV_HEADS, HEAD_DIM),
            kvw.reshape(B, T, 2, NSA_KV_HEADS, HEAD_DIM),
            jax.nn.sigmoid(gl.astype(jnp.float32)).reshape(B, T, NSA_KV_HEADS, NSA_GROUP, 3),
            dq.reshape(B, T, DIFF_HEADS, 2, HEAD_DIM),
            dkv.reshape(B, T, 2, DIFF_HEADS, DIFF_VDIM))


def even_block(qn, gates, dq, q_pos, nctx, kw, vw, w_pos, dk, dv, k_pos, tab, lam, lam_init, subln):
    ck, cv, c_end, ks_b, vs_b, c2s = nctx
    B, Tq = qn.shape[:2]
    tab_n = tab[:, :NSA_HEADS]
    tab_d = tab[:, NSA_HEADS:NSA_HEADS + DIFF_HEADS]
    p_cmp = dense_probs(qn, ck, q_pos, c_end, tab_n)
    o_cmp = apply_probs(p_cmp, cv)
    o_sel = nsa_selected(qn, p_cmp, ks_b, vs_b, c2s, q_pos, tab_n)
    o_win = apply_probs(dense_probs(qn, kw, q_pos, w_pos, tab_n, window=NSA_WINDOW), vw)
    o_nsa = gates[..., 0:1] * o_cmp + gates[..., 1:2] * o_sel + gates[..., 2:3] * o_win
    p1 = dense_probs(dq[:, :, :, 0:1], dk[..., :HEAD_DIM], q_pos, k_pos, tab_d)
    p2 = dense_probs(dq[:, :, :, 1:2], dk[..., HEAD_DIM:], q_pos, k_pos, tab_d)
    o_diff = apply_probs(p1 - lam * p2, dv)[:, :, :, 0]
    o_diff = rmsnorm(o_diff, subln, SUBLN_EPS) * (1.0 - lam_init)
    return jnp.concatenate([o_nsa.reshape(B, Tq, -1), o_diff.reshape(B, Tq, -1)], axis=-1)


def even_prompt(h, w_in, cmp_w, tab, lam, lam_init, subln):
    B, T, _ = h.shape
    qn, kvn, kvw, gates, dq, dkv = split_even(h @ w_in)
    nctx = nsa_context(kvn, cmp_w)
    kwp = jnp.pad(kvw, ((0, 0), (NSA_WINDOW, 0), (0, 0), (0, 0), (0, 0)))
    dk, dv = dkv[:, :, 0], dkv[:, :, 1]
    k_pos = jnp.arange(T, dtype=jnp.int32)
    n_blk = T // Q_BLOCK

    def to_blocks(a):
        return jnp.swapaxes(a.reshape(B, n_blk, Q_BLOCK, *a.shape[2:]), 0, 1)

    def one_block(xs):
        c, qn_b, g_b, dq_b = xs
        start = c * Q_BLOCK
        q_pos = start + jnp.arange(Q_BLOCK, dtype=jnp.int32)
        wctx = lax.dynamic_slice_in_dim(kwp, start, Q_BLOCK + NSA_WINDOW, axis=1)
        w_pos = start - NSA_WINDOW + jnp.arange(Q_BLOCK + NSA_WINDOW, dtype=jnp.int32)
        return even_block(qn_b, g_b, dq_b, q_pos, nctx, wctx[:, :, 0], wctx[:, :, 1], w_pos, dk, dv, k_pos, tab, lam, lam_init, subln)

    out = lax.map(one_block, (jnp.arange(n_blk, dtype=jnp.int32), to_blocks(qn), to_blocks(gates), to_blocks(dq)))
    out = jnp.swapaxes(out, 0, 1).reshape(B, T, EVEN_MIX)
    win = kvw[:, T - min(NSA_WINDOW, T):]
    return out, kvn, dkv, win


def even_sample(h, nsa_pool, diff_pool, win_buf, page_table, w_in, cmp_w, tab, lam, lam_init, subln):
    B, T, _ = h.shape
    qn, kvn, kvw, gates, dq, dkv = split_even(h @ w_in)
    past = page_table.shape[1] * nsa_pool.shape[1]
    total = past + T
    tk = -(-total // SEL_BLOCK) * SEL_BLOCK
    wlen = win_buf.shape[1]
    q_pos = past + jnp.arange(T, dtype=jnp.int32)
    k_pos = jnp.arange(tk, dtype=jnp.int32)
    w_pos = past - wlen + jnp.arange(wlen + T, dtype=jnp.int32)

    def pad_keys(a):
        return jnp.pad(a, ((0, tk - total),) + ((0, 0),) * (a.ndim - 1))

    def one_seq(xs):
        qn_s, g_s, dq_s, kvn_s, kvw_s, dkv_s, prow, wbuf = xs
        nsa_past = nsa_pool[prow].reshape(past, *nsa_pool.shape[2:])
        nsa_full = pad_keys(jnp.concatenate([nsa_past, kvn_s.astype(nsa_pool.dtype)], axis=0))[None]
        diff_past = diff_pool[prow].reshape(past, *diff_pool.shape[2:])
        diff_full = pad_keys(jnp.concatenate([diff_past, dkv_s.astype(diff_pool.dtype)], axis=0))[None]
        nctx = nsa_context(nsa_full, cmp_w)
        wctx = jnp.concatenate([wbuf, kvw_s.astype(wbuf.dtype)], axis=0)[None]
        out = even_block(qn_s[None], g_s[None], dq_s[None], q_pos, nctx, wctx[:, :, 0], wctx[:, :, 1], w_pos,
                         diff_full[:, :, 0], diff_full[:, :, 1], k_pos, tab, lam, lam_init, subln)
        return out[0]

    out = lax.map(one_seq, (qn, gates, dq, kvn, kvw, dkv, page_table, win_buf))
    new_win = jnp.concatenate([win_buf, kvw.astype(win_buf.dtype)], axis=1)[:, T:]
    return out, kvn, dkv, new_win


def rwkv_mix(ur, shift0, wkv0, rw):
    mu, w0, w2, a0, a2, g2, k_k, k_a, r_k, ln_w, ln_b = [p.astype(jnp.float32) for p in rw]
    B, T, _ = ur.shape
    u = ur.astype(jnp.float32)
    prev = jnp.concatenate([shift0[:, None].astype(jnp.float32), u[:, :-1]], axis=1)
    xs = u + (prev - u) * mu
    r, k, v, xw, xa, xg = split_cols(xs, RWKV_SIZES)
    wlog = -jax.nn.softplus(-(w0 + jnp.tanh(xw) @ w2)) - 0.5
    decay = jnp.exp(-jnp.exp(wlog))
    a = jax.nn.sigmoid(a0 + xa @ a2)
    g = jax.nn.sigmoid(xg) @ g2

    def heads(t):
        return t.reshape(B, T, RWKV_HEADS, RWKV_N)

    kk = heads(k * k_k)
    kk = kk / jnp.maximum(jnp.sqrt(jnp.sum(kk * kk, axis=-1, keepdims=True)), 1e-12)
    k = k * (1.0 + (a - 1.0) * k_a)
    r4, k4, v4, w4, a4 = heads(r), heads(k), heads(v), heads(decay), heads(a)

    def step(S, inp):
        rt, wt, kt, vt, at, bt = inp
        S = (S * wt[:, :, None, :]
             + jnp.einsum('bhvk,bhk->bhv', S, at)[..., None] * bt[:, :, None, :]
             + vt[..., None] * kt[:, :, None, :])
        return S, jnp.einsum('bhvk,bhk->bhv', S, rt)

    seq = tuple(jnp.swapaxes(t, 0, 1) for t in (r4, w4, k4, v4, -kk, kk * a4))
    S, y = lax.scan(step, wkv0.astype(jnp.float32), seq)
    y = jnp.swapaxes(y, 0, 1)
    mean = jnp.mean(y, axis=-1, keepdims=True)
    var = jnp.mean(jnp.square(y - mean), axis=-1, keepdims=True)
    y = (y - mean) * lax.rsqrt(var + RWKV_GN_EPS) * ln_w.reshape(RWKV_HEADS, RWKV_N) + ln_b.reshape(RWKV_HEADS, RWKV_N)
    y = y + jnp.sum(r4 * k4 * r_k, axis=-1, keepdims=True) * v4
    y = y.reshape(B, T, RWKV_WIDTH) * g
    return y, S, ur[:, -1]


def odd_mixer(h, swa_buf, wkv0, shift0, pos0, w_in, sinks, tab, rw):
    B, T, _ = h.shape
    q, kv, ur = split_cols(h @ w_in, ODD_SIZES)
    q = q.reshape(B, T, SWA_KV_HEADS, SWA_GROUP, HEAD_DIM)
    kv = kv.reshape(B, T, 2, SWA_KV_HEADS, HEAD_DIM)
    wlen = swa_buf.shape[1]
    ctx = jnp.concatenate([swa_buf.astype(kv.dtype), kv], axis=1)
    ctx_pos = pos0 - wlen + jnp.arange(wlen + T, dtype=jnp.int32)
    qb = min(Q_BLOCK, T)
    n_blk = T // qb
    idx = np.arange(n_blk)[:, None] * qb + np.arange(qb + wlen)[None, :]
    kb = ctx[:, idx]
    kpos_b = ctx_pos[idx]
    qpos_b = (pos0 + jnp.arange(T, dtype=jnp.int32)).reshape(n_blk, qb)
    q_b = q.reshape(B, n_blk, qb, SWA_KV_HEADS, SWA_GROUP, HEAD_DIM)
    tab_s = tab[:, :SWA_HEADS]
    sink = sinks.astype(jnp.float32).reshape(SWA_KV_HEADS, SWA_GROUP, 1, 1)

    def band(qq, kk, vv, qp, kp):
        return apply_probs(dense_probs(qq, kk, qp, kp, tab_s, window=SWA_WINDOW, sink=sink), vv)

    o = jax.vmap(band, in_axes=(1, 1, 1, 0, 0), out_axes=1)(q_b, kb[:, :, :, 0], kb[:, :, :, 1], qpos_b, kpos_b)
    o_swa = o.reshape(B, T, SWA_HEADS * HEAD_DIM)
    o_rwkv, wkv, shift = rwkv_mix(ur, shift0, wkv0, rw)
    return jnp.concatenate([o_swa, o_rwkv], axis=-1), ctx[:, T:], wkv, shift


def setup_inputs(seed: int = 0) -> dict:
    key = jax.random.key(seed)
    ks = iter(jax.random.split(key, 48))
    f32 = jnp.float32

    def nrm(shape, scale=1.0):
        return jax.random.normal(next(ks), shape, f32) * scale

    def gain(shape):
        return 1.0 + nrm(shape, 0.05)

    n_pages = PAST_LEN // PAGE_SIZE
    n_used = DEC_BATCH * n_pages
    n_pool = n_used + max(1, n_used // 4)
    inputs = {}
    inputs['x_prompt'] = nrm((BATCH, SEQ, D_MODEL))
    inputs['x_sample'] = nrm((DEC_BATCH, DEC_SEQ, D_MODEL))
    inputs['cache_nsa_kv'] = nrm((N_EVEN, n_pool, PAGE_SIZE, 4, NSA_KV_HEADS, HEAD_DIM))
    inputs['cache_diff_kv'] = nrm((N_EVEN, n_pool, PAGE_SIZE, 2, DIFF_HEADS, DIFF_VDIM))
    inputs['cache_nsa_win'] = nrm((N_EVEN, DEC_BATCH, min(NSA_WINDOW, PAST_LEN), 2, NSA_KV_HEADS, HEAD_DIM))
    inputs['cache_swa'] = nrm((N_ODD, DEC_BATCH, min(SWA_WINDOW, PAST_LEN), 2, SWA_KV_HEADS, HEAD_DIM))
    inputs['state_rwkv_wkv'] = nrm((N_ODD, DEC_BATCH, RWKV_HEADS, RWKV_N, RWKV_N), 0.3)
    inputs['state_rwkv_shift'] = nrm((N_ODD, DEC_BATCH, RWKV_PROJ))
    inputs['page_table'] = jax.random.permutation(next(ks), n_pool)[:n_used].reshape(DEC_BATCH, n_pages).astype(jnp.int32)
    inputs['rel_bias'] = nrm((N_BUCKETS, BIAS_HEADS), 0.3)
    inputs['norm_mix'] = gain((DEPTH, D_MODEL))
    inputs['norm_ffn'] = gain((DEPTH, D_MODEL))
    inputs['norm_final'] = gain((D_MODEL,))
    inputs['w_in_even'] = nrm((N_EVEN, D_MODEL, EVEN_IN), D_MODEL ** -0.5)
    inputs['w_out_even'] = nrm((N_EVEN, EVEN_MIX, D_MODEL), EVEN_MIX ** -0.5)
    inputs['nsa_cmp_w'] = nrm((N_EVEN, 2, NSA_KV_HEADS, CMP_LEN), 0.5)
    inputs['diff_lambda'] = nrm((N_EVEN, 4, HEAD_DIM), 0.1)
    inputs['diff_subln'] = gain((N_EVEN, DIFF_VDIM))
    inputs['w_in_odd'] = nrm((N_ODD, D_MODEL, ODD_IN), D_MODEL ** -0.5)
    inputs['w_out_odd'] = nrm((N_ODD, ODD_MIX, D_MODEL), ODD_MIX ** -0.5)
    inputs['swa_sinks'] = nrm((N_ODD, SWA_HEADS), 0.5)
    inputs['rwkv_mu'] = jax.random.uniform(next(ks), (N_ODD, RWKV_PROJ), f32, 0.0, 1.0)
    inputs['rwkv_w0'] = jax.random.uniform(next(ks), (N_ODD, RWKV_WIDTH), f32, -6.0, -1.0)
    inputs['rwkv_w2'] = nrm((N_ODD, DECAY_LORA, RWKV_WIDTH), 0.1)
    inputs['rwkv_a0'] = nrm((N_ODD, RWKV_WIDTH), 0.1)
    inputs['rwkv_a2'] = nrm((N_ODD, AAA_LORA, RWKV_WIDTH), 0.1)
    inputs['rwkv_g2'] = nrm((N_ODD, GATE_LORA, RWKV_WIDTH), GATE_LORA ** -0.5)
    inputs['rwkv_k_k'] = 0.85 + nrm((N_ODD, RWKV_WIDTH), 0.05)
    inputs['rwkv_k_a'] = gain((N_ODD, RWKV_WIDTH))
    inputs['rwkv_r_k'] = nrm((N_ODD, RWKV_HEADS, RWKV_N), 0.1)
    inputs['rwkv_ln_w'] = gain((N_ODD, RWKV_WIDTH))
    inputs['rwkv_ln_b'] = nrm((N_ODD, RWKV_WIDTH), 0.02)
    inputs['ffn_w_gate'] = nrm((DEPTH, D_MODEL, D_FF), D_MODEL ** -0.5)
    inputs['ffn_w_up'] = nrm((DEPTH, D_MODEL, D_FF), D_MODEL ** -0.5)
    inputs['ffn_w_down'] = nrm((DEPTH, D_FF, D_MODEL), D_FF ** -0.5)
    return inputs


def reference(x_prompt, x_sample, cache_nsa_kv, cache_diff_kv, cache_nsa_win, cache_swa, state_rwkv_wkv,
              state_rwkv_shift, page_table, rel_bias, norm_mix, norm_ffn, norm_final, w_in_even, w_out_even,
              nsa_cmp_w, diff_lambda, diff_subln, w_in_odd, w_out_odd, swa_sinks, rwkv_mu, rwkv_w0, rwkv_w2,
              rwkv_a0, rwkv_a2, rwkv_g2, rwkv_k_k, rwkv_k_a, rwkv_r_k, rwkv_ln_w, rwkv_ln_b,
              ffn_w_gate, ffn_w_up, ffn_w_down):
    xp, xs = x_prompt, x_sample
    past = page_table.shape[1] * cache_nsa_kv.shape[2]
    nsa_p, nsa_s, diff_p, diff_s, win_p, win_s = [], [], [], [], [], []
    swa_p, swa_s, wkv_p, wkv_s, sh_p, sh_s = [], [], [], [], [], []
    for l in range(DEPTH):
        hp = rmsnorm(xp, norm_mix[l])
        hs = rmsnorm(xs, norm_mix[l])
        if l % 2 == 0:
            e = l // 2
            lam_init = 0.8 - 0.6 * math.exp(-0.3 * l)
            lq = diff_lambda[e].astype(jnp.float32)
            lam = jnp.exp(jnp.sum(lq[0] * lq[1])) - jnp.exp(jnp.sum(lq[2] * lq[3])) + lam_init
            mp, a1, a2, a3 = even_prompt(hp, w_in_even[e], nsa_cmp_w[e], rel_bias, lam, lam_init, diff_subln[e])
            ms, b1, b2, b3 = even_sample(hs, cache_nsa_kv[e], cache_diff_kv[e], cache_nsa_win[e], page_table,
                                         w_in_even[e], nsa_cmp_w[e], rel_bias, lam, lam_init, diff_subln[e])
            nsa_p.append(a1); diff_p.append(a2); win_p.append(a3)
            nsa_s.append(b1); diff_s.append(b2); win_s.append(b3)
            w_out = w_out_even[e]
        else:
            o = l // 2
            rw = (rwkv_mu[o], rwkv_w0[o], rwkv_w2[o], rwkv_a0[o], rwkv_a2[o], rwkv_g2[o], rwkv_k_k[o],
                  rwkv_k_a[o], rwkv_r_k[o], rwkv_ln_w[o], rwkv_ln_b[o])
            bsz, tp = xp.shape[0], xp.shape[1]
            buf0 = jnp.zeros((bsz, min(SWA_WINDOW, tp), 2, SWA_KV_HEADS, HEAD_DIM), hp.dtype)
            wkv0 = jnp.zeros((bsz, RWKV_HEADS, RWKV_N, RWKV_N), jnp.float32)
            sh0 = jnp.zeros((bsz, RWKV_PROJ), hp.dtype)
            mp, a1, a2, a3 = odd_mixer(hp, buf0, wkv0, sh0, 0, w_in_odd[o], swa_sinks[o], rel_bias, rw)
            ms, b1, b2, b3 = odd_mixer(hs, cache_swa[o], state_rwkv_wkv[o], state_rwkv_shift[o], past,
                                       w_in_odd[o], swa_sinks[o], rel_bias, rw)
            swa_p.append(a1); wkv_p.append(a2); sh_p.append(a3)
            swa_s.append(b1); wkv_s.append(b2); sh_s.append(b3)
            w_out = w_out_odd[o]
        xp = xp + mp.astype(xp.dtype) @ w_out
        xs = xs + ms.astype(xs.dtype) @ w_out
        xp = xp + swiglu(rmsnorm(xp, norm_ffn[l]), ffn_w_gate[l], ffn_w_up[l], ffn_w_down[l])
        xs = xs + swiglu(rmsnorm(xs, norm_ffn[l]), ffn_w_gate[l], ffn_w_up[l], ffn_w_down[l])
    y_prompt = rmsnorm(xp, norm_final)
    y_sample = rmsnorm(xs, norm_final)
    return (y_prompt, y_sample, jnp.stack(nsa_p), jnp.stack(nsa_s), jnp.stack(diff_p), jnp.stack(diff_s),
            jnp.stack(win_p), jnp.stack(win_s), jnp.stack(swa_p), jnp.stack(swa_s),
            jnp.stack(wkv_p), jnp.stack(wkv_s), jnp.stack(sh_p), jnp.stack(sh_s))
```

```python
import functools
import math

import numpy as np
import jax
import jax.numpy as jnp
from jax import lax
from jax.experimental import pallas as pl
from jax.experimental.pallas import tpu as pltpu

F32 = jnp.float32
BF16 = jnp.bfloat16
HI = lax.Precision.HIGHEST

HEAD_DIM = 64
NSA_KV_HEADS = 2
NSA_GROUP = 4
NSA_HEADS = NSA_KV_HEADS * NSA_GROUP
CMP_STRIDE = 16
CMP_LEN = 32
SEL_BLOCK = 64
N_SELECT = 16
NSA_WINDOW = 512
DIFF_HEADS = 4
DIFF_VDIM = 128
SWA_HEADS = 8
SWA_KV_HEADS = 2
SWA_GROUP = 4
SWA_WINDOW = 128
RWKV_N = 64
RWKV_HEADS = 8
RWKV_WIDTH = RWKV_N * RWKV_HEADS
DECAY_LORA = 32
AAA_LORA = 32
GATE_LORA = 96
LORA_PAD = 256
N_BUCKETS = 32
MAX_DISTANCE = 128
NORM_EPS = 1e-6
SUBLN_EPS = 1e-5
RWKV_GN_EPS = 64e-5
NEG = -1e30
FORCE = 1e6
REMOVED = -3e38

LANES = 128
VMEM_LIMIT_BYTES = 56 * 1024 * 1024

TQ = 256
ROW_TILE = 512
CMP_PAD = 16


def _cparams(sem):
    return pltpu.CompilerParams(dimension_semantics=sem, vmem_limit_bytes=VMEM_LIMIT_BYTES)


def _const_spec(shape):
    n = len(shape)
    return pl.BlockSpec(shape, lambda *_: (0,) * n)


def _smem_spec():
    return pl.BlockSpec(memory_space=pltpu.SMEM)


def _dot(a, b, precision=None):
    return jnp.dot(a, b, preferred_element_type=F32, precision=precision)


def _dot_nt(a, b, precision=None):
    return lax.dot_general(a, b, (((1,), (1,)), ((), ())), preferred_element_type=F32, precision=precision)


def _dot_tn(a, b, precision=None):
    return lax.dot_general(a, b, (((0,), (0,)), ((), ())), preferred_element_type=F32, precision=precision)


def _norm_proj_body(x_ref, g_ref, w_ref, *out_refs, groups):
    x = x_ref[...]
    h = (x * lax.rsqrt(jnp.mean(x * x, axis=-1, keepdims=True) + NORM_EPS)) * g_ref[...]
    hb = h.astype(BF16)
    i = 0
    for off, wd, dts in groups:
        r = _dot(hb, w_ref[:, off:off + wd])
        for dt in dts:
            out_refs[i][...] = r.astype(dt)
            i += 1


def norm_proj(x2d, g, w_bf16, groups):
    m, d = x2d.shape
    tm = min(ROW_TILE, m)
    out_shape, out_specs = [], []
    for _, wd, dts in groups:
        for dt in dts:
            out_shape.append(jax.ShapeDtypeStruct((m, wd), dt))
            out_specs.append(pl.BlockSpec((tm, wd), lambda i: (i, 0)))
    return pl.pallas_call(
        functools.partial(_norm_proj_body, groups=groups),
        out_shape=out_shape,
        grid=(m // tm,),
        in_specs=[pl.BlockSpec((tm, d), lambda i: (i, 0)), _const_spec((1, d)), _const_spec(w_bf16.shape)],
        out_specs=out_specs,
        compiler_params=_cparams(("parallel",)),
        name="norm_proj",
    )(x2d, g.reshape(1, d), w_bf16)


def _out_proj_body(x_ref, a_ref, b_ref, wa_ref, wb_ref, o_ref):
    acc = _dot(a_ref[...].astype(BF16), wa_ref[...]) + _dot(b_ref[...].astype(BF16), wb_ref[...])
    o_ref[...] = x_ref[...] + acc


def out_proj(x2d, a, b, wa, wb):
    m, d = x2d.shape
    tm = min(ROW_TILE, m)
    return pl.pallas_call(
        _out_proj_body,
        out_shape=jax.ShapeDtypeStruct((m, d), F32),
        grid=(m // tm,),
        in_specs=[pl.BlockSpec((tm, d), lambda i: (i, 0)),
                  pl.BlockSpec((tm, a.shape[1]), lambda i: (i, 0)),
                  pl.BlockSpec((tm, b.shape[1]), lambda i: (i, 0)),
                  _const_spec(wa.shape), _const_spec(wb.shape)],
        out_specs=pl.BlockSpec((tm, d), lambda i: (i, 0)),
        compiler_params=_cparams(("parallel",)),
        name="out_proj",
    )(x2d, a, b, wa, wb)


def _ffn_body(x_ref, g_ref, gf_ref, wg_ref, wu_ref, wd_ref, o_ref, h_sc, acc_sc, *, final_norm):
    f = pl.program_id(1)

    @pl.when(f == 0)
    def _():
        x = x_ref[...]
        h = (x * lax.rsqrt(jnp.mean(x * x, axis=-1, keepdims=True) + NORM_EPS)) * g_ref[...]
        h_sc[...] = h.astype(BF16)
        acc_sc[...] = jnp.zeros_like(acc_sc)

    hb = h_sc[...]
    gate = _dot(hb, wg_ref[...])
    up = _dot(hb, wu_ref[...])
    act = (gate * jax.nn.sigmoid(gate)) * up
    acc_sc[...] += _dot(act.astype(BF16), wd_ref[...])

    @pl.when(f == pl.num_programs(1) - 1)
    def _():
        y = x_ref[...] + acc_sc[...]
        if final_norm:
            y = (y * lax.rsqrt(jnp.mean(y * y, axis=-1, keepdims=True) + NORM_EPS)) * gf_ref[...]
        o_ref[...] = y


def ffn(x2d, g, g_final, wg, wu, wd, final_norm):
    m, d = x2d.shape
    dff = wg.shape[1]
    tm = min(ROW_TILE, m)
    tf = dff // 2
    return pl.pallas_call(
        functools.partial(_ffn_body, final_norm=final_norm),
        out_shape=jax.ShapeDtypeStruct((m, d), F32),
        grid=(m // tm, dff // tf),
        in_specs=[pl.BlockSpec((tm, d), lambda i, f: (i, 0)), _const_spec((1, d)), _const_spec((1, d)),
                  pl.BlockSpec((d, tf), lambda i, f: (0, f)), pl.BlockSpec((d, tf), lambda i, f: (0, f)),
                  pl.BlockSpec((tf, d), lambda i, f: (f, 0))],
        out_specs=pl.BlockSpec((tm, d), lambda i, f: (i, 0)),
        scratch_shapes=[pltpu.VMEM((tm, d), BF16), pltpu.VMEM((tm, d), F32)],
        compiler_params=_cparams(("parallel", "arbitrary")),
        name="ffn",
    )(x2d, g.reshape(1, d), g_final.reshape(1, d), wg, wu, wd)


def _t5_bucket(dist):
    n = jnp.maximum(dist, 0)
    max_exact = N_BUCKETS // 2
    nf = jnp.maximum(n, 1).astype(F32)
    large = max_exact + (jnp.log(nf / max_exact) / math.log(MAX_DISTANCE / max_exact)
                         * (N_BUCKETS - max_exact)).astype(jnp.int32)
    large = jnp.minimum(large, N_BUCKETS - 1)
    return jnp.where(n < max_exact, n, large)


def _bias_table(tab, dist, valid):
    dist = jnp.asarray(dist, jnp.int32)
    b = jnp.moveaxis(tab[_t5_bucket(dist)].astype(F32), -1, 0)
    return jnp.where(jnp.asarray(valid)[None], b, NEG)


def _cmp_to_sel(n_cmp, n_sel, rows, cols):
    i = np.arange(n_cmp)[:, None]
    j = np.arange(n_sel)[None, :]
    m = (i * CMP_STRIDE < (j + 1) * SEL_BLOCK) & (i * CMP_STRIDE + CMP_LEN > j * SEL_BLOCK)
    out = np.zeros((rows, cols), np.float32)
    out[CMP_PAD:CMP_PAD + n_cmp, :n_sel] = m
    return out


def _block_expand(n_chunks, n_blocks, chunk_keys):
    c = np.arange(n_chunks)[:, None, None]
    j = np.arange(n_blocks)[None, :, None]
    l = np.arange(chunk_keys)[None, None, :]
    return (j == (c * chunk_keys + l) // SEL_BLOCK).astype(np.float32)


def _place_half(x64, half):
    z = jnp.zeros_like(x64)
    return jnp.concatenate([x64, z], axis=1) if half == 0 else jnp.concatenate([z, x64], axis=1)


def _masked_softmax_parts(parts):
    m = None
    for s, msk in parts:
        mm = jnp.max(jnp.where(msk, s, NEG), axis=-1, keepdims=True)
        m = mm if m is None else jnp.maximum(m, mm)
    ps, den = [], None
    for s, msk in parts:
        p = jnp.where(msk, jnp.exp(jnp.where(msk, s, NEG) - m), 0.0)
        ps.append(p)
        d = jnp.sum(p, axis=-1, keepdims=True)
        den = d if den is None else den + d
    inv = 1.0 / jnp.maximum(den, 1e-30)
    return [p * inv for p in ps]


def _select_blocks(score, n_top):
    lane = lax.broadcasted_iota(jnp.int32, score.shape, 1)
    big = score.shape[1]
    sel = jnp.zeros(score.shape, F32)
    sc = score
    for _ in range(n_top):
        m = jnp.max(sc, axis=-1, keepdims=True)
        idx = jnp.min(jnp.where(sc == m, lane, big), axis=-1, keepdims=True)
        hit = lane == idx
        sel = jnp.where(hit & (m > 0.5 * NEG), 1.0, sel)
        sc = jnp.where(hit, REMOVED, sc)
    return sel


def _block_scores(imp, q_pos):
    j = lax.broadcasted_iota(jnp.int32, imp.shape, 1)
    cur = q_pos // SEL_BLOCK
    avail = j * SEL_BLOCK <= q_pos
    forced = (j == 0) | (j == cur) | (j == cur - 1)
    return jnp.where(avail, jnp.where(forced, FORCE, imp), NEG)


def _compress_body(x_ref, w1_ref, w2_ref, o_ref, *, nblk):
    x = x_ref[0].reshape(nblk, CMP_STRIDE, LANES)
    p1 = (x * w1_ref[0][None]).sum(axis=1)
    p2 = (x * w2_ref[0][None]).sum(axis=1)
    ck = p1 + pltpu.roll(p2, nblk - 1, 0)
    row = lax.broadcasted_iota(jnp.int32, ck.shape, 0)
    o_ref[0, 0] = jnp.zeros(o_ref.shape[2:], F32)
    o_ref[0, 0, CMP_PAD:CMP_PAD + nblk, :] = jnp.where(row < nblk - 1, ck, 0.0)


def compress_prompt(kv, w1, w2, cp):
    b, t, _ = kv.shape
    nblk = t // CMP_STRIDE
    return pl.pallas_call(
        functools.partial(_compress_body, nblk=nblk),
        out_shape=jax.ShapeDtypeStruct((b, 2, cp, LANES), F32),
        grid=(b, 2),
        in_specs=[pl.BlockSpec((1, t, LANES), lambda i, k: (i, 0, k)),
                  pl.BlockSpec((1, CMP_STRIDE, LANES), lambda i, k: (k, 0, 0)),
                  pl.BlockSpec((1, CMP_STRIDE, LANES), lambda i, k: (k, 0, 0))],
        out_specs=pl.BlockSpec((1, 1, cp, LANES), lambda i, k: (i, k, 0, 0)),
        compiler_params=_cparams(("parallel", "parallel")),
        name="compress_prompt",
    )(kv, w1, w2)


def _flash_update(h, s, vblk, m_sc, l_sc, acc_sc):
    m_old = m_sc[h]
    m_new = jnp.maximum(m_old, jnp.max(s, axis=-1, keepdims=True))
    alpha = jnp.exp(m_old - m_new)
    p = jnp.exp(s - m_new)
    l_sc[h] = alpha * l_sc[h] + jnp.sum(p, axis=-1, keepdims=True)
    acc_sc[h] = alpha * acc_sc[h] + _dot(p.astype(BF16), vblk)
    m_sc[h] = m_new


def _nsa_prompt_body(c31_ref, q_ref, gl_ref, ck_ref, cv_ref, c2s_ref, selk_ref, selv_ref,
                     wk0_ref, wk1_ref, wk2_ref, wv0_ref, wv1_ref, wv2_ref,
                     selb_ref, winb_ref, cnear_ref, e3_ref, o_ref, m_sc, l_sc, acc_sc, *, cp):
    qi = pl.program_id(1)
    tq = q_ref.shape[1]
    near = tq // CMP_STRIDE + CMP_PAD
    near0 = pl.multiple_of(qi * (tq // CMP_STRIDE), tq // CMP_STRIDE)
    q = q_ref[0] * (HEAD_DIM ** -0.5)
    gates = jax.nn.sigmoid(gl_ref[0])
    ck = ck_ref[0, 0]
    cv = cv_ref[0, 0]
    ck_near = ck_ref[0, 0, pl.ds(near0, near), :]
    cv_near = cv_ref[0, 0, pl.ds(near0, near), :]
    c2s = c2s_ref[...]
    c2s_near = c2s_ref[pl.ds(near0, near), :]
    kcat = jnp.concatenate([wk0_ref[0], wk1_ref[0], wk2_ref[0]], axis=0)
    vcat = jnp.concatenate([wv0_ref[0], wv1_ref[0], wv2_ref[0]], axis=0)
    q_pos = qi * tq + lax.broadcasted_iota(jnp.int32, (tq, 1), 0)
    cp_idx = lax.broadcasted_iota(jnp.int32, (tq, cp), 1)
    far_mask = (cp_idx >= CMP_PAD) & (cp_idx < near0)
    near_lane = lax.broadcasted_iota(jnp.int32, (tq, near), 1)
    near_ok = (near_lane >= CMP_PAD) | (qi > 0)
    wcol = lax.broadcasted_iota(jnp.int32, (tq, 3 * tq), 1)
    win_ok = wcol >= (2 - qi) * tq

    pieces = []
    for g in range(NSA_KV_HEADS):
        qz, qzb = [], []
        for h in range(NSA_GROUP):
            hh = g * NSA_GROUP + h
            z = _place_half(q[:, hh * HEAD_DIM:(hh + 1) * HEAD_DIM], g)
            qz.append(z)
            qzb.append(z.astype(BF16))

        o_cmp = []
        psum_far = jnp.zeros((tq, cp), F32)
        psum_near = jnp.zeros((tq, near), F32)
        for h in range(NSA_GROUP):
            hh = g * NSA_GROUP + h
            s_far = _dot_nt(qz[h], ck, HI) + c31_ref[hh]
            nb = cnear_ref[hh]
            s_near = _dot_nt(qz[h], ck_near, HI) + nb
            p_far, p_near = _masked_softmax_parts([(s_far, far_mask), (s_near, (nb > 0.5 * NEG) & near_ok)])
            o_cmp.append(_dot(p_far, cv, HI) + _dot(p_near, cv_near, HI))
            psum_far = psum_far + p_far
            psum_near = psum_near + p_near
        imp = _dot(psum_far, c2s, HI) + _dot(psum_near, c2s_near, HI)
        sel = _select_blocks(_block_scores(imp, q_pos), N_SELECT).astype(BF16)

        for h in range(NSA_GROUP):
            m_sc[h] = jnp.full((tq, 1), REMOVED, F32)
            l_sc[h] = jnp.zeros((tq, 1), F32)
            acc_sc[h] = jnp.zeros((tq, LANES), F32)

        def sel_step(kt, bias_kind):
            k0 = pl.multiple_of(kt * tq, tq)
            kblk = selk_ref[0, pl.ds(k0, tq), :]
            vblk = selv_ref[0, pl.ds(k0, tq), :]
            madd = (_dot(sel, e3_ref[kt]) - 1.0) * 1e30
            for h in range(NSA_GROUP):
                hh = g * NSA_GROUP + h
                s = _dot_nt(qzb[h], kblk)
                if bias_kind is None:
                    s = s + (madd + c31_ref[hh])
                else:
                    s = s + (madd + selb_ref[hh, bias_kind])
                _flash_update(h, s, vblk, m_sc, l_sc, acc_sc)

        def far_body(kt, carry):
            sel_step(kt, None)
            return carry

        lax.fori_loop(0, jnp.maximum(qi - 1, 0), far_body, 0)

        @pl.when(qi >= 1)
        def _():
            sel_step(qi - 1, 0)

        sel_step(qi, 1)

        for h in range(NSA_GROUP):
            hh = g * NSA_GROUP + h
            wb = winb_ref[hh]
            s = _dot_nt(qzb[h], kcat) + wb
            msk = (wb > 0.5 * NEG) & win_ok
            (p,) = _masked_softmax_parts([(s, msk)])
            o_win = _dot(p.astype(BF16), vcat)
            o_sel = acc_sc[h] / l_sc[h]
            c = g * NSA_GROUP * 3 + h * 3
            o = gates[:, c:c + 1] * o_cmp[h] + gates[:, c + 1:c + 2] * o_sel + gates[:, c + 2:c + 3] * o_win
            pieces.append(o[:, g * HEAD_DIM:(g + 1) * HEAD_DIM])
    o_ref[0] = jnp.concatenate(pieces, axis=1)


def nsa_prompt(q, gl, ckv, kvb, kvwb, tabs):
    b, t, _ = q.shape
    nq = t // TQ
    cp = ckv.shape[2]

    def win_spec(back, col):
        return pl.BlockSpec((1, TQ, LANES), lambda i, j: (i, jnp.maximum(j - back, 0), col))

    return pl.pallas_call(
        functools.partial(_nsa_prompt_body, cp=cp),
        out_shape=jax.ShapeDtypeStruct((b, t, NSA_HEADS * HEAD_DIM), F32),
        grid=(b, nq),
        in_specs=[_smem_spec(),
                  pl.BlockSpec((1, TQ, NSA_HEADS * HEAD_DIM), lambda i, j: (i, j, 0)),
                  pl.BlockSpec((1, TQ, LANES), lambda i, j: (i, j, 0)),
                  pl.BlockSpec((1, 1, cp, LANES), lambda i, j: (i, 0, 0, 0)),
                  pl.BlockSpec((1, 1, cp, LANES), lambda i, j: (i, 1, 0, 0)),
                  _const_spec(tabs["c2s"].shape),
                  pl.BlockSpec((1, t, LANES), lambda i, j: (i, 0, 2)),
                  pl.BlockSpec((1, t, LANES), lambda i, j: (i, 0, 3)),
                  win_spec(2, 0), win_spec(1, 0), win_spec(0, 0),
                  win_spec(2, 1), win_spec(1, 1), win_spec(0, 1),
                  _const_spec(tabs["selb"].shape), _const_spec(tabs["winb"].shape),
                  _const_spec(tabs["cnear"].shape), _const_spec(tabs["e3"].shape)],
        out_specs=pl.BlockSpec((1, TQ, NSA_HEADS * HEAD_DIM), lambda i, j: (i, j, 0)),
        scratch_shapes=[pltpu.VMEM((NSA_GROUP, TQ, 1), F32), pltpu.VMEM((NSA_GROUP, TQ, 1), F32),
                        pltpu.VMEM((NSA_GROUP, TQ, LANES), F32)],
        compiler_params=_cparams(("parallel", "parallel")),
        name="nsa_prompt",
    )(tabs["c31"], q, gl, ckv, ckv, tabs["c2s"], kvb, kvb, kvwb, kvwb, kvwb, kvwb, kvwb, kvwb,
      tabs["selb"], tabs["winb"], tabs["cnear"], tabs["e3"])


def nsa_prompt_tables(tab_n, t):
    nq = t // TQ
    cp = t // CMP_STRIDE + LANES
    r = np.arange(TQ)[:, None]
    c = np.arange(TQ)[None, :]
    d_prev = TQ + r - c
    d_diag = r - c
    selb = jnp.stack([_bias_table(tab_n, d_prev, d_prev >= 0), _bias_table(tab_n, d_diag, d_diag >= 0)], axis=1)
    cw = np.arange(3 * TQ)[None, :]
    d_win = r + 2 * TQ - cw
    winb = _bias_table(tab_n, d_win, (d_win >= 0) & (d_win < NSA_WINDOW))
    near = TQ // CMP_STRIDE + CMP_PAD
    ln = np.arange(near)[None, :]
    d_near = r - CMP_STRIDE * ln + (CMP_PAD * CMP_STRIDE - (CMP_LEN - 1))
    cnear = _bias_table(tab_n, d_near, d_near >= 0)
    return dict(
        c31=tab_n[N_BUCKETS - 1].astype(F32),
        selb=selb, winb=winb, cnear=cnear,
        c2s=jnp.asarray(_cmp_to_sel(t // CMP_STRIDE - 1, t // SEL_BLOCK, cp, LANES)),
        e3=jnp.asarray(_block_expand(nq, LANES, TQ), BF16),
    )


def _diff_prompt_body(sc_ref, q_ref, k_ref, v_ref, bias_ref, sub_ref, o_ref, m_sc, l_sc, acc_sc):
    h = pl.program_id(1)
    qi = pl.program_id(2)
    tq = q_ref.shape[1]
    q = q_ref[0] * (HEAD_DIM ** -0.5)
    lane = lax.broadcasted_iota(jnp.int32, q.shape, 1)
    q2 = jnp.concatenate([jnp.where(lane < HEAD_DIM, q, 0.0), jnp.where(lane >= HEAD_DIM, q, 0.0)], axis=0).astype(BF16)
    c31 = sc_ref[2 + h]
    m_sc[0] = jnp.full((2 * tq, 1), REMOVED, F32)
    l_sc[0] = jnp.zeros((2 * tq, 1), F32)
    acc_sc[0] = jnp.zeros((2 * tq, LANES), F32)

    def step(kt, bias_kind):
        k0 = pl.multiple_of(kt * tq, tq)
        s = _dot_nt(q2, k_ref[0, pl.ds(k0, tq), :])
        if bias_kind is None:
            s = s + c31
        else:
            bt = bias_ref[0, bias_kind]
            s = s + jnp.concatenate([bt, bt], axis=0)
        _flash_update(0, s, v_ref[0, pl.ds(k0, tq), :], m_sc, l_sc, acc_sc)

    def far_body(kt, carry):
        step(kt, None)
        return carry

    lax.fori_loop(0, jnp.maximum(qi - 1, 0), far_body, 0)

    @pl.when(qi >= 1)
    def _():
        step(qi - 1, 0)

    step(qi, 1)
    o12 = acc_sc[0] / l_sc[0]
    o = o12[:tq] - sc_ref[0] * o12[tq:]
    y = (o * lax.rsqrt(jnp.mean(o * o, axis=-1, keepdims=True) + SUBLN_EPS)) * sub_ref[...]
    o_ref[0] = y * sc_ref[1]


def diff_prompt(dq, dkvb, scal, bias, subln):
    b, t, _ = dq.shape
    nq = t // TQ
    return pl.pallas_call(
        _diff_prompt_body,
        out_shape=jax.ShapeDtypeStruct((b, t, DIFF_HEADS * DIFF_VDIM), F32),
        grid=(b, DIFF_HEADS, nq),
        in_specs=[_smem_spec(),
                  pl.BlockSpec((1, TQ, LANES), lambda i, h, j: (i, j, h)),
                  pl.BlockSpec((1, t, LANES), lambda i, h, j: (i, 0, h)),
                  pl.BlockSpec((1, t, LANES), lambda i, h, j: (i, 0, DIFF_HEADS + h)),
                  pl.BlockSpec((1, 2, TQ, TQ), lambda i, h, j: (h, 0, 0, 0)),
                  _const_spec((1, DIFF_VDIM))],
        out_specs=pl.BlockSpec((1, TQ, LANES), lambda i, h, j: (i, j, h)),
        scratch_shapes=[pltpu.VMEM((1, 2 * TQ, 1), F32), pltpu.VMEM((1, 2 * TQ, 1), F32),
                        pltpu.VMEM((1, 2 * TQ, LANES), F32)],
        compiler_params=_cparams(("parallel", "parallel", "parallel")),
        name="diff_prompt",
    )(scal, dq, dkvb, dkvb, bias, subln.reshape(1, DIFF_VDIM))


def diff_prompt_tables(tab_d):
    r = np.arange(TQ)[:, None]
    c = np.arange(TQ)[None, :]
    d_prev = TQ + r - c
    d_diag = r - c
    return jnp.stack([_bias_table(tab_d, d_prev, d_prev >= 0), _bias_table(tab_d, d_diag, d_diag >= 0)], axis=1)


NSA_PAGES_PER_STEP = 16
DIFF_PAGES_PER_STEP = 8
NEW_ROWS = 8


def _pad_rows(x, rows):
    return jnp.concatenate([x, jnp.zeros((rows - x.shape[0], x.shape[1]), x.dtype)], axis=0)


def _nsa_dec_body(pt_ref, *refs, n_pg, n_ch, past, cpd, n_tok):
    del pt_ref
    pages = refs[:n_pg]
    (qz_ref, glr_ref, newkv_ref, wbuf_ref, neww_ref, w1_ref, w2_ref, c2s_ref, cmpb_ref, selnear_ref,
     winb_ref, c31_ref, e3_ref, rsum_ref, rexp_ref, o_ref, p1k, p2k, p1v, p2v, sc_sc, v_sc) = refs[n_pg:]
    c = pl.program_id(1)
    page_rows = pages[0].shape[1]
    blk_per_page = page_rows // CMP_STRIDE
    qz = qz_ref[0] * (HEAD_DIM ** -0.5)
    qzb = qz.astype(BF16)

    @pl.when(c == 0)
    def _():
        for ref in (p1k, p2k, p1v, p2v):
            ref[...] = jnp.zeros(ref.shape, F32)

    for i in range(n_pg):
        page = pages[i][0]
        pg = c * n_pg + i
        blk0 = pl.multiple_of(CMP_PAD + pg * blk_per_page, 8)
        xk = page[:, 0:LANES].reshape(blk_per_page, CMP_STRIDE, LANES)
        xv = page[:, LANES:2 * LANES].reshape(blk_per_page, CMP_STRIDE, LANES)
        p1k[pl.ds(blk0, blk_per_page), :] = (xk * w1_ref[0][None]).sum(axis=1)
        p2k[pl.ds(blk0, blk_per_page), :] = (xk * w2_ref[0][None]).sum(axis=1)
        p1v[pl.ds(blk0, blk_per_page), :] = (xv * w1_ref[1][None]).sum(axis=1)
        p2v[pl.ds(blk0, blk_per_page), :] = (xv * w2_ref[1][None]).sum(axis=1)
        sc_sc[c, :, i * page_rows:(i + 1) * page_rows] = _dot_nt(qzb, page[:, 2 * LANES:3 * LANES].astype(BF16))
        v_sc[pl.ds(pl.multiple_of(pg * page_rows, page_rows), page_rows), :] = page[:, 3 * LANES:4 * LANES].astype(BF16)

    @pl.when(c == n_ch - 1)
    def _():
        nk = newkv_ref[0]
        s_new = _dot_nt(qzb, _pad_rows(nk[:, 2 * LANES:3 * LANES], LANES).astype(BF16))
        v_sc[pl.ds(past, LANES), :] = _pad_rows(nk[:, 3 * LANES:4 * LANES], LANES).astype(BF16)

        ck = p1k[...] + pltpu.roll(p2k[...], cpd - 1, 0)
        cv = p1v[...] + pltpu.roll(p2v[...], cpd - 1, 0)
        cb = cmpb_ref[...]
        (p_cmp,) = _masked_softmax_parts([(_dot_nt(qz, ck, HI) + cb, cb > 0.5 * NEG)])
        o_cmp = _dot(p_cmp, cv, HI)
        imp = _dot(_dot(rsum_ref[...], p_cmp, HI), c2s_ref[...], HI)
        rows = lax.broadcasted_iota(jnp.int32, (imp.shape[0], 1), 0)
        sel = _select_blocks(_block_scores(imp, past + rows % n_tok), N_SELECT)
        sel = _dot(rexp_ref[...], sel).astype(BF16)

        c31 = c31_ref[...][:, :1]
        selnear = selnear_ref[...]
        chunk = n_pg * page_rows
        parts = []
        for ci in range(n_ch):
            s = sc_sc[ci] + (_dot(sel, e3_ref[ci]) - 1.0) * 1e30
            if ci < n_ch - 1:
                s = s + c31
            else:
                s = s + jnp.concatenate([jnp.broadcast_to(c31, (s.shape[0], chunk - LANES)), selnear[:, :LANES]], axis=1)
            parts.append(s)
        parts.append(s_new + (_dot(sel, e3_ref[n_ch][:, :LANES]) - 1.0) * 1e30 + selnear[:, LANES:])
        m = parts[0].max(axis=-1, keepdims=True)
        for s in parts[1:]:
            m = jnp.maximum(m, s.max(axis=-1, keepdims=True))
        den = jnp.zeros_like(m)
        acc = jnp.zeros((m.shape[0], LANES), F32)
        for ci, s in enumerate(parts):
            p = jnp.exp(s - m)
            den = den + p.sum(axis=-1, keepdims=True)
            acc = acc + _dot(p.astype(BF16), v_sc[ci * chunk:ci * chunk + s.shape[1], :])
        o_sel = acc / den

        wb = wbuf_ref[0]
        nw = neww_ref[0]
        wlen = wb.shape[0]
        kcat = jnp.concatenate([wb[:, :LANES], _pad_rows(nw[:, :LANES], LANES)], axis=0).astype(BF16)
        vcat = jnp.concatenate([wb[:, LANES:], _pad_rows(nw[:, LANES:], LANES)], axis=0).astype(BF16)
        wbias = winb_ref[...]
        (p_win,) = _masked_softmax_parts([(_dot_nt(qzb, kcat) + wbias, wbias > 0.5 * NEG)])
        o_win = _dot(p_win.astype(BF16), vcat)
        del wlen
        gates = jax.nn.sigmoid(glr_ref[0])
        o_ref[0] = gates[:, 0:1] * o_cmp + gates[:, 1:2] * o_sel + gates[:, 2:3] * o_win


def nsa_decode_tables(tab_n, past, n_tok, wlen, n_ch, n_pg, page_rows):
    rows = NSA_KV_HEADS * n_tok * NSA_GROUP
    g = np.arange(rows) // (n_tok * NSA_GROUP)
    t = (np.arange(rows) // NSA_GROUP) % n_tok
    h = np.arange(rows) % NSA_GROUP
    head = g * NSA_GROUP + h
    cpd = past // CMP_STRIDE + LANES
    n_sel_pad = 2 * LANES

    def per_row(table):
        return table[head, np.arange(rows)]

    cidx = np.arange(cpd)[None, :] - CMP_PAD
    d_cmp = (past + t)[:, None] - (cidx * CMP_STRIDE + CMP_LEN - 1)
    cmpb = per_row(_bias_table(tab_n, d_cmp, (d_cmp >= 0) & (cidx >= 0)))
    cn = np.arange(2 * LANES)[None, :]
    d_near = t[:, None] + LANES - cn
    selnear = per_row(_bias_table(tab_n, d_near, d_near >= 0))
    cw = np.arange(wlen + LANES)[None, :]
    d_win = t[:, None] + wlen - cw
    winb = per_row(_bias_table(tab_n, d_win, (d_win >= 0) & (d_win < NSA_WINDOW)))
    c31 = jnp.broadcast_to(tab_n[N_BUCKETS - 1, head].astype(F32)[:, None], (rows, LANES))
    n_cmp = (past + SEL_BLOCK) // CMP_STRIDE - 1
    n_sel = (past + SEL_BLOCK) // SEL_BLOCK
    rsum = (np.arange(rows)[None, :] // NSA_GROUP == np.arange(rows // NSA_GROUP)[:, None]).astype(np.float32)
    return dict(cmpb=cmpb, selnear=selnear, winb=winb, c31=c31,
                c2s=jnp.asarray(_cmp_to_sel(min(n_cmp, cpd - CMP_PAD), n_sel, cpd, n_sel_pad)),
                e3=jnp.asarray(_block_expand(n_ch + 1, n_sel_pad, n_pg * page_rows), BF16),
                rsum=jnp.asarray(rsum), rexp=jnp.asarray(rsum.T))


def nsa_decode(page_table, pool, qz, glr, newkv, wbuf, neww, w1, w2, tabs, n_tok):
    s, n_pages = page_table.shape
    page_rows = pool.shape[1]
    n_pg = min(NSA_PAGES_PER_STEP, n_pages)
    n_ch = n_pages // n_pg
    past = n_pages * page_rows
    cpd = tabs["cmpb"].shape[1]
    rows = qz.shape[1]

    def page_spec(i):
        return pl.BlockSpec((1, page_rows, pool.shape[2]), lambda b, c, pt: (pt[b, c * n_pg + i], 0, 0))

    def seq_spec(a):
        return pl.BlockSpec((1,) + a.shape[1:], lambda b, c, pt: (b,) + (0,) * (a.ndim - 1))

    def const(a):
        n = a.ndim
        return pl.BlockSpec(a.shape, lambda b, c, pt: (0,) * n)

    consts = [w1, w2, tabs["c2s"], tabs["cmpb"], tabs["selnear"], tabs["winb"], tabs["c31"], tabs["e3"],
              tabs["rsum"], tabs["rexp"]]
    seqs = [qz, glr, newkv, wbuf, neww]
    grid_spec = pltpu.PrefetchScalarGridSpec(
        num_scalar_prefetch=1,
        grid=(s, n_ch),
        in_specs=[page_spec(i) for i in range(n_pg)] + [seq_spec(a) for a in seqs] + [const(a) for a in consts],
        out_specs=pl.BlockSpec((1, rows, LANES), lambda b, c, pt: (b, 0, 0)),
        scratch_shapes=[pltpu.VMEM((cpd, LANES), F32)] * 4
        + [pltpu.VMEM((n_ch, rows, n_pg * page_rows), F32), pltpu.VMEM((past + LANES, LANES), BF16)],
    )
    return pl.pallas_call(
        functools.partial(_nsa_dec_body, n_pg=n_pg, n_ch=n_ch, past=past, cpd=cpd, n_tok=n_tok),
        out_shape=jax.ShapeDtypeStruct((s, rows, LANES), F32),
        grid_spec=grid_spec,
        compiler_params=_cparams(("parallel", "arbitrary")),
        name="nsa_decode",
    )(page_table, *([pool] * n_pg), *seqs, *consts)


def _diff_dec_body(pt_ref, *refs, n_pg, n_ch, past, n_tok):
    del pt_ref
    pages = refs[:n_pg]
    sc_ref, qd_ref, newd_ref, near_ref, sub_ref, o_ref, s_sc, v_sc = refs[n_pg:]
    c = pl.program_id(1)
    page_rows = pages[0].shape[1]
    qd = (qd_ref[0] * (HEAD_DIM ** -0.5)).astype(BF16)
    for i in range(n_pg):
        page = pages[i][0]
        pg = c * n_pg + i
        for h in range(DIFF_HEADS):
            s_sc[c, h, :, i * page_rows:(i + 1) * page_rows] = _dot_nt(qd[h], page[:, h * LANES:(h + 1) * LANES].astype(BF16))
            v_sc[h, pl.ds(pl.multiple_of(pg * page_rows, page_rows), page_rows), :] = (
                page[:, (DIFF_HEADS + h) * LANES:(DIFF_HEADS + h + 1) * LANES].astype(BF16))

    @pl.when(c == n_ch - 1)
    def _():
        nd = newd_ref[0]
        chunk = n_pg * page_rows
        for h in range(DIFF_HEADS):
            near = near_ref[h]
            c31 = sc_ref[2 + h]
            s_new = _dot_nt(qd[h], _pad_rows(nd[:, h * LANES:(h + 1) * LANES], LANES).astype(BF16)) + near[:, LANES:]
            v_sc[h, pl.ds(past, LANES), :] = _pad_rows(nd[:, (DIFF_HEADS + h) * LANES:(DIFF_HEADS + h + 1) * LANES], LANES).astype(BF16)
            parts = []
            for ci in range(n_ch):
                s = s_sc[ci, h]
                if ci < n_ch - 1:
                    s = s + c31
                else:
                    s = s + jnp.concatenate([jnp.full((s.shape[0], chunk - LANES), c31, F32), near[:, :LANES]], axis=1)
                parts.append(s)
            parts.append(s_new)
            m = parts[0].max(axis=-1, keepdims=True)
            for s in parts[1:]:
                m = jnp.maximum(m, s.max(axis=-1, keepdims=True))
            den = jnp.zeros_like(m)
            acc = jnp.zeros((m.shape[0], LANES), F32)
            for ci, s in enumerate(parts):
                p = jnp.exp(s - m)
                den = den + p.sum(axis=-1, keepdims=True)
                acc = acc + _dot(p.astype(BF16), v_sc[h, ci * chunk:ci * chunk + s.shape[1], :])
            o12 = acc / den
            o = o12[:n_tok] - sc_ref[0] * o12[n_tok:]
            y = (o * lax.rsqrt(jnp.mean(o * o, axis=-1, keepdims=True) + SUBLN_EPS)) * sub_ref[...]
            o_ref[0, h] = y * sc_ref[1]


def diff_decode_tables(tab_d, n_tok):
    t = np.concatenate([np.arange(n_tok), np.arange(n_tok)])
    cn = np.arange(2 * LANES)[None, :]
    d_near = t[:, None] + LANES - cn
    return _bias_table(tab_d, d_near, d_near >= 0)


def diff_decode(page_table, pool, qd, newd, scal, near, subln, n_tok):
    s, n_pages = page_table.shape
    page_rows = pool.shape[1]
    n_pg = min(DIFF_PAGES_PER_STEP, n_pages)
    n_ch = n_pages // n_pg
    past = n_pages * page_rows

    def page_spec(i):
        return pl.BlockSpec((1, page_rows, pool.shape[2]), lambda b, c, pt: (pt[b, c * n_pg + i], 0, 0))

    grid_spec = pltpu.PrefetchScalarGridSpec(
        num_scalar_prefetch=1,
        grid=(s, n_ch),
        in_specs=[page_spec(i) for i in range(n_pg)] + [
            pl.BlockSpec(memory_space=pltpu.SMEM),
            pl.BlockSpec((1,) + qd.shape[1:], lambda b, c, pt: (b, 0, 0, 0)),
            pl.BlockSpec((1,) + newd.shape[1:], lambda b, c, pt: (b, 0, 0)),
            pl.BlockSpec(near.shape, lambda b, c, pt: (0, 0, 0)),
            pl.BlockSpec((1, DIFF_VDIM), lambda b, c, pt: (0, 0))],
        out_specs=pl.BlockSpec((1, DIFF_HEADS, n_tok, LANES), lambda b, c, pt: (b, 0, 0, 0)),
        scratch_shapes=[pltpu.VMEM((n_ch, DIFF_HEADS, 2 * n_tok, n_pg * page_rows), F32),
                        pltpu.VMEM((DIFF_HEADS, past + LANES, LANES), BF16)],
    )
    return pl.pallas_call(
        functools.partial(_diff_dec_body, n_pg=n_pg, n_ch=n_ch, past=past, n_tok=n_tok),
        out_shape=jax.ShapeDtypeStruct((s, DIFF_HEADS, n_tok, LANES), F32),
        grid_spec=grid_spec,
        compiler_params=_cparams(("parallel", "arbitrary")),
        name="diff_decode",
    )(page_table, *([pool] * n_pg), scal, qd, newd, near, subln.reshape(1, DIFF_VDIM))


def _sink_attention(qzb, kcat, vcat, bias, ok, sink):
    s = _dot_nt(qzb, kcat) + bias
    msk = (bias > 0.5 * NEG) & ok
    sm = jnp.where(msk, s, NEG)
    m = jnp.maximum(jnp.max(sm, axis=-1, keepdims=True), sink)
    p = jnp.where(msk, jnp.exp(sm - m), 0.0)
    den = jnp.sum(p, axis=-1, keepdims=True) + jnp.exp(sink - m)
    p = p / jnp.maximum(den, 1e-30)
    return _dot(p.astype(BF16), vcat)


def _swa_prompt_body(sink_ref, q_ref, kp_ref, kc_ref, vp_ref, vc_ref, bias_ref, o_ref):
    qi = pl.program_id(1)
    tq = q_ref.shape[1]
    q = q_ref[0] * (HEAD_DIM ** -0.5)
    kcat = jnp.concatenate([kp_ref[0], kc_ref[0]], axis=0)
    vcat = jnp.concatenate([vp_ref[0], vc_ref[0]], axis=0)
    col = lax.broadcasted_iota(jnp.int32, (tq, kcat.shape[0]), 1)
    ok = (col >= SWA_WINDOW) | (qi > 0)
    pieces = []
    for g in range(SWA_KV_HEADS):
        for h in range(SWA_GROUP):
            hh = g * SWA_GROUP + h
            qzb = _place_half(q[:, hh * HEAD_DIM:(hh + 1) * HEAD_DIM], g).astype(BF16)
            o = _sink_attention(qzb, kcat, vcat, bias_ref[hh], ok, sink_ref[hh])
            pieces.append(o[:, g * HEAD_DIM:(g + 1) * HEAD_DIM])
    o_ref[0] = jnp.concatenate(pieces, axis=1)


def swa_prompt(q, kvb, sinks, tab_s):
    b, t, _ = q.shape
    r = np.arange(TQ)[:, None]
    c = np.arange(SWA_WINDOW + TQ)[None, :]
    d = r + SWA_WINDOW - c
    bias = _bias_table(tab_s, d, (d >= 0) & (d < SWA_WINDOW))
    per = TQ // SWA_WINDOW

    def prev_spec(col):
        return pl.BlockSpec((1, SWA_WINDOW, LANES), lambda i, j: (i, jnp.maximum(per * j - 1, 0), col))

    def cur_spec(col):
        return pl.BlockSpec((1, TQ, LANES), lambda i, j: (i, j, col))

    return pl.pallas_call(
        _swa_prompt_body,
        out_shape=jax.ShapeDtypeStruct((b, t, SWA_HEADS * HEAD_DIM), F32),
        grid=(b, t // TQ),
        in_specs=[_smem_spec(), pl.BlockSpec((1, TQ, SWA_HEADS * HEAD_DIM), lambda i, j: (i, j, 0)),
                  prev_spec(0), cur_spec(0), prev_spec(1), cur_spec(1), _const_spec(bias.shape)],
        out_specs=pl.BlockSpec((1, TQ, SWA_HEADS * HEAD_DIM), lambda i, j: (i, j, 0)),
        compiler_params=_cparams(("parallel", "parallel")),
        name="swa_prompt",
    )(sinks.astype(F32), q, kvb, kvb, kvb, kvb, bias)


def _swa_sample_body(qz_ref, buf_ref, new_ref, bias_ref, sink_ref, o_ref):
    qzb = (qz_ref[0] * (HEAD_DIM ** -0.5)).astype(BF16)
    buf = buf_ref[0]
    new = new_ref[0]
    kcat = jnp.concatenate([buf[:, :LANES], _pad_rows(new[:, :LANES], LANES)], axis=0).astype(BF16)
    vcat = jnp.concatenate([buf[:, LANES:], _pad_rows(new[:, LANES:], LANES)], axis=0).astype(BF16)
    o_ref[0] = _sink_attention(qzb, kcat, vcat, bias_ref[...], True, sink_ref[...][:, :1])


def swa_sample(qz, buf, new, sinks, tab_s, n_tok):
    s, rows, _ = qz.shape
    wlen = buf.shape[1]
    g = np.arange(rows) // (n_tok * SWA_GROUP)
    t = (np.arange(rows) // SWA_GROUP) % n_tok
    head = g * SWA_GROUP + np.arange(rows) % SWA_GROUP
    c = np.arange(wlen + LANES)[None, :]
    d = t[:, None] + wlen - c
    bias = _bias_table(tab_s, d, (d >= 0) & (d < SWA_WINDOW))[head, np.arange(rows)]
    sink_rows = jnp.broadcast_to(sinks.astype(F32)[head][:, None], (rows, LANES))
    return pl.pallas_call(
        _swa_sample_body,
        out_shape=jax.ShapeDtypeStruct((s, rows, LANES), F32),
        grid=(s,),
        in_specs=[pl.BlockSpec((1, rows, LANES), lambda i: (i, 0, 0)),
                  pl.BlockSpec((1, wlen, 2 * LANES), lambda i: (i, 0, 0)),
                  pl.BlockSpec((1, NEW_ROWS, 2 * LANES), lambda i: (i, 0, 0)),
                  _const_spec(bias.shape), _const_spec(sink_rows.shape)],
        out_specs=pl.BlockSpec((1, rows, LANES), lambda i: (i, 0, 0)),
        compiler_params=_cparams(("parallel",)),
        name="swa_sample",
    )(qz, buf, new, bias, sink_rows)


RWKV_CHUNK = 64


def _head_sum(x, hsum_ref):
    return _dot(x, hsum_ref[...], HI)


def _rwkv_prep_body(ur_ref, uk_ref, uv_ref, ul_ref, pr_ref, pk_ref, pv_ref, pl_ref,
                    mur_ref, muk_ref, muv_ref, mul_ref, w0_ref, w2_ref, a0_ref, a2_ref, g2_ref,
                    kk_ref, ka_ref, hsum_ref, r_o, lw_o, k_o, v_o, na_o, b_o, g_o):
    def mix(u_ref, p_ref, mu_ref):
        u = u_ref[...]
        return u + (p_ref[...] - u) * mu_ref[...]

    r = mix(ur_ref, pr_ref, mur_ref)
    k = mix(uk_ref, pk_ref, muk_ref)
    v = mix(uv_ref, pv_ref, muv_ref)
    lo = mix(ul_ref, pl_ref, mul_ref)
    z = -(w0_ref[...] + _dot(jnp.tanh(lo), w2_ref[...], HI))
    softplus = jnp.maximum(z, 0.0) + jnp.log(1.0 + jnp.exp(-jnp.abs(z)))
    wlog = -softplus - 0.5
    a = jax.nn.sigmoid(a0_ref[...] + _dot(lo, a2_ref[...], HI))
    g = _dot(jax.nn.sigmoid(lo), g2_ref[...], HI)
    kk = k * kk_ref[...]
    kk = kk / jnp.maximum(jnp.sqrt(_head_sum(kk * kk, hsum_ref)), 1e-12)
    r_o[...] = r
    lw_o[...] = -jnp.exp(wlog)
    k_o[...] = k * (1.0 + (a - 1.0) * ka_ref[...])
    v_o[...] = v
    na_o[...] = -kk
    b_o[...] = kk * a
    g_o[...] = g


def rwkv_prep(us, prevs, mus, w0, w2p, a0, a2p, g2p, k_k, k_a, hsum):
    m = us[0].shape[0]
    tm = min(ROW_TILE, m)
    row = lambda a: pl.BlockSpec((tm, a.shape[1]), lambda i: (i, 0))
    vec = lambda a: a.reshape(1, -1)
    consts = [vec(x) for x in mus] + [vec(w0), w2p, vec(a0), a2p, g2p, vec(k_k), vec(k_a), hsum]
    return pl.pallas_call(
        _rwkv_prep_body,
        out_shape=[jax.ShapeDtypeStruct((m, RWKV_WIDTH), F32)] * 7,
        grid=(m // tm,),
        in_specs=[row(a) for a in us] + [row(a) for a in prevs] + [_const_spec(c.shape) for c in consts],
        out_specs=[pl.BlockSpec((tm, RWKV_WIDTH), lambda i: (i, 0))] * 7,
        compiler_params=_cparams(("parallel",)),
        name="rwkv_prep",
    )(*us, *prevs, *consts)


def _rwkv_chunk_body(r_ref, lw_ref, k_ref, v_ref, a_ref, b_ref, s0_ref, y_ref, sT_ref, st_sc):
    ci = pl.program_id(1)
    n_heads, c, n = r_ref.shape[1:]

    @pl.when(ci == 0)
    def _():
        st_sc[...] = s0_ref[0]

    row = lax.broadcasted_iota(jnp.int32, (c, c), 0)
    col = lax.broadcasted_iota(jnp.int32, (c, c), 1)
    incl = row >= col
    strict = row > col
    eye_c = (row == col).astype(F32)
    eye_n = (lax.broadcasted_iota(jnp.int32, (n, n), 0) == lax.broadcasted_iota(jnp.int32, (n, n), 1)).astype(F32)
    for h in range(n_heads):
        r, lw, k, v, a, b = (ref[0, h] for ref in (r_ref, lw_ref, k_ref, v_ref, a_ref, b_ref))
        cs = _dot(incl.astype(F32), lw, HI)
        gam = jnp.exp(cs)
        ginv = jnp.exp(-cs)
        to_end = jnp.exp(cs[c - 1:c, :] - cs)
        at = a * jnp.exp(cs - lw)
        rt = r * gam
        bt = b * ginv
        kt = k * ginv
        lb = jnp.where(strict, _dot_nt(at, bt, HI), 0.0)
        lk = jnp.where(strict, _dot_nt(at, kt, HI), 0.0)
        pb = jnp.where(incl, _dot_nt(rt, bt, HI), 0.0)
        pk = jnp.where(incl, _dot_nt(rt, kt, HI), 0.0)
        tinv = eye_c + lb
        lp = lb
        covered = 2
        while covered < c:
            lp = _dot(lp, lp, HI)
            tinv = tinv + _dot(tinv, lp, HI)
            covered *= 2
        w = _dot(tinv, at, HI)
        uv = _dot(tinv, _dot(lk, v, HI), HI)
        q = rt + _dot(pb, w, HI)
        bh = b * to_end
        kh = k * to_end
        tm = eye_n * gam[c - 1:c, :] + _dot_tn(bh, w, HI)
        bm = _dot_tn(bh, uv, HI) + _dot_tn(kh, v, HI)
        x = _dot(jnp.concatenate([q, tm], axis=0), st_sc[h], HI)
        y_ref[0, h] = x[:c] + _dot(pb, uv, HI) + _dot(pk, v, HI)
        st_sc[h] = x[c:] + bm

    @pl.when(ci == pl.num_programs(1) - 1)
    def _():
        sT_ref[0] = st_sc[...]


def rwkv_chunk(seqs, s0t, chunk):
    b, h, t, n = seqs[0].shape
    seq_spec = pl.BlockSpec((1, h, chunk, n), lambda i, j: (i, 0, j, 0))
    st_spec = pl.BlockSpec((1, h, n, n), lambda i, j: (i, 0, 0, 0))
    return pl.pallas_call(
        _rwkv_chunk_body,
        out_shape=[jax.ShapeDtypeStruct((b, h, t, n), F32), jax.ShapeDtypeStruct((b, h, n, n), F32)],
        grid=(b, t // chunk),
        in_specs=[seq_spec] * 6 + [st_spec],
        out_specs=[seq_spec, st_spec],
        scratch_shapes=[pltpu.VMEM((h, n, n), F32)],
        compiler_params=_cparams(("parallel", "arbitrary")),
        name="rwkv_chunk",
    )(*seqs, s0t)


def _rwkv_post_body(y_ref, r_ref, k_ref, v_ref, g_ref, rk_ref, lnw_ref, lnb_ref, hsum_ref, o_ref):
    y = y_ref[...]
    inv_n = 1.0 / RWKV_N
    mean = _head_sum(y, hsum_ref) * inv_n
    yc = y - mean
    var = _head_sum(yc * yc, hsum_ref) * inv_n
    yn = yc * lax.rsqrt(var + RWKV_GN_EPS) * lnw_ref[...] + lnb_ref[...]
    rk = _head_sum(r_ref[...] * k_ref[...] * rk_ref[...], hsum_ref)
    o_ref[...] = (yn + rk * v_ref[...]) * g_ref[...]


def rwkv_post(y, r, k, v, g, r_k, ln_w, ln_b, hsum):
    m = y.shape[0]
    tm = min(ROW_TILE, m)
    row = pl.BlockSpec((tm, RWKV_WIDTH), lambda i: (i, 0))
    vec = lambda a: a.reshape(1, -1)
    return pl.pallas_call(
        _rwkv_post_body,
        out_shape=jax.ShapeDtypeStruct((m, RWKV_WIDTH), F32),
        grid=(m // tm,),
        in_specs=[row] * 5 + [_const_spec((1, RWKV_WIDTH))] * 3 + [_const_spec(hsum.shape)],
        out_specs=row,
        compiler_params=_cparams(("parallel",)),
        name="rwkv_post",
    )(y, r, k, v, g, vec(r_k), vec(ln_w), vec(ln_b), hsum)


RWKV_SIZES = (RWKV_WIDTH, RWKV_WIDTH, RWKV_WIDTH, DECAY_LORA, AAA_LORA, GATE_LORA)
LORA_WIDTH = DECAY_LORA + AAA_LORA + GATE_LORA
ODD_SIZES = (SWA_HEADS * HEAD_DIM, 2 * SWA_KV_HEADS * HEAD_DIM, 3 * RWKV_WIDTH + LORA_WIDTH)
ODD_GROUPS = ((0, 512, (F32,)), (512, 256, (F32, BF16)), (768, 512, (F32,)), (1280, 512, (F32,)),
              (1792, 512, (F32,)), (2304, LORA_PAD, (F32,)))


def odd_weights(w_in):
    return jnp.pad(w_in, ((0, 0), (0, LORA_PAD - LORA_WIDTH))).astype(BF16)


def _lora_rows(w, first):
    return jnp.pad(w.astype(F32), ((first, LORA_PAD - first - w.shape[0]), (0, 0)))


def odd_mixer(x, g, w_odd, swa_buf, wkv0, shift0, prompt, sinks, tab, rw, hsum):
    b, t, d = x.shape
    mu, w0, w2, a0, a2, g2, k_k, k_a, r_k, ln_w, ln_b = rw
    q, kv, kvb, ur, uk, uv, ul = norm_proj(x.reshape(b * t, d), g, w_odd, ODD_GROUPS)
    r3 = lambda a: a.reshape(b, t, a.shape[-1])
    tab_s = tab[:, :SWA_HEADS]
    if prompt:
        o_swa = swa_prompt(r3(q), r3(kvb), sinks, tab_s).reshape(b * t, -1)
        ctx = r3(kv)[:, t - min(SWA_WINDOW, t):]
        shift0 = jnp.zeros((b, 3 * RWKV_WIDTH + LORA_WIDTH), F32)
        wkv0 = jnp.zeros((b, RWKV_HEADS, RWKV_N, RWKV_N), F32)
        chunk = RWKV_CHUNK
        t_pad = t
    else:
        rows = SWA_KV_HEADS * t * SWA_GROUP
        q5 = q.reshape(b, t, SWA_KV_HEADS, SWA_GROUP, HEAD_DIM).transpose(0, 2, 1, 3, 4)
        z = jnp.zeros_like(q5[:, 0])
        qz = jnp.stack([jnp.concatenate([q5[:, 0], z], -1), jnp.concatenate([z, q5[:, 1]], -1)], axis=1).reshape(b, rows, LANES)
        new = jnp.pad(r3(kv), ((0, 0), (0, NEW_ROWS - t), (0, 0)))
        o_rows = swa_sample(qz, swa_buf, new, sinks, tab_s, t)
        o6 = o_rows.reshape(b, SWA_KV_HEADS, t, SWA_GROUP, 2, HEAD_DIM)
        o_swa = jnp.stack([o6[:, 0, :, :, 0], o6[:, 1, :, :, 1]], axis=2).reshape(b * t, SWA_HEADS * HEAD_DIM)
        ctx = jnp.concatenate([swa_buf, r3(kv)], axis=1)[:, t:]
        chunk = NEW_ROWS
        t_pad = NEW_ROWS

    s_r, s_k, s_v, s_l = _split_cols(shift0.astype(F32), (RWKV_WIDTH,) * 3 + (LORA_WIDTH,))
    s_l = jnp.pad(s_l, ((0, 0), (0, LORA_PAD - LORA_WIDTH)))
    us = [ur, uk, uv, ul]
    prevs = [jnp.concatenate([s[:, None], r3(u)[:, :-1]], axis=1).reshape(b * t, -1) for s, u in zip((s_r, s_k, s_v, s_l), us)]
    mu_r, mu_k, mu_v, mu_l = _split_cols(mu.astype(F32), (RWKV_WIDTH,) * 3 + (LORA_WIDTH,))
    mu_l = jnp.pad(mu_l, (0, LORA_PAD - LORA_WIDTH))
    r, lw, k, v, na, bb, gate = rwkv_prep(us, prevs, [mu_r, mu_k, mu_v, mu_l], w0, _lora_rows(w2, 0), a0,
                                          _lora_rows(a2, DECAY_LORA), _lora_rows(g2, DECAY_LORA + AAA_LORA), k_k, k_a, hsum)

    def heads(a):
        a = a.reshape(b, t, RWKV_HEADS, RWKV_N).transpose(0, 2, 1, 3)
        return jnp.pad(a, ((0, 0), (0, 0), (0, t_pad - t), (0, 0)))

    y, st = rwkv_chunk([heads(a) for a in (r, lw, k, v, na, bb)], jnp.swapaxes(wkv0.astype(F32), -1, -2), chunk)
    y = y[:, :, :t].transpose(0, 2, 1, 3).reshape(b * t, RWKV_WIDTH)
    o_rwkv = rwkv_post(y, r, k, v, gate, r_k, ln_w, ln_b, hsum)
    shift = jnp.concatenate([r3(ur)[:, -1], r3(uk)[:, -1], r3(uv)[:, -1], r3(ul)[:, -1, :LORA_WIDTH]], axis=-1)
    return o_swa, o_rwkv, ctx, jnp.swapaxes(st, -1, -2), shift


EVEN_SIZES = (NSA_HEADS * HEAD_DIM, 4 * NSA_KV_HEADS * HEAD_DIM, 2 * NSA_KV_HEADS * HEAD_DIM, 3 * NSA_HEADS,
              2 * DIFF_HEADS * HEAD_DIM, 2 * DIFF_HEADS * DIFF_VDIM)
EVEN_GROUPS = ((0, 512, (F32,)), (512, 512, (F32, BF16)), (1024, 256, (F32, BF16)), (1280, 512, (F32,)),
               (1792, 1024, (F32, BF16)), (2816, 128, (F32,)))


def _split_cols(w, sizes):
    offs = np.cumsum([0] + list(sizes))
    return [w[..., int(offs[i]):int(offs[i + 1])] for i in range(len(sizes))]


def even_weights(w_in):
    wq, wkv, wkvw, wgl, wdq, wdkv = _split_cols(w_in, EVEN_SIZES)
    wgl = jnp.pad(wgl, ((0, 0), (0, LANES - wgl.shape[1])))
    return jnp.concatenate([wq, wkv, wkvw, wdq, wdkv, wgl], axis=1).astype(BF16)


def cmp_lane_weights(cmp_w):
    wt = jax.nn.softmax(cmp_w.astype(F32), axis=-1)
    wl = jnp.repeat(jnp.swapaxes(wt, 1, 2), HEAD_DIM, axis=2)
    return wl[:, :CMP_STRIDE], wl[:, CMP_STRIDE:]


def diff_scalars(lq, layer, tab_d):
    lam_init = 0.8 - 0.6 * math.exp(-0.3 * layer)
    lq = lq.astype(F32)
    lam = jnp.exp(jnp.sum(lq[0] * lq[1])) - jnp.exp(jnp.sum(lq[2] * lq[3])) + lam_init
    return jnp.concatenate([jnp.stack([lam, jnp.asarray(1.0 - lam_init, F32)]), tab_d[N_BUCKETS - 1].astype(F32)])


def even_prompt(x, g, w_even, w1, w2, tab, scal, subln):
    b, t, d = x.shape
    q, kv, kvb, kvw, kvwb, dq, dkv, dkvb, gl = norm_proj(x.reshape(b * t, d), g, w_even, EVEN_GROUPS)
    r3 = lambda a: a.reshape(b, t, a.shape[-1])
    tab_n, tab_d = tab[:, :NSA_HEADS], tab[:, NSA_HEADS:NSA_HEADS + DIFF_HEADS]
    tabs = nsa_prompt_tables(tab_n, t)
    ckv = compress_prompt(r3(kv), w1, w2, tabs["c2s"].shape[0])
    o_nsa = nsa_prompt(r3(q), r3(gl), ckv, r3(kvb), r3(kvwb), tabs)
    o_diff = diff_prompt(r3(dq), r3(dkvb), scal, diff_prompt_tables(tab_d), subln)
    return o_nsa.reshape(b * t, -1), o_diff.reshape(b * t, -1), kv, kvw, dkv


def even_sample(x, g, w_even, nsa_pool, diff_pool, win_buf, page_table, w1, w2, tab, scal, subln):
    s, n_tok, d = x.shape
    q, kv, _, kvw, _, dq, dkv, _, gl = norm_proj(x.reshape(s * n_tok, d), g, w_even, EVEN_GROUPS)
    tab_n, tab_d = tab[:, :NSA_HEADS], tab[:, NSA_HEADS:NSA_HEADS + DIFF_HEADS]
    n_pages = page_table.shape[1]
    page_rows = nsa_pool.shape[1]
    past = n_pages * page_rows
    wlen = win_buf.shape[1]
    n_pg = min(NSA_PAGES_PER_STEP, n_pages)
    rows = NSA_KV_HEADS * n_tok * NSA_GROUP

    q5 = q.reshape(s, n_tok, NSA_KV_HEADS, NSA_GROUP, HEAD_DIM).transpose(0, 2, 1, 3, 4)
    z = jnp.zeros_like(q5[:, 0])
    qz = jnp.stack([jnp.concatenate([q5[:, 0], z], -1), jnp.concatenate([z, q5[:, 1]], -1)], axis=1).reshape(s, rows, LANES)
    glr = gl[:, :3 * NSA_HEADS].reshape(s, n_tok, NSA_KV_HEADS, NSA_GROUP, 3).transpose(0, 2, 1, 3, 4).reshape(s, rows, 3)
    glr = jnp.pad(glr, ((0, 0), (0, 0), (0, LANES - 3)))
    pad_tok = lambda a: jnp.pad(a.reshape(s, n_tok, a.shape[-1]), ((0, 0), (0, NEW_ROWS - n_tok), (0, 0)))
    tabs = nsa_decode_tables(tab_n, past, n_tok, wlen, n_pages // n_pg, n_pg, page_rows)
    o_rows = nsa_decode(page_table, nsa_pool.reshape(nsa_pool.shape[0], page_rows, -1), qz, glr, pad_tok(kv),
                        win_buf.reshape(s, wlen, -1), pad_tok(kvw), w1, w2, tabs, n_tok)
    o6 = o_rows.reshape(s, NSA_KV_HEADS, n_tok, NSA_GROUP, 2, HEAD_DIM)
    o_nsa = jnp.stack([o6[:, 0, :, :, 0], o6[:, 1, :, :, 1]], axis=2).reshape(s * n_tok, NSA_HEADS * HEAD_DIM)

    dq5 = dq.reshape(s, n_tok, DIFF_HEADS, 2, HEAD_DIM).transpose(0, 2, 3, 1, 4)
    zd = jnp.zeros_like(dq5[:, :, 0])
    qd = jnp.concatenate([jnp.concatenate([dq5[:, :, 0], zd], -1), jnp.concatenate([zd, dq5[:, :, 1]], -1)], axis=2)
    o_d = diff_decode(page_table, diff_pool.reshape(diff_pool.shape[0], page_rows, -1), qd, pad_tok(dkv), scal,
                      diff_decode_tables(tab_d, n_tok), subln, n_tok)
    o_diff = o_d.transpose(0, 2, 1, 3).reshape(s * n_tok, DIFF_HEADS * DIFF_VDIM)
    return o_nsa, o_diff, kv, kvw, dkv


def kernel(x_prompt, x_sample, cache_nsa_kv, cache_diff_kv, cache_nsa_win, cache_swa, state_rwkv_wkv, state_rwkv_shift, page_table, rel_bias, norm_mix, norm_ffn, norm_final, w_in_even, w_out_even, nsa_cmp_w, diff_lambda, diff_subln, w_in_odd, w_out_odd, swa_sinks, rwkv_mu, rwkv_w0, rwkv_w2, rwkv_a0, rwkv_a2, rwkv_g2, rwkv_k_k, rwkv_k_a, rwkv_r_k, rwkv_ln_w, rwkv_ln_b, ffn_w_gate, ffn_w_up, ffn_w_down):
    b, t, d = x_prompt.shape
    s, n_tok, _ = x_sample.shape
    depth = norm_mix.shape[0]
    assert NSA_WINDOW == 2 * TQ and t % TQ == 0 and TQ >= MAX_DISTANCE and n_tok <= NEW_ROWS
    tab = rel_bias.astype(F32)
    tab_d = tab[:, NSA_HEADS:NSA_HEADS + DIFF_HEADS]
    head_id = np.arange(RWKV_WIDTH) // RWKV_N
    hsum = jnp.asarray((head_id[:, None] == head_id[None, :]).astype(np.float32))
    xp = x_prompt.reshape(b * t, d)
    xs = x_sample.reshape(s * n_tok, d)
    outs = {name: [] for name in ("nsa_p", "nsa_s", "diff_p", "diff_s", "win_p", "win_s",
                                  "swa_p", "swa_s", "wkv_p", "wkv_s", "sh_p", "sh_s")}
    for l in range(depth):
        if l % 2 == 0:
            e = l // 2
            w_even = even_weights(w_in_even[e])
            w1, w2 = cmp_lane_weights(nsa_cmp_w[e])
            scal = diff_scalars(diff_lambda[e], l, tab_d)
            pa, pb, kv, kvw, dkv = even_prompt(xp.reshape(b, t, d), norm_mix[l], w_even, w1, w2, tab, scal, diff_subln[e])
            sa, sb, skv, skvw, sdkv = even_sample(xs.reshape(s, n_tok, d), norm_mix[l], w_even, cache_nsa_kv[e],
                                                  cache_diff_kv[e], cache_nsa_win[e], page_table, w1, w2, tab, scal,
                                                  diff_subln[e])
            outs["nsa_p"].append(kv.reshape(b, t, 4, NSA_KV_HEADS, HEAD_DIM))
            outs["nsa_s"].append(skv.reshape(s, n_tok, 4, NSA_KV_HEADS, HEAD_DIM))
            outs["diff_p"].append(dkv.reshape(b, t, 2, DIFF_HEADS, DIFF_VDIM))
            outs["diff_s"].append(sdkv.reshape(s, n_tok, 2, DIFF_HEADS, DIFF_VDIM))
            outs["win_p"].append(kvw.reshape(b, t, 2, NSA_KV_HEADS, HEAD_DIM)[:, t - min(NSA_WINDOW, t):])
            new_win = skvw.reshape(s, n_tok, 2, NSA_KV_HEADS, HEAD_DIM).astype(cache_nsa_win.dtype)
            outs["win_s"].append(jnp.concatenate([cache_nsa_win[e], new_win], axis=1)[:, n_tok:])
            w_out = w_out_even[e].astype(BF16)
        else:
            o = l // 2
            rw = (rwkv_mu[o], rwkv_w0[o], rwkv_w2[o], rwkv_a0[o], rwkv_a2[o], rwkv_g2[o], rwkv_k_k[o],
                  rwkv_k_a[o], rwkv_r_k[o].reshape(-1), rwkv_ln_w[o], rwkv_ln_b[o])
            w_odd = odd_weights(w_in_odd[o])
            pa, pb, ctx_p, wkv_p, sh_p = odd_mixer(xp.reshape(b, t, d), norm_mix[l], w_odd, None, None, None, True,
                                                   swa_sinks[o], tab, rw, hsum)
            swa_buf = cache_swa[o].reshape(s, cache_swa.shape[2], -1).astype(F32)
            sa, sb, ctx_s, wkv_s, sh_s = odd_mixer(xs.reshape(s, n_tok, d), norm_mix[l], w_odd, swa_buf,
                                                   state_rwkv_wkv[o], state_rwkv_shift[o], False, swa_sinks[o], tab,
                                                   rw, hsum)
            kv_shape = (2, SWA_KV_HEADS, HEAD_DIM)
            outs["swa_p"].append(ctx_p.reshape(b, -1, *kv_shape))
            outs["swa_s"].append(ctx_s.reshape(s, -1, *kv_shape))
            outs["wkv_p"].append(wkv_p)
            outs["wkv_s"].append(wkv_s)
            outs["sh_p"].append(sh_p)
            outs["sh_s"].append(sh_s)
            w_out = w_out_odd[o].astype(BF16)
        half = pa.shape[1]
        xp = out_proj(xp, pa, pb, w_out[:half], w_out[half:])
        xs = out_proj(xs, sa, sb, w_out[:half], w_out[half:])
        wg, wu, wd = ffn_w_gate[l].astype(BF16), ffn_w_up[l].astype(BF16), ffn_w_down[l].astype(BF16)
        last = l == depth - 1
        xp = ffn(xp, norm_ffn[l], norm_final, wg, wu, wd, last)
        xs = ffn(xs, norm_ffn[l], norm_final, wg, wu, wd, last)
    if depth == 0:
        raise ValueError("depth must be positive")
    st = lambda name: jnp.stack(outs[name])
    return (xp.reshape(b, t, d), xs.reshape(s, n_tok, d), st("nsa_p"), st("nsa_s"), st("diff_p"), st("diff_s"),
            st("win_p"), st("win_s"), st("swa_p"), st("swa_s"), st("wkv_p"), st("wkv_s"), st("sh_p"), st("sh_s"))
```

```python
import functools
import math

import numpy as np
import jax
import jax.numpy as jnp
from jax import lax
from jax.experimental import pallas as pl
from jax.experimental.pallas import tpu as pltpu

F32 = jnp.float32
BF16 = jnp.bfloat16
HI = lax.Precision.HIGHEST

HEAD_DIM = 64
NSA_KV_HEADS = 2
NSA_GROUP = 4
NSA_HEADS = NSA_KV_HEADS * NSA_GROUP
CMP_STRIDE = 16
CMP_LEN = 32
SEL_BLOCK = 64
N_SELECT = 16
NSA_WINDOW = 512
DIFF_HEADS = 4
DIFF_VDIM = 128
SWA_HEADS = 8
SWA_KV_HEADS = 2
SWA_GROUP = 4
SWA_WINDOW = 128
RWKV_N = 64
RWKV_HEADS = 8
RWKV_WIDTH = RWKV_N * RWKV_HEADS
DECAY_LORA = 32
AAA_LORA = 32
GATE_LORA = 96
LORA_PAD = 256
N_BUCKETS = 32
MAX_DISTANCE = 128
NORM_EPS = 1e-6
SUBLN_EPS = 1e-5
RWKV_GN_EPS = 64e-5
NEG = -1e30
FORCE = 1e6
REMOVED = -3e38

LANES = 128
SUBLANES = 8
VMEM_LIMIT_BYTES = 56 * 1024 * 1024

TQ = 256
FAR_TILES = 4
ROW_TILE = 512
CMP_PAD = 16

NN = (((1,), (0,)), ((), ()))
NT = (((1,), (1,)), ((), ()))
TN = (((0,), (0,)), ((), ()))


def _cparams(sem):
    return pltpu.CompilerParams(dimension_semantics=sem, vmem_limit_bytes=VMEM_LIMIT_BYTES)


def _const_spec(shape):
    n = len(shape)
    return pl.BlockSpec(shape, lambda *_: (0,) * n)


def _smem_spec():
    return pl.BlockSpec(memory_space=pltpu.SMEM)


def _dot(a, b, precision=None):
    return jnp.dot(a, b, preferred_element_type=F32, precision=precision)


def _dot_nt(a, b, precision=None):
    return lax.dot_general(a, b, NT, preferred_element_type=F32, precision=precision)


def _split(x):
    hi = x.astype(BF16)
    return hi, (x - hi.astype(F32)).astype(BF16)


def _dot3(a, b, dims=NN):
    f = lambda x, y: lax.dot_general(x, y, dims, preferred_element_type=F32)
    return f(a[0], b[0]) + (f(a[1], b[0]) + f(a[0], b[1]))


def _dot2(a, b, dims=NN):
    f = lambda x, y: lax.dot_general(x, y, dims, preferred_element_type=F32)
    return f(a[0], b) + f(a[1], b)


def _norm_proj_body(x_ref, g_ref, w_ref, *out_refs, groups):
    x = x_ref[...]
    h = (x * lax.rsqrt(jnp.mean(x * x, axis=-1, keepdims=True) + NORM_EPS)) * g_ref[...]
    hb = h.astype(BF16)
    i = 0
    for off, wd, dts in groups:
        r = _dot(hb, w_ref[:, off:off + wd])
        for dt in dts:
            out_refs[i][...] = r.astype(dt)
            i += 1


def norm_proj(x2d, g, w_bf16, groups):
    m, d = x2d.shape
    tm = min(ROW_TILE, m)
    out_shape, out_specs = [], []
    for _, wd, dts in groups:
        for dt in dts:
            out_shape.append(jax.ShapeDtypeStruct((m, wd), dt))
            out_specs.append(pl.BlockSpec((tm, wd), lambda i: (i, 0)))
    return pl.pallas_call(
        functools.partial(_norm_proj_body, groups=groups),
        out_shape=out_shape,
        grid=(m // tm,),
        in_specs=[pl.BlockSpec((tm, d), lambda i: (i, 0)), _const_spec((1, d)), _const_spec(w_bf16.shape)],
        out_specs=out_specs,
        compiler_params=_cparams(("parallel",)),
        name="norm_proj",
    )(x2d, g.reshape(1, d), w_bf16)


def _out_proj_body(x_ref, a_ref, b_ref, wa_ref, wb_ref, o_ref):
    acc = _dot(a_ref[...].astype(BF16), wa_ref[...]) + _dot(b_ref[...].astype(BF16), wb_ref[...])
    o_ref[...] = x_ref[...] + acc


def out_proj(x2d, a, b, wa, wb):
    m, d = x2d.shape
    tm = min(ROW_TILE, m)
    return pl.pallas_call(
        _out_proj_body,
        out_shape=jax.ShapeDtypeStruct((m, d), F32),
        grid=(m // tm,),
        in_specs=[pl.BlockSpec((tm, d), lambda i: (i, 0)),
                  pl.BlockSpec((tm, a.shape[1]), lambda i: (i, 0)),
                  pl.BlockSpec((tm, b.shape[1]), lambda i: (i, 0)),
                  _const_spec(wa.shape), _const_spec(wb.shape)],
        out_specs=pl.BlockSpec((tm, d), lambda i: (i, 0)),
        compiler_params=_cparams(("parallel",)),
        name="out_proj",
    )(x2d, a, b, wa, wb)


def _ffn_body(x_ref, g_ref, gf_ref, wg_ref, wu_ref, wd_ref, o_ref, h_sc, acc_sc, *, final_norm):
    f = pl.program_id(1)

    @pl.when(f == 0)
    def _():
        x = x_ref[...]
        h = (x * lax.rsqrt(jnp.mean(x * x, axis=-1, keepdims=True) + NORM_EPS)) * g_ref[...]
        h_sc[...] = h.astype(BF16)
        acc_sc[...] = jnp.zeros_like(acc_sc)

    hb = h_sc[...]
    gate = _dot(hb, wg_ref[...])
    up = _dot(hb, wu_ref[...])
    act = (gate * jax.nn.sigmoid(gate)) * up
    acc_sc[...] += _dot(act.astype(BF16), wd_ref[...])

    @pl.when(f == pl.num_programs(1) - 1)
    def _():
        y = x_ref[...] + acc_sc[...]
        if final_norm:
            y = (y * lax.rsqrt(jnp.mean(y * y, axis=-1, keepdims=True) + NORM_EPS)) * gf_ref[...]
        o_ref[...] = y


def ffn(x2d, g, g_final, wg, wu, wd, final_norm):
    m, d = x2d.shape
    dff = wg.shape[1]
    tm = min(ROW_TILE, m)
    tf = dff // 2
    return pl.pallas_call(
        functools.partial(_ffn_body, final_norm=final_norm),
        out_shape=jax.ShapeDtypeStruct((m, d), F32),
        grid=(m // tm, dff // tf),
        in_specs=[pl.BlockSpec((tm, d), lambda i, f: (i, 0)), _const_spec((1, d)), _const_spec((1, d)),
                  pl.BlockSpec((d, tf), lambda i, f: (0, f)), pl.BlockSpec((d, tf), lambda i, f: (0, f)),
                  pl.BlockSpec((tf, d), lambda i, f: (f, 0))],
        out_specs=pl.BlockSpec((tm, d), lambda i, f: (i, 0)),
        scratch_shapes=[pltpu.VMEM((tm, d), BF16), pltpu.VMEM((tm, d), F32)],
        compiler_params=_cparams(("parallel", "arbitrary")),
        name="ffn",
    )(x2d, g.reshape(1, d), g_final.reshape(1, d), wg, wu, wd)


def _t5_bucket(dist):
    n = jnp.maximum(dist, 0)
    max_exact = N_BUCKETS // 2
    nf = jnp.maximum(n, 1).astype(F32)
    large = max_exact + (jnp.log(nf / max_exact) / math.log(MAX_DISTANCE / max_exact)
                         * (N_BUCKETS - max_exact)).astype(jnp.int32)
    large = jnp.minimum(large, N_BUCKETS - 1)
    return jnp.where(n < max_exact, n, large)


def _toeplitz_bias(tab, rows, cols, offset, valid_lo, valid_hi):
    length = rows + cols - 1
    d = jnp.arange(length, dtype=jnp.int32) - (cols - 1) + offset
    g = jnp.where((d >= valid_lo) & (d <= valid_hi), tab[_t5_bucket(d)].astype(F32).T, NEG)
    h = g[:, ::-1]
    flat = jnp.tile(h, (1, rows + 1))[:, :rows * (length + 1)].reshape(-1, rows, length + 1)[:, :, :cols]
    return flat[:, ::-1, :]


def _cmp_to_sel(n_cmp, n_sel, rows, cols):
    i = np.arange(n_cmp)[:, None]
    j = np.arange(n_sel)[None, :]
    m = (i * CMP_STRIDE < (j + 1) * SEL_BLOCK) & (i * CMP_STRIDE + CMP_LEN > j * SEL_BLOCK)
    out = np.zeros((rows, cols), np.float32)
    out[CMP_PAD:CMP_PAD + n_cmp, :n_sel] = m
    return out


def _block_expand(n_chunks, n_blocks, chunk_keys):
    c = np.arange(n_chunks)[:, None, None]
    j = np.arange(n_blocks)[None, :, None]
    l = np.arange(chunk_keys)[None, None, :]
    return (j == (c * chunk_keys + l) // SEL_BLOCK).astype(np.float32)


def _place_half(x64, half):
    z = jnp.zeros_like(x64)
    return jnp.concatenate([x64, z], axis=1) if half == 0 else jnp.concatenate([z, x64], axis=1)


def _lane_tile(x, width):
    reps = width // x.shape[1]
    return x if reps == 1 else jnp.concatenate([x] * reps, axis=1)


def _masked_softmax_parts(parts):
    m = None
    for s, msk in parts:
        mm = jnp.max(jnp.where(msk, s, NEG), axis=-1, keepdims=True)
        m = mm if m is None else jnp.maximum(m, mm)
    ps, den = [], None
    for s, msk in parts:
        p = jnp.where(msk, jnp.exp(jnp.where(msk, s, NEG) - m), 0.0)
        ps.append(p)
        d = jnp.sum(p, axis=-1, keepdims=True)
        den = d if den is None else den + d
    inv = 1.0 / jnp.maximum(den, 1e-30)
    return [p * inv for p in ps]


def _select_blocks(score, n_top):
    lane = lax.broadcasted_iota(jnp.int32, score.shape, 1)
    big = score.shape[1]
    sel = jnp.zeros(score.shape, F32)
    sc = score
    for _ in range(n_top):
        m = jnp.max(sc, axis=-1, keepdims=True)
        idx = jnp.min(jnp.where(sc == m, lane, big), axis=-1, keepdims=True)
        hit = lane == idx
        sel = jnp.where(hit & (m > 0.5 * NEG), 1.0, sel)
        sc = jnp.where(hit, REMOVED, sc)
    return sel


def _block_scores(imp, q_pos):
    j = lax.broadcasted_iota(jnp.int32, imp.shape, 1)
    cur = q_pos // SEL_BLOCK
    avail = j * SEL_BLOCK <= q_pos
    forced = (j == 0) | (j == cur) | (j == cur - 1)
    return jnp.where(avail, jnp.where(forced, FORCE, imp), NEG)


def _flash_init(h, m_sc, l_sc, acc_sc):
    m_sc[h] = jnp.full(m_sc.shape[1:], REMOVED, F32)
    l_sc[h] = jnp.zeros(l_sc.shape[1:], F32)
    acc_sc[h] = jnp.zeros(acc_sc.shape[1:], F32)


def _flash_update(h, s, vblk, m_sc, l_sc, acc_sc, shift=None):
    m_prev = m_sc[h]
    m_cur = jnp.max(s, axis=-1, keepdims=True)
    if shift is not None:
        m_cur = m_cur + shift
    m_next = jnp.maximum(m_prev, m_cur)
    alpha = jnp.exp(m_prev - m_next)
    sub = m_next if shift is None else m_next - shift
    p = jnp.exp(s - _lane_tile(sub, s.shape[1]))
    l_sc[h] = alpha * l_sc[h] + jnp.sum(p, axis=-1, keepdims=True)
    acc_sc[h] = alpha * acc_sc[h] + _dot(p.astype(BF16), vblk)
    m_sc[h] = m_next


def _causal_tiles(qi, step):
    n_far = jnp.maximum(qi - 1, 0)
    n_big = n_far // FAR_TILES

    def big_body(kb, carry):
        step(pl.multiple_of(kb * (FAR_TILES * TQ), FAR_TILES * TQ), FAR_TILES * TQ, None, kb)
        return carry

    def far_body(kt, carry):
        step(pl.multiple_of(kt * TQ, TQ), TQ, None, kt)
        return carry

    lax.fori_loop(0, n_big, big_body, 0)
    lax.fori_loop(n_big * FAR_TILES, n_far, far_body, 0)

    @pl.when(qi >= 1)
    def _():
        step(pl.multiple_of((qi - 1) * TQ, TQ), TQ, 0, qi - 1)

    step(pl.multiple_of(qi * TQ, TQ), TQ, 1, qi)


def _compress_body(x_ref, w1_ref, w2_ref, o_ref, *, nblk):
    x = x_ref[0].reshape(nblk, CMP_STRIDE, LANES)
    p1 = (x * w1_ref[0][None]).sum(axis=1)
    p2 = (x * w2_ref[0][None]).sum(axis=1)
    ck = p1 + pltpu.roll(p2, nblk - 1, 0)
    row = lax.broadcasted_iota(jnp.int32, ck.shape, 0)
    o_ref[0, 0] = jnp.zeros(o_ref.shape[2:], F32)
    o_ref[0, 0, CMP_PAD:CMP_PAD + nblk, :] = jnp.where(row < nblk - 1, ck, 0.0)


def compress_prompt(kv, w1, w2, cp):
    b, t, _ = kv.shape
    nblk = t // CMP_STRIDE
    return pl.pallas_call(
        functools.partial(_compress_body, nblk=nblk),
        out_shape=jax.ShapeDtypeStruct((b, 2, cp, LANES), F32),
        grid=(b, 2),
        in_specs=[pl.BlockSpec((1, t, LANES), lambda i, k: (i, 0, k)),
                  pl.BlockSpec((1, CMP_STRIDE, LANES), lambda i, k: (k, 0, 0)),
                  pl.BlockSpec((1, CMP_STRIDE, LANES), lambda i, k: (k, 0, 0))],
        out_specs=pl.BlockSpec((1, 1, cp, LANES), lambda i, k: (i, k, 0, 0)),
        compiler_params=_cparams(("parallel", "parallel")),
        name="compress_prompt",
    )(kv, w1, w2)


def _nsa_prompt_body(c31_ref, q_ref, gl_ref, ck_ref, cv_ref, c2s_ref, selk_ref, selv_ref,
                     wk0_ref, wk1_ref, wk2_ref, wv0_ref, wv1_ref, wv2_ref,
                     selb_ref, winb_ref, cnear_ref, e3_ref, e3big_ref, o_ref, m_sc, l_sc, acc_sc, *, cp):
    qi = pl.program_id(1)
    tq = q_ref.shape[1]
    near = tq // CMP_STRIDE + CMP_PAD
    near0 = pl.multiple_of(qi * (tq // CMP_STRIDE), tq // CMP_STRIDE)
    q = q_ref[0] * (HEAD_DIM ** -0.5)
    gates = jax.nn.sigmoid(gl_ref[0])
    ck = _split(ck_ref[0, 0])
    cv = _split(cv_ref[0, 0])
    ck_near = _split(ck_ref[0, 0, pl.ds(near0, near), :])
    cv_near = _split(cv_ref[0, 0, pl.ds(near0, near), :])
    c2s = c2s_ref[...].astype(BF16)
    c2s_near = c2s_ref[pl.ds(near0, near), :].astype(BF16)
    kcat = jnp.concatenate([wk0_ref[0], wk1_ref[0], wk2_ref[0]], axis=0)
    vcat = jnp.concatenate([wv0_ref[0], wv1_ref[0], wv2_ref[0]], axis=0)
    q_pos = qi * tq + lax.broadcasted_iota(jnp.int32, (tq, 1), 0)
    cp_idx = lax.broadcasted_iota(jnp.int32, (tq, cp), 1)
    far_mask = (cp_idx >= CMP_PAD) & (cp_idx < near0)
    near_lane = lax.broadcasted_iota(jnp.int32, (tq, near), 1)
    near_ok = (near_lane >= CMP_PAD) | (qi > 0)
    wcol = lax.broadcasted_iota(jnp.int32, (tq, 3 * tq), 1)
    win_ok = wcol >= (2 - qi) * tq

    pieces = []
    for g in range(NSA_KV_HEADS):
        qz, qzb = [], []
        for h in range(NSA_GROUP):
            hh = g * NSA_GROUP + h
            z = _place_half(q[:, hh * HEAD_DIM:(hh + 1) * HEAD_DIM], g)
            qz.append(_split(z))
            qzb.append(qz[h][0])

        o_cmp = []
        psum_far = jnp.zeros((tq, cp), F32)
        psum_near = jnp.zeros((tq, near), F32)
        for h in range(NSA_GROUP):
            hh = g * NSA_GROUP + h
            s_far = _dot3(qz[h], ck, NT) + c31_ref[hh]
            nb = cnear_ref[hh]
            s_near = _dot3(qz[h], ck_near, NT) + nb
            p_far, p_near = _masked_softmax_parts([(s_far, far_mask), (s_near, (nb > 0.5 * NEG) & near_ok)])
            o_cmp.append(_dot3(_split(p_far), cv) + _dot3(_split(p_near), cv_near))
            psum_far = psum_far + p_far
            psum_near = psum_near + p_near
        imp = _dot2(_split(psum_far), c2s) + _dot2(_split(psum_near), c2s_near)
        sel = _select_blocks(_block_scores(imp, q_pos), N_SELECT).astype(BF16)

        for h in range(NSA_GROUP):
            _flash_init(h, m_sc, l_sc, acc_sc)

        def sel_step(k0, size, kind, tile, g=g, qzb=qzb, sel=sel):
            kblk = selk_ref[0, pl.ds(k0, size), :]
            vblk = selv_ref[0, pl.ds(k0, size), :]
            expand = e3big_ref[tile] if size != tq else e3_ref[tile]
            madd = (_dot(sel, expand) - 1.0) * 1e30
            for h in range(NSA_GROUP):
                hh = g * NSA_GROUP + h
                s = _dot_nt(qzb[h], kblk)
                if kind is None:
                    _flash_update(h, s + madd, vblk, m_sc, l_sc, acc_sc, shift=c31_ref[hh])
                else:
                    _flash_update(h, s + (madd + selb_ref[hh, kind]), vblk, m_sc, l_sc, acc_sc)

        _causal_tiles(qi, sel_step)

        for h in range(NSA_GROUP):
            hh = g * NSA_GROUP + h
            wb = winb_ref[hh]
            s = _dot_nt(qzb[h], kcat) + wb
            msk = (wb > 0.5 * NEG) & win_ok
            (p,) = _masked_softmax_parts([(s, msk)])
            o_win = _dot(p.astype(BF16), vcat)
            o_sel = acc_sc[h] / l_sc[h]
            c = g * NSA_GROUP * 3 + h * 3
            o = gates[:, c:c + 1] * o_cmp[h] + gates[:, c + 1:c + 2] * o_sel + gates[:, c + 2:c + 3] * o_win
            pieces.append(o[:, g * HEAD_DIM:(g + 1) * HEAD_DIM])
    o_ref[0] = jnp.concatenate(pieces, axis=1)


def nsa_prompt(q, gl, ckv, kvb, kvwb, tabs):
    b, t, _ = q.shape
    nq = t // TQ
    cp = ckv.shape[2]

    def win_spec(back, col):
        return pl.BlockSpec((1, TQ, LANES), lambda i, j: (i, jnp.maximum(j - back, 0), col))

    return pl.pallas_call(
        functools.partial(_nsa_prompt_body, cp=cp),
        out_shape=jax.ShapeDtypeStruct((b, t, NSA_HEADS * HEAD_DIM), F32),
        grid=(b, nq),
        in_specs=[_smem_spec(),
                  pl.BlockSpec((1, TQ, NSA_HEADS * HEAD_DIM), lambda i, j: (i, j, 0)),
                  pl.BlockSpec((1, TQ, LANES), lambda i, j: (i, j, 0)),
                  pl.BlockSpec((1, 1, cp, LANES), lambda i, j: (i, 0, 0, 0)),
                  pl.BlockSpec((1, 1, cp, LANES), lambda i, j: (i, 1, 0, 0)),
                  _const_spec(tabs["c2s"].shape),
                  pl.BlockSpec((1, t, LANES), lambda i, j: (i, 0, 2)),
                  pl.BlockSpec((1, t, LANES), lambda i, j: (i, 0, 3)),
                  win_spec(2, 0), win_spec(1, 0), win_spec(0, 0),
                  win_spec(2, 1), win_spec(1, 1), win_spec(0, 1),
                  _const_spec(tabs["selb"].shape), _const_spec(tabs["winb"].shape),
                  _const_spec(tabs["cnear"].shape), _const_spec(tabs["e3"].shape),
                  _const_spec(tabs["e3big"].shape)],
        out_specs=pl.BlockSpec((1, TQ, NSA_HEADS * HEAD_DIM), lambda i, j: (i, j, 0)),
        scratch_shapes=[pltpu.VMEM((NSA_GROUP, TQ, LANES), F32), pltpu.VMEM((NSA_GROUP, TQ, LANES), F32),
                        pltpu.VMEM((NSA_GROUP, TQ, LANES), F32)],
        compiler_params=_cparams(("parallel", "parallel")),
        name="nsa_prompt",
    )(tabs["c31"], q, gl, ckv, ckv, tabs["c2s"], kvb, kvb, kvwb, kvwb, kvwb, kvwb, kvwb, kvwb,
      tabs["selb"], tabs["winb"], tabs["cnear"], tabs["e3"], tabs["e3big"])


def _prev_diag_bias(tab):
    big = 1 << 30
    return jnp.stack([_toeplitz_bias(tab, TQ, TQ, TQ, 0, big), _toeplitz_bias(tab, TQ, TQ, 0, 0, big)], axis=1)


def nsa_prompt_tables(tab_n, t):
    nq = t // TQ
    cp = t // CMP_STRIDE + LANES
    near = TQ // CMP_STRIDE + CMP_PAD
    cnear = _toeplitz_bias(tab_n, TQ, near * CMP_STRIDE, CMP_PAD * CMP_STRIDE - (CMP_LEN - 1), 0, 1 << 30)
    return dict(
        c31=tab_n[N_BUCKETS - 1].astype(F32),
        selb=_prev_diag_bias(tab_n),
        winb=_toeplitz_bias(tab_n, TQ, 3 * TQ, 2 * TQ, 0, NSA_WINDOW - 1),
        cnear=cnear[:, :, ::CMP_STRIDE],
        c2s=jnp.asarray(_cmp_to_sel(t // CMP_STRIDE - 1, t // SEL_BLOCK, cp, LANES)),
        e3=jnp.asarray(_block_expand(nq, LANES, TQ), BF16),
        e3big=jnp.asarray(_block_expand(max(nq // FAR_TILES, 1), LANES, FAR_TILES * TQ), BF16),
    )


def _diff_prompt_body(sc_ref, q_ref, k_ref, v_ref, bias_ref, sub_ref, o_ref, m_sc, l_sc, acc_sc):
    h = pl.program_id(1)
    qi = pl.program_id(2)
    tq = q_ref.shape[1]
    q = q_ref[0] * (HEAD_DIM ** -0.5)
    lane = lax.broadcasted_iota(jnp.int32, q.shape, 1)
    q2 = jnp.concatenate([jnp.where(lane < HEAD_DIM, q, 0.0), jnp.where(lane >= HEAD_DIM, q, 0.0)], axis=0).astype(BF16)
    c31 = sc_ref[2 + h]
    _flash_init(0, m_sc, l_sc, acc_sc)

    def step(k0, size, kind, tile):
        del tile
        s = _dot_nt(q2, k_ref[0, pl.ds(k0, size), :])
        vblk = v_ref[0, pl.ds(k0, size), :]
        if kind is None:
            _flash_update(0, s, vblk, m_sc, l_sc, acc_sc, shift=c31)
        else:
            bt = bias_ref[0, kind]
            _flash_update(0, s + jnp.concatenate([bt, bt], axis=0), vblk, m_sc, l_sc, acc_sc)

    _causal_tiles(qi, step)
    o12 = acc_sc[0] / l_sc[0]
    o = o12[:tq] - sc_ref[0] * o12[tq:]
    y = (o * lax.rsqrt(jnp.mean(o * o, axis=-1, keepdims=True) + SUBLN_EPS)) * sub_ref[...]
    o_ref[0] = y * sc_ref[1]


def diff_prompt(dq, dkvb, scal, bias, subln):
    b, t, _ = dq.shape
    nq = t // TQ
    return pl.pallas_call(
        _diff_prompt_body,
        out_shape=jax.ShapeDtypeStruct((b, t, DIFF_HEADS * DIFF_VDIM), F32),
        grid=(b, DIFF_HEADS, nq),
        in_specs=[_smem_spec(),
                  pl.BlockSpec((1, TQ, LANES), lambda i, h, j: (i, j, h)),
                  pl.BlockSpec((1, t, LANES), lambda i, h, j: (i, 0, h)),
                  pl.BlockSpec((1, t, LANES), lambda i, h, j: (i, 0, DIFF_HEADS + h)),
                  pl.BlockSpec((1, 2, TQ, TQ), lambda i, h, j: (h, 0, 0, 0)),
                  _const_spec((1, DIFF_VDIM))],
        out_specs=pl.BlockSpec((1, TQ, LANES), lambda i, h, j: (i, j, h)),
        scratch_shapes=[pltpu.VMEM((1, 2 * TQ, LANES), F32), pltpu.VMEM((1, 2 * TQ, LANES), F32),
                        pltpu.VMEM((1, 2 * TQ, LANES), F32)],
        compiler_params=_cparams(("parallel", "parallel", "parallel")),
        name="diff_prompt",
    )(scal, dq, dkvb, dkvb, bias, subln.reshape(1, DIFF_VDIM))


NSA_PAGES_PER_STEP = 16
DIFF_PAGES_PER_STEP = 8
NEW_ROWS = 8


def _pad_rows(x, rows):
    return jnp.concatenate([x, jnp.zeros((rows - x.shape[0], x.shape[1]), x.dtype)], axis=0)


def _nsa_dec_body(pt_ref, *refs, n_pg, n_ch, past, cpd, n_tok):
    del pt_ref
    pages = refs[:n_pg]
    (qz_ref, glr_ref, newkv_ref, wbuf_ref, neww_ref, w1_ref, w2_ref, c2s_ref, cmpb_ref, selnear_ref,
     winb_ref, c31_ref, e3_ref, rsum_ref, rexp_ref, o_ref, p1k, p2k, p1v, p2v, sc_sc, vt_sc) = refs[n_pg:]
    c = pl.program_id(1)
    page_rows = pages[0].shape[2]
    blk_per_page = page_rows // CMP_STRIDE
    qz = qz_ref[0] * (HEAD_DIM ** -0.5)
    qzb = qz.astype(BF16)

    @pl.when(c == 0)
    def _():
        for ref in (p1k, p2k, p1v, p2v):
            ref[...] = jnp.zeros(ref.shape, F32)

    for i in range(n_pg):
        page = pages[i][0]
        pg = c * n_pg + i
        blk0 = pl.multiple_of(CMP_PAD + pg * blk_per_page, SUBLANES)
        xk = page[0:LANES, :].T.reshape(blk_per_page, CMP_STRIDE, LANES)
        xv = page[LANES:2 * LANES, :].T.reshape(blk_per_page, CMP_STRIDE, LANES)
        p1k[pl.ds(blk0, blk_per_page), :] = (xk * w1_ref[0][None]).sum(axis=1)
        p2k[pl.ds(blk0, blk_per_page), :] = (xk * w2_ref[0][None]).sum(axis=1)
        p1v[pl.ds(blk0, blk_per_page), :] = (xv * w1_ref[1][None]).sum(axis=1)
        p2v[pl.ds(blk0, blk_per_page), :] = (xv * w2_ref[1][None]).sum(axis=1)
        sc_sc[c, :, i * page_rows:(i + 1) * page_rows] = _dot(qzb, page[2 * LANES:3 * LANES, :].astype(BF16))
        vt_sc[c, :, i * page_rows:(i + 1) * page_rows] = page[3 * LANES:4 * LANES, :].astype(BF16)

    @pl.when(c == n_ch - 1)
    def _():
        nk = newkv_ref[0]
        s_new = _dot_nt(qzb, _pad_rows(nk[:, 2 * LANES:3 * LANES], LANES).astype(BF16))
        v_new = _pad_rows(nk[:, 3 * LANES:4 * LANES], LANES).astype(BF16)

        ck = p1k[...] + pltpu.roll(p2k[...], cpd - 1, 0)
        cv = p1v[...] + pltpu.roll(p2v[...], cpd - 1, 0)
        cb = cmpb_ref[...]
        (p_cmp,) = _masked_softmax_parts([(_dot_nt(qz, ck, HI) + cb, cb > 0.5 * NEG)])
        o_cmp = _dot(p_cmp, cv, HI)
        imp = _dot(_dot(rsum_ref[...], p_cmp, HI), c2s_ref[...], HI)
        rows = lax.broadcasted_iota(jnp.int32, (imp.shape[0], 1), 0)
        sel = _select_blocks(_block_scores(imp, past + rows % n_tok), N_SELECT)
        sel = _dot(rexp_ref[...], sel).astype(BF16)

        c31 = c31_ref[...][:, :1]
        selnear = selnear_ref[...]
        chunk = n_pg * page_rows
        parts = []
        for ci in range(n_ch):
            s = sc_sc[ci] + (_dot(sel, e3_ref[ci]) - 1.0) * 1e30
            if ci < n_ch - 1:
                s = s + c31
            else:
                s = s + jnp.concatenate([jnp.broadcast_to(c31, (s.shape[0], chunk - LANES)), selnear[:, :LANES]], axis=1)
            parts.append(s)
        parts.append(s_new + (_dot(sel, e3_ref[n_ch][:, :LANES]) - 1.0) * 1e30 + selnear[:, LANES:])
        m = parts[0].max(axis=-1, keepdims=True)
        for s in parts[1:]:
            m = jnp.maximum(m, s.max(axis=-1, keepdims=True))
        den = jnp.zeros_like(m)
        acc = jnp.zeros((m.shape[0], LANES), F32)
        for ci, s in enumerate(parts):
            p = jnp.exp(s - m)
            den = den + p.sum(axis=-1, keepdims=True)
            if ci < n_ch:
                acc = acc + _dot_nt(p.astype(BF16), vt_sc[ci])
            else:
                acc = acc + _dot(p.astype(BF16), v_new)
        o_sel = acc / den

        wb = wbuf_ref[0]
        nw = neww_ref[0]
        kcat = jnp.concatenate([wb[:, :LANES], _pad_rows(nw[:, :LANES], LANES)], axis=0).astype(BF16)
        vcat = jnp.concatenate([wb[:, LANES:], _pad_rows(nw[:, LANES:], LANES)], axis=0).astype(BF16)
        wbias = winb_ref[...]
        (p_win,) = _masked_softmax_parts([(_dot_nt(qzb, kcat) + wbias, wbias > 0.5 * NEG)])
        o_win = _dot(p_win.astype(BF16), vcat)
        gates = jax.nn.sigmoid(glr_ref[0])
        o_ref[0] = gates[:, 0:1] * o_cmp + gates[:, 1:2] * o_sel + gates[:, 2:3] * o_win


def _row_tables(table, head, tok):
    return table[head, tok]


def nsa_decode_tables(tab_n, past, n_tok, wlen, n_ch, n_pg, page_rows):
    rows = NSA_KV_HEADS * n_tok * NSA_GROUP
    g = np.arange(rows) // (n_tok * NSA_GROUP)
    t = (np.arange(rows) // NSA_GROUP) % n_tok
    head = g * NSA_GROUP + np.arange(rows) % NSA_GROUP
    cpd = past // CMP_STRIDE + LANES
    n_sel_pad = 2 * LANES
    big = 1 << 30
    cmpb = _toeplitz_bias(tab_n, n_tok, cpd * CMP_STRIDE, past + CMP_PAD * CMP_STRIDE - (CMP_LEN - 1), 0, big)
    cmpb = _row_tables(cmpb[:, :, ::CMP_STRIDE], head, t)
    valid_cp = (np.arange(cpd) >= CMP_PAD)[None, :]
    cmpb = jnp.where(valid_cp, cmpb, NEG)
    selnear = _row_tables(_toeplitz_bias(tab_n, n_tok, 2 * LANES, LANES, 0, big), head, t)
    winb = _row_tables(_toeplitz_bias(tab_n, n_tok, wlen + LANES, wlen, 0, NSA_WINDOW - 1), head, t)
    c31 = jnp.broadcast_to(tab_n[N_BUCKETS - 1, head].astype(F32)[:, None], (rows, LANES))
    n_cmp = (past + SEL_BLOCK) // CMP_STRIDE - 1
    n_sel = (past + SEL_BLOCK) // SEL_BLOCK
    rsum = (np.arange(rows)[None, :] // NSA_GROUP == np.arange(rows // NSA_GROUP)[:, None]).astype(np.float32)
    return dict(cmpb=cmpb, selnear=selnear, winb=winb, c31=c31,
                c2s=jnp.asarray(_cmp_to_sel(min(n_cmp, cpd - CMP_PAD), n_sel, cpd, n_sel_pad)),
                e3=jnp.asarray(_block_expand(n_ch + 1, n_sel_pad, n_pg * page_rows), BF16),
                rsum=jnp.asarray(rsum), rexp=jnp.asarray(rsum.T))


def nsa_decode(page_table, pool_t, qz, glr, newkv, wbuf, neww, w1, w2, tabs, n_tok):
    s, n_pages = page_table.shape
    page_rows = pool_t.shape[2]
    n_pg = min(NSA_PAGES_PER_STEP, n_pages)
    n_ch = n_pages // n_pg
    past = n_pages * page_rows
    cpd = tabs["cmpb"].shape[1]
    rows = qz.shape[1]

    def page_spec(i):
        return pl.BlockSpec((1, pool_t.shape[1], page_rows), lambda b, c, pt: (pt[b, c * n_pg + i], 0, 0))

    def seq_spec(a):
        return pl.BlockSpec((1,) + a.shape[1:], lambda b, c, pt: (b,) + (0,) * (a.ndim - 1))

    def const(a):
        n = a.ndim
        return pl.BlockSpec(a.shape, lambda b, c, pt: (0,) * n)

    consts = [w1, w2, tabs["c2s"], tabs["cmpb"], tabs["selnear"], tabs["winb"], tabs["c31"], tabs["e3"],
              tabs["rsum"], tabs["rexp"]]
    seqs = [qz, glr, newkv, wbuf, neww]
    grid_spec = pltpu.PrefetchScalarGridSpec(
        num_scalar_prefetch=1,
        grid=(s, n_ch),
        in_specs=[page_spec(i) for i in range(n_pg)] + [seq_spec(a) for a in seqs] + [const(a) for a in consts],
        out_specs=pl.BlockSpec((1, rows, LANES), lambda b, c, pt: (b, 0, 0)),
        scratch_shapes=[pltpu.VMEM((cpd, LANES), F32)] * 4
        + [pltpu.VMEM((n_ch, rows, n_pg * page_rows), F32), pltpu.VMEM((n_ch, LANES, n_pg * page_rows), BF16)],
    )
    return pl.pallas_call(
        functools.partial(_nsa_dec_body, n_pg=n_pg, n_ch=n_ch, past=past, cpd=cpd, n_tok=n_tok),
        out_shape=jax.ShapeDtypeStruct((s, rows, LANES), F32),
        grid_spec=grid_spec,
        compiler_params=_cparams(("parallel", "arbitrary")),
        name="nsa_decode",
    )(page_table, *([pool_t] * n_pg), *seqs, *consts)


def _diff_dec_body(pt_ref, *refs, n_pg, n_ch, past, n_tok, page_rows):
    del pt_ref
    pages = refs[:n_pg]
    sc_ref, qd_ref, newd_ref, near_ref, sub_ref, o_ref, s_sc, v_sc = refs[n_pg:]
    c = pl.program_id(1)
    per_pos = 2 * DIFF_HEADS
    qd = (qd_ref[0] * (HEAD_DIM ** -0.5)).astype(BF16)
    for i in range(n_pg):
        pg = c * n_pg + i
        for h in range(DIFF_HEADS):
            kh = pages[i][0, pl.ds(h, page_rows, stride=per_pos), :]
            vh = pages[i][0, pl.ds(DIFF_HEADS + h, page_rows, stride=per_pos), :]
            s_sc[c, h, :, i * page_rows:(i + 1) * page_rows] = _dot_nt(qd[h], kh.astype(BF16))
            v_sc[h, pl.ds(pl.multiple_of(pg * page_rows, page_rows), page_rows), :] = vh.astype(BF16)

    @pl.when(c == n_ch - 1)
    def _():
        nd = newd_ref[0]
        chunk = n_pg * page_rows
        for h in range(DIFF_HEADS):
            near = near_ref[h]
            c31 = sc_ref[2 + h]
            s_new = _dot_nt(qd[h], _pad_rows(nd[:, h * LANES:(h + 1) * LANES], LANES).astype(BF16)) + near[:, LANES:]
            v_sc[h, pl.ds(past, LANES), :] = _pad_rows(nd[:, (DIFF_HEADS + h) * LANES:(DIFF_HEADS + h + 1) * LANES], LANES).astype(BF16)
            parts = []
            for ci in range(n_ch):
                s = s_sc[ci, h]
                if ci < n_ch - 1:
                    s = s + c31
                else:
                    s = s + jnp.concatenate([jnp.full((s.shape[0], chunk - LANES), c31, F32), near[:, :LANES]], axis=1)
                parts.append(s)
            parts.append(s_new)
            m = parts[0].max(axis=-1, keepdims=True)
            for s in parts[1:]:
                m = jnp.maximum(m, s.max(axis=-1, keepdims=True))
            den = jnp.zeros_like(m)
            acc = jnp.zeros((m.shape[0], LANES), F32)
            for ci, s in enumerate(parts):
                p = jnp.exp(s - m)
                den = den + p.sum(axis=-1, keepdims=True)
                acc = acc + _dot(p.astype(BF16), v_sc[h, ci * chunk:ci * chunk + s.shape[1], :])
            o12 = acc / den
            o = o12[:n_tok] - sc_ref[0] * o12[n_tok:]
            y = (o * lax.rsqrt(jnp.mean(o * o, axis=-1, keepdims=True) + SUBLN_EPS)) * sub_ref[...]
            o_ref[0, h] = y * sc_ref[1]


def diff_decode_tables(tab_d, n_tok):
    near = _toeplitz_bias(tab_d, n_tok, 2 * LANES, LANES, 0, 1 << 30)
    return jnp.concatenate([near, near], axis=1)


def diff_decode(page_table, pool_rows, qd, newd, scal, near, subln, n_tok, page_rows):
    s, n_pages = page_table.shape
    n_pg = min(DIFF_PAGES_PER_STEP, n_pages)
    n_ch = n_pages // n_pg
    past = n_pages * page_rows

    def page_spec(i):
        return pl.BlockSpec((1,) + pool_rows.shape[1:], lambda b, c, pt: (pt[b, c * n_pg + i], 0, 0))

    grid_spec = pltpu.PrefetchScalarGridSpec(
        num_scalar_prefetch=1,
        grid=(s, n_ch),
        in_specs=[page_spec(i) for i in range(n_pg)] + [
            pl.BlockSpec(memory_space=pltpu.SMEM),
            pl.BlockSpec((1,) + qd.shape[1:], lambda b, c, pt: (b, 0, 0, 0)),
            pl.BlockSpec((1,) + newd.shape[1:], lambda b, c, pt: (b, 0, 0)),
            pl.BlockSpec(near.shape, lambda b, c, pt: (0, 0, 0)),
            pl.BlockSpec((1, DIFF_VDIM), lambda b, c, pt: (0, 0))],
        out_specs=pl.BlockSpec((1, DIFF_HEADS, n_tok, LANES), lambda b, c, pt: (b, 0, 0, 0)),
        scratch_shapes=[pltpu.VMEM((n_ch, DIFF_HEADS, 2 * n_tok, n_pg * page_rows), F32),
                        pltpu.VMEM((DIFF_HEADS, past + LANES, LANES), BF16)],
    )
    return pl.pallas_call(
        functools.partial(_diff_dec_body, n_pg=n_pg, n_ch=n_ch, past=past, n_tok=n_tok, page_rows=page_rows),
        out_shape=jax.ShapeDtypeStruct((s, DIFF_HEADS, n_tok, LANES), F32),
        grid_spec=grid_spec,
        compiler_params=_cparams(("parallel", "arbitrary")),
        name="diff_decode",
    )(page_table, *([pool_rows] * n_pg), scal, qd, newd, near, subln.reshape(1, DIFF_VDIM))


def _sink_attention(qzb, kcat, vcat, bias, ok, sink):
    s = _dot_nt(qzb, kcat) + bias
    msk = (bias > 0.5 * NEG) & ok
    sm = jnp.where(msk, s, NEG)
    m = jnp.maximum(jnp.max(sm, axis=-1, keepdims=True), sink)
    p = jnp.where(msk, jnp.exp(sm - m), 0.0)
    den = jnp.sum(p, axis=-1, keepdims=True) + jnp.exp(sink - m)
    p = p / jnp.maximum(den, 1e-30)
    return _dot(p.astype(BF16), vcat)


def _swa_prompt_body(sink_ref, q_ref, kp_ref, kc_ref, vp_ref, vc_ref, bias_ref, o_ref):
    qi = pl.program_id(1)
    tq = q_ref.shape[1]
    q = q_ref[0] * (HEAD_DIM ** -0.5)
    kcat = jnp.concatenate([kp_ref[0], kc_ref[0]], axis=0)
    vcat = jnp.concatenate([vp_ref[0], vc_ref[0]], axis=0)
    col = lax.broadcasted_iota(jnp.int32, (tq, kcat.shape[0]), 1)
    ok = (col >= SWA_WINDOW) | (qi > 0)
    pieces = []
    for g in range(SWA_KV_HEADS):
        for h in range(SWA_GROUP):
            hh = g * SWA_GROUP + h
            qzb = _place_half(q[:, hh * HEAD_DIM:(hh + 1) * HEAD_DIM], g).astype(BF16)
            o = _sink_attention(qzb, kcat, vcat, bias_ref[hh], ok, sink_ref[hh])
            pieces.append(o[:, g * HEAD_DIM:(g + 1) * HEAD_DIM])
    o_ref[0] = jnp.concatenate(pieces, axis=1)


def swa_prompt(q, kvb, sinks, tab_s):
    b, t, _ = q.shape
    bias = _toeplitz_bias(tab_s, TQ, SWA_WINDOW + TQ, SWA_WINDOW, 0, SWA_WINDOW - 1)
    per = TQ // SWA_WINDOW

    def prev_spec(col):
        return pl.BlockSpec((1, SWA_WINDOW, LANES), lambda i, j: (i, jnp.maximum(per * j - 1, 0), col))

    def cur_spec(col):
        return pl.BlockSpec((1, TQ, LANES), lambda i, j: (i, j, col))

    return pl.pallas_call(
        _swa_prompt_body,
        out_shape=jax.ShapeDtypeStruct((b, t, SWA_HEADS * HEAD_DIM), F32),
        grid=(b, t // TQ),
        in_specs=[_smem_spec(), pl.BlockSpec((1, TQ, SWA_HEADS * HEAD_DIM), lambda i, j: (i, j, 0)),
                  prev_spec(0), cur_spec(0), prev_spec(1), cur_spec(1), _const_spec(bias.shape)],
        out_specs=pl.BlockSpec((1, TQ, SWA_HEADS * HEAD_DIM), lambda i, j: (i, j, 0)),
        compiler_params=_cparams(("parallel", "parallel")),
        name="swa_prompt",
    )(sinks.astype(F32), q, kvb, kvb, kvb, kvb, bias)


def _swa_sample_body(qz_ref, buf_ref, new_ref, bias_ref, sink_ref, o_ref):
    qzb = (qz_ref[0] * (HEAD_DIM ** -0.5)).astype(BF16)
    buf = buf_ref[0]
    new = new_ref[0]
    kcat = jnp.concatenate([buf[:, :LANES], _pad_rows(new[:, :LANES], LANES)], axis=0).astype(BF16)
    vcat = jnp.concatenate([buf[:, LANES:], _pad_rows(new[:, LANES:], LANES)], axis=0).astype(BF16)
    o_ref[0] = _sink_attention(qzb, kcat, vcat, bias_ref[...], True, sink_ref[...][:, :1])


def swa_sample(qz, buf, new, sinks, tab_s, n_tok):
    s, rows, _ = qz.shape
    wlen = buf.shape[1]
    g = np.arange(rows) // (n_tok * SWA_GROUP)
    t = (np.arange(rows) // SWA_GROUP) % n_tok
    head = g * SWA_GROUP + np.arange(rows) % SWA_GROUP
    bias = _row_tables(_toeplitz_bias(tab_s, n_tok, wlen + LANES, wlen, 0, SWA_WINDOW - 1), head, t)
    sink_rows = jnp.broadcast_to(sinks.astype(F32)[head][:, None], (rows, LANES))
    return pl.pallas_call(
        _swa_sample_body,
        out_shape=jax.ShapeDtypeStruct((s, rows, LANES), F32),
        grid=(s,),
        in_specs=[pl.BlockSpec((1, rows, LANES), lambda i: (i, 0, 0)),
                  pl.BlockSpec((1, wlen, 2 * LANES), lambda i: (i, 0, 0)),
                  pl.BlockSpec((1, NEW_ROWS, 2 * LANES), lambda i: (i, 0, 0)),
                  _const_spec(bias.shape), _const_spec(sink_rows.shape)],
        out_specs=pl.BlockSpec((1, rows, LANES), lambda i: (i, 0, 0)),
        compiler_params=_cparams(("parallel",)),
        name="swa_sample",
    )(qz, buf, new, bias, sink_rows)


RWKV_CHUNK = 64


def _head_sum(x, hsum_ref):
    return _dot2(_split(x), hsum_ref[...])


def _rwkv_prep_body(ur_ref, uk_ref, uv_ref, ul_ref, pr_ref, pk_ref, pv_ref, pl_ref,
                    mur_ref, muk_ref, muv_ref, mul_ref, w0_ref, w2_ref, a0_ref, a2_ref, g2_ref,
                    kk_ref, ka_ref, hsum_ref, r_o, lw_o, k_o, v_o, na_o, b_o, g_o):
    def mix(u_ref, p_ref, mu_ref):
        u = u_ref[...]
        return u + (p_ref[...] - u) * mu_ref[...]

    r = mix(ur_ref, pr_ref, mur_ref)
    k = mix(uk_ref, pk_ref, muk_ref)
    v = mix(uv_ref, pv_ref, muv_ref)
    lo = mix(ul_ref, pl_ref, mul_ref)
    z = -(w0_ref[...] + _dot3(_split(jnp.tanh(lo)), _split(w2_ref[...])))
    softplus = jnp.maximum(z, 0.0) + jnp.log(1.0 + jnp.exp(-jnp.abs(z)))
    wlog = -softplus - 0.5
    a = jax.nn.sigmoid(a0_ref[...] + _dot3(_split(lo), _split(a2_ref[...])))
    g = _dot3(_split(jax.nn.sigmoid(lo)), _split(g2_ref[...]))
    kk = k * kk_ref[...]
    kk = kk / jnp.maximum(jnp.sqrt(_head_sum(kk * kk, hsum_ref)), 1e-12)
    r_o[...] = r
    lw_o[...] = -jnp.exp(wlog)
    k_o[...] = k * (1.0 + (a - 1.0) * ka_ref[...])
    v_o[...] = v
    na_o[...] = -kk
    b_o[...] = kk * a
    g_o[...] = g


def rwkv_prep(us, prevs, mus, w0, w2p, a0, a2p, g2p, k_k, k_a, hsum):
    m = us[0].shape[0]
    tm = min(ROW_TILE, m)
    row = lambda a: pl.BlockSpec((tm, a.shape[1]), lambda i: (i, 0))
    vec = lambda a: a.reshape(1, -1)
    consts = [vec(x) for x in mus] + [vec(w0), w2p, vec(a0), a2p, g2p, vec(k_k), vec(k_a), hsum]
    return pl.pallas_call(
        _rwkv_prep_body,
        out_shape=[jax.ShapeDtypeStruct((m, RWKV_WIDTH), F32)] * 7,
        grid=(m // tm,),
        in_specs=[row(a) for a in us] + [row(a) for a in prevs] + [_const_spec(c.shape) for c in consts],
        out_specs=[pl.BlockSpec((tm, RWKV_WIDTH), lambda i: (i, 0))] * 7,
        compiler_params=_cparams(("parallel",)),
        name="rwkv_prep",
    )(*us, *prevs, *consts)


def _rwkv_chunk_body(r_ref, lw_ref, k_ref, v_ref, a_ref, b_ref, s0_ref, y_ref, sT_ref, st_sc):
    ci = pl.program_id(1)
    n_heads, c, n = r_ref.shape[1:]

    @pl.when(ci == 0)
    def _():
        st_sc[...] = s0_ref[0]

    row = lax.broadcasted_iota(jnp.int32, (c, c), 0)
    col = lax.broadcasted_iota(jnp.int32, (c, c), 1)
    incl = row >= col
    strict = row > col
    ones_incl = incl.astype(BF16)
    eye_c = (row == col).astype(F32)
    eye_n = (lax.broadcasted_iota(jnp.int32, (n, n), 0) == lax.broadcasted_iota(jnp.int32, (n, n), 1)).astype(F32)
    for h in range(n_heads):
        r, lw, k, v, a, b = (ref[0, h] for ref in (r_ref, lw_ref, k_ref, v_ref, a_ref, b_ref))
        lw_hi, lw_lo = _split(lw)
        lw_lo2 = (lw - lw_hi.astype(F32) - lw_lo.astype(F32)).astype(BF16)
        cs = _dot(ones_incl, lw_hi) + (_dot(ones_incl, lw_lo) + _dot(ones_incl, lw_lo2))
        gam = jnp.exp(cs)
        ginv = jnp.exp(-cs)
        to_end = jnp.exp(cs[c - 1:c, :] - cs)
        at = _split(a * jnp.exp(cs - lw))
        rt_f = r * gam
        rt = _split(rt_f)
        bt = _split(b * ginv)
        kt = _split(k * ginv)
        vs = _split(v)
        lb = jnp.where(strict, _dot3(at, bt, NT), 0.0)
        lk = jnp.where(strict, _dot3(at, kt, NT), 0.0)
        pb = _split(jnp.where(incl, _dot3(rt, bt, NT), 0.0))
        pk = _split(jnp.where(incl, _dot3(rt, kt, NT), 0.0))
        tinv = eye_c + lb
        lp = _split(lb)
        covered = 2
        while covered < c:
            lp_f = _dot3(lp, lp)
            lp = _split(lp_f)
            tinv = tinv + _dot3(_split(tinv), lp)
            covered *= 2
        tinv = _split(tinv)
        w = _split(_dot3(tinv, at))
        uv = _split(_dot3(tinv, _split(_dot3(_split(lk), vs))))
        q = rt_f + _dot3(pb, w)
        bh = _split(b * to_end)
        kh = _split(k * to_end)
        tm = eye_n * gam[c - 1:c, :] + _dot3(bh, w, TN)
        bm = _dot3(bh, uv, TN) + _dot3(kh, vs, TN)
        x = _dot3(_split(jnp.concatenate([q, tm], axis=0)), _split(st_sc[h]))
        y_ref[0, h] = x[:c] + _dot3(pb, uv) + _dot3(pk, vs)
        st_sc[h] = x[c:] + bm

    @pl.when(ci == pl.num_programs(1) - 1)
    def _():
        sT_ref[0] = st_sc[...]


def rwkv_chunk(seqs, s0t, chunk):
    b, h, t, n = seqs[0].shape
    seq_spec = pl.BlockSpec((1, h, chunk, n), lambda i, j: (i, 0, j, 0))
    st_spec = pl.BlockSpec((1, h, n, n), lambda i, j: (i, 0, 0, 0))
    return pl.pallas_call(
        _rwkv_chunk_body,
        out_shape=[jax.ShapeDtypeStruct((b, h, t, n), F32), jax.ShapeDtypeStruct((b, h, n, n), F32)],
        grid=(b, t // chunk),
        in_specs=[seq_spec] * 6 + [st_spec],
        out_specs=[seq_spec, st_spec],
        scratch_shapes=[pltpu.VMEM((h, n, n), F32)],
        compiler_params=_cparams(("parallel", "arbitrary")),
        name="rwkv_chunk",
    )(*seqs, s0t)


def _rwkv_post_body(y_ref, r_ref, k_ref, v_ref, g_ref, rk_ref, lnw_ref, lnb_ref, hsum_ref, o_ref):
    y = y_ref[...]
    inv_n = 1.0 / RWKV_N
    mean = _head_sum(y, hsum_ref) * inv_n
    yc = y - mean
    var = _head_sum(yc * yc, hsum_ref) * inv_n
    yn = yc * lax.rsqrt(var + RWKV_GN_EPS) * lnw_ref[...] + lnb_ref[...]
    rk = _head_sum(r_ref[...] * k_ref[...] * rk_ref[...], hsum_ref)
    o_ref[...] = (yn + rk * v_ref[...]) * g_ref[...]


def rwkv_post(y, r, k, v, g, r_k, ln_w, ln_b, hsum):
    m = y.shape[0]
    tm = min(ROW_TILE, m)
    row = pl.BlockSpec((tm, RWKV_WIDTH), lambda i: (i, 0))
    vec = lambda a: a.reshape(1, -1)
    return pl.pallas_call(
        _rwkv_post_body,
        out_shape=jax.ShapeDtypeStruct((m, RWKV_WIDTH), F32),
        grid=(m // tm,),
        in_specs=[row] * 5 + [_const_spec((1, RWKV_WIDTH))] * 3 + [_const_spec(hsum.shape)],
        out_specs=row,
        compiler_params=_cparams(("parallel",)),
        name="rwkv_post",
    )(y, r, k, v, g, vec(r_k), vec(ln_w), vec(ln_b), hsum)


RWKV_SIZES = (RWKV_WIDTH, RWKV_WIDTH, RWKV_WIDTH, DECAY_LORA, AAA_LORA, GATE_LORA)
LORA_WIDTH = DECAY_LORA + AAA_LORA + GATE_LORA
ODD_SIZES = (SWA_HEADS * HEAD_DIM, 2 * SWA_KV_HEADS * HEAD_DIM, 3 * RWKV_WIDTH + LORA_WIDTH)
ODD_GROUPS = ((0, 512, (F32,)), (512, 256, (F32, BF16)), (768, 512, (F32,)), (1280, 512, (F32,)),
              (1792, 512, (F32,)), (2304, LORA_PAD, (F32,)))


def _split_cols(w, sizes):
    offs = np.cumsum([0] + list(sizes))
    return [w[..., int(offs[i]):int(offs[i + 1])] for i in range(len(sizes))]


def odd_weights(w_in):
    return jnp.pad(w_in, ((0, 0), (0, LORA_PAD - LORA_WIDTH))).astype(BF16)


def _lora_rows(w, first):
    return jnp.pad(w.astype(F32), ((first, LORA_PAD - first - w.shape[0]), (0, 0)))


def odd_mixer(x, g, w_odd, swa_buf, wkv0, shift0, prompt, sinks, tab, rw, hsum):
    b, t, d = x.shape
    mu, w0, w2, a0, a2, g2, k_k, k_a, r_k, ln_w, ln_b = rw
    q, kv, kvb, ur, uk, uv, ul = norm_proj(x.reshape(b * t, d), g, w_odd, ODD_GROUPS)
    r3 = lambda a: a.reshape(b, t, a.shape[-1])
    tab_s = tab[:, :SWA_HEADS]
    if prompt:
        o_swa = swa_prompt(r3(q), r3(kvb), sinks, tab_s).reshape(b * t, -1)
        ctx = r3(kv)[:, t - min(SWA_WINDOW, t):]
        shift0 = jnp.zeros((b, 3 * RWKV_WIDTH + LORA_WIDTH), F32)
        wkv0 = jnp.zeros((b, RWKV_HEADS, RWKV_N, RWKV_N), F32)
        chunk = RWKV_CHUNK
        t_pad = t
    else:
        rows = SWA_KV_HEADS * t * SWA_GROUP
        q5 = q.reshape(b, t, SWA_KV_HEADS, SWA_GROUP, HEAD_DIM).transpose(0, 2, 1, 3, 4)
        z = jnp.zeros_like(q5[:, 0])
        qz = jnp.stack([jnp.concatenate([q5[:, 0], z], -1), jnp.concatenate([z, q5[:, 1]], -1)], axis=1).reshape(b, rows, LANES)
        new = jnp.pad(r3(kv), ((0, 0), (0, NEW_ROWS - t), (0, 0)))
        o_rows = swa_sample(qz, swa_buf, new, sinks, tab_s, t)
        o6 = o_rows.reshape(b, SWA_KV_HEADS, t, SWA_GROUP, 2, HEAD_DIM)
        o_swa = jnp.stack([o6[:, 0, :, :, 0], o6[:, 1, :, :, 1]], axis=2).reshape(b * t, SWA_HEADS * HEAD_DIM)
        ctx = jnp.concatenate([swa_buf, r3(kv)], axis=1)[:, t:]
        chunk = NEW_ROWS
        t_pad = NEW_ROWS

    s_r, s_k, s_v, s_l = _split_cols(shift0.astype(F32), (RWKV_WIDTH,) * 3 + (LORA_WIDTH,))
    s_l = jnp.pad(s_l, ((0, 0), (0, LORA_PAD - LORA_WIDTH)))
    us = [ur, uk, uv, ul]
    prevs = [jnp.concatenate([s[:, None], r3(u)[:, :-1]], axis=1).reshape(b * t, -1) for s, u in zip((s_r, s_k, s_v, s_l), us)]
    mu_r, mu_k, mu_v, mu_l = _split_cols(mu.astype(F32), (RWKV_WIDTH,) * 3 + (LORA_WIDTH,))
    mu_l = jnp.pad(mu_l, (0, LORA_PAD - LORA_WIDTH))
    r, lw, k, v, na, bb, gate = rwkv_prep(us, prevs, [mu_r, mu_k, mu_v, mu_l], w0, _lora_rows(w2, 0), a0,
                                          _lora_rows(a2, DECAY_LORA), _lora_rows(g2, DECAY_LORA + AAA_LORA), k_k, k_a, hsum)

    def heads(a):
        a = a.reshape(b, t, RWKV_HEADS, RWKV_N).transpose(0, 2, 1, 3)
        return jnp.pad(a, ((0, 0), (0, 0), (0, t_pad - t), (0, 0)))

    y, st = rwkv_chunk([heads(a) for a in (r, lw, k, v, na, bb)], jnp.swapaxes(wkv0.astype(F32), -1, -2), chunk)
    y = y[:, :, :t].transpose(0, 2, 1, 3).reshape(b * t, RWKV_WIDTH)
    o_rwkv = rwkv_post(y, r, k, v, gate, r_k, ln_w, ln_b, hsum)
    shift = jnp.concatenate([r3(ur)[:, -1], r3(uk)[:, -1], r3(uv)[:, -1], r3(ul)[:, -1, :LORA_WIDTH]], axis=-1)
    return o_swa, o_rwkv, ctx, jnp.swapaxes(st, -1, -2), shift


EVEN_SIZES = (NSA_HEADS * HEAD_DIM, 4 * NSA_KV_HEADS * HEAD_DIM, 2 * NSA_KV_HEADS * HEAD_DIM, 3 * NSA_HEADS,
              2 * DIFF_HEADS * HEAD_DIM, 2 * DIFF_HEADS * DIFF_VDIM)
EVEN_GROUPS = ((0, 512, (F32,)), (512, 512, (F32, BF16)), (1024, 256, (F32, BF16)), (1280, 512, (F32,)),
               (1792, 1024, (F32, BF16)), (2816, 128, (F32,)))


def even_weights(w_in):
    wq, wkv, wkvw, wgl, wdq, wdkv = _split_cols(w_in, EVEN_SIZES)
    wgl = jnp.pad(wgl, ((0, 0), (0, LANES - wgl.shape[1])))
    return jnp.concatenate([wq, wkv, wkvw, wdq, wdkv, wgl], axis=1).astype(BF16)


def cmp_lane_weights(cmp_w):
    wt = jax.nn.softmax(cmp_w.astype(F32), axis=-1)
    wl = jnp.repeat(jnp.swapaxes(wt, 1, 2), HEAD_DIM, axis=2)
    return wl[:, :CMP_STRIDE], wl[:, CMP_STRIDE:]


def diff_scalars(lq, layer, tab_d):
    lam_init = 0.8 - 0.6 * math.exp(-0.3 * layer)
    lq = lq.astype(F32)
    lam = jnp.exp(jnp.sum(lq[0] * lq[1])) - jnp.exp(jnp.sum(lq[2] * lq[3])) + lam_init
    return jnp.concatenate([jnp.stack([lam, jnp.asarray(1.0 - lam_init, F32)]), tab_d[N_BUCKETS - 1].astype(F32)])


def even_prompt(x, g, w_even, w1, w2, tab, scal, subln):
    b, t, d = x.shape
    q, kv, kvb, kvw, kvwb, dq, dkv, dkvb, gl = norm_proj(x.reshape(b * t, d), g, w_even, EVEN_GROUPS)
    r3 = lambda a: a.reshape(b, t, a.shape[-1])
    tab_n, tab_d = tab[:, :NSA_HEADS], tab[:, NSA_HEADS:NSA_HEADS + DIFF_HEADS]
    tabs = nsa_prompt_tables(tab_n, t)
    ckv = compress_prompt(r3(kv), w1, w2, tabs["c2s"].shape[0])
    o_nsa = nsa_prompt(r3(q), r3(gl), ckv, r3(kvb), r3(kvwb), tabs)
    o_diff = diff_prompt(r3(dq), r3(dkvb), scal, _prev_diag_bias(tab_d), subln)
    return o_nsa.reshape(b * t, -1), o_diff.reshape(b * t, -1), kv, kvw, dkv


def even_sample(x, g, w_even, nsa_pool, diff_pool, win_buf, page_table, w1, w2, tab, scal, subln):
    s, n_tok, d = x.shape
    q, kv, _, kvw, _, dq, dkv, _, gl = norm_proj(x.reshape(s * n_tok, d), g, w_even, EVEN_GROUPS)
    tab_n, tab_d = tab[:, :NSA_HEADS], tab[:, NSA_HEADS:NSA_HEADS + DIFF_HEADS]
    n_pages = page_table.shape[1]
    n_pool, page_rows = nsa_pool.shape[:2]
    past = n_pages * page_rows
    wlen = win_buf.shape[1]
    n_pg = min(NSA_PAGES_PER_STEP, n_pages)
    rows = NSA_KV_HEADS * n_tok * NSA_GROUP

    q5 = q.reshape(s, n_tok, NSA_KV_HEADS, NSA_GROUP, HEAD_DIM).transpose(0, 2, 1, 3, 4)
    z = jnp.zeros_like(q5[:, 0])
    qz = jnp.stack([jnp.concatenate([q5[:, 0], z], -1), jnp.concatenate([z, q5[:, 1]], -1)], axis=1).reshape(s, rows, LANES)
    glr = gl[:, :3 * NSA_HEADS].reshape(s, n_tok, NSA_KV_HEADS, NSA_GROUP, 3).transpose(0, 2, 1, 3, 4).reshape(s, rows, 3)
    glr = jnp.pad(glr, ((0, 0), (0, 0), (0, LANES - 3)))
    pad_tok = lambda a: jnp.pad(a.reshape(s, n_tok, a.shape[-1]), ((0, 0), (0, NEW_ROWS - n_tok), (0, 0)))
    tabs = nsa_decode_tables(tab_n, past, n_tok, wlen, n_pages // n_pg, n_pg, page_rows)
    pool_t = jnp.transpose(nsa_pool, (0, 2, 3, 4, 1)).reshape(n_pool, -1, page_rows)
    o_rows = nsa_decode(page_table, pool_t, qz, glr, pad_tok(kv), win_buf.reshape(s, wlen, -1), pad_tok(kvw),
                        w1, w2, tabs, n_tok)
    o6 = o_rows.reshape(s, NSA_KV_HEADS, n_tok, NSA_GROUP, 2, HEAD_DIM)
    o_nsa = jnp.stack([o6[:, 0, :, :, 0], o6[:, 1, :, :, 1]], axis=2).reshape(s * n_tok, NSA_HEADS * HEAD_DIM)

    dq5 = dq.reshape(s, n_tok, DIFF_HEADS, 2, HEAD_DIM).transpose(0, 2, 3, 1, 4)
    zd = jnp.zeros_like(dq5[:, :, 0])
    qd = jnp.concatenate([jnp.concatenate([dq5[:, :, 0], zd], -1), jnp.concatenate([zd, dq5[:, :, 1]], -1)], axis=2)
    pool_rows = diff_pool.reshape(n_pool, page_rows * 2 * DIFF_HEADS, DIFF_VDIM)
    o_d = diff_decode(page_table, pool_rows, qd, pad_tok(dkv), scal, diff_decode_tables(tab_d, n_tok), subln,
                      n_tok, page_rows)
    o_diff = o_d.transpose(0, 2, 1, 3).reshape(s * n_tok, DIFF_HEADS * DIFF_VDIM)
    return o_nsa, o_diff, kv, kvw, dkv


def kernel(x_prompt, x_sample, cache_nsa_kv, cache_diff_kv, cache_nsa_win, cache_swa, state_rwkv_wkv, state_rwkv_shift, page_table, rel_bias, norm_mix, norm_ffn, norm_final, w_in_even, w_out_even, nsa_cmp_w, diff_lambda, diff_subln, w_in_odd, w_out_odd, swa_sinks, rwkv_mu, rwkv_w0, rwkv_w2, rwkv_a0, rwkv_a2, rwkv_g2, rwkv_k_k, rwkv_k_a, rwkv_r_k, rwkv_ln_w, rwkv_ln_b, ffn_w_gate, ffn_w_up, ffn_w_down):
    b, t, d = x_prompt.shape
    s, n_tok, _ = x_sample.shape
    depth = norm_mix.shape[0]
    assert NSA_WINDOW == 2 * TQ and t % TQ == 0 and TQ >= MAX_DISTANCE and n_tok <= NEW_ROWS and depth > 0
    tab = rel_bias.astype(F32)
    tab_d = tab[:, NSA_HEADS:NSA_HEADS + DIFF_HEADS]
    head_id = np.arange(RWKV_WIDTH) // RWKV_N
    hsum = jnp.asarray((head_id[:, None] == head_id[None, :]).astype(np.float32), BF16)
    xp = x_prompt.reshape(b * t, d)
    xs = x_sample.reshape(s * n_tok, d)
    outs = {name: [] for name in ("nsa_p", "nsa_s", "diff_p", "diff_s", "win_p", "win_s",
                                  "swa_p", "swa_s", "wkv_p", "wkv_s", "sh_p", "sh_s")}
    for l in range(depth):
        if l % 2 == 0:
            e = l // 2
            w_even = even_weights(w_in_even[e])
            w1, w2 = cmp_lane_weights(nsa_cmp_w[e])
            scal = diff_scalars(diff_lambda[e], l, tab_d)
            pa, pb, kv, kvw, dkv = even_prompt(xp.reshape(b, t, d), norm_mix[l], w_even, w1, w2, tab, scal, diff_subln[e])
            sa, sb, skv, skvw, sdkv = even_sample(xs.reshape(s, n_tok, d), norm_mix[l], w_even, cache_nsa_kv[e],
                                                  cache_diff_kv[e], cache_nsa_win[e], page_table, w1, w2, tab, scal,
                                                  diff_subln[e])
            outs["nsa_p"].append(kv.reshape(b, t, 4, NSA_KV_HEADS, HEAD_DIM))
            outs["nsa_s"].append(skv.reshape(s, n_tok, 4, NSA_KV_HEADS, HEAD_DIM))
            outs["diff_p"].append(dkv.reshape(b, t, 2, DIFF_HEADS, DIFF_VDIM))
            outs["diff_s"].append(sdkv.reshape(s, n_tok, 2, DIFF_HEADS, DIFF_VDIM))
            outs["win_p"].append(kvw.reshape(b, t, 2, NSA_KV_HEADS, HEAD_DIM)[:, t - min(NSA_WINDOW, t):])
            new_win = skvw.reshape(s, n_tok, 2, NSA_KV_HEADS, HEAD_DIM).astype(cache_nsa_win.dtype)
            outs["win_s"].append(jnp.concatenate([cache_nsa_win[e], new_win], axis=1)[:, n_tok:])
            w_out = w_out_even[e].astype(BF16)
        else:
            o = l // 2
            rw = (rwkv_mu[o], rwkv_w0[o], rwkv_w2[o], rwkv_a0[o], rwkv_a2[o], rwkv_g2[o], rwkv_k_k[o],
                  rwkv_k_a[o], rwkv_r_k[o].reshape(-1), rwkv_ln_w[o], rwkv_ln_b[o])
            w_odd = odd_weights(w_in_odd[o])
            pa, pb, ctx_p, wkv_p, sh_p = odd_mixer(xp.reshape(b, t, d), norm_mix[l], w_odd, None, None, None, True,
                                                   swa_sinks[o], tab, rw, hsum)
            swa_buf = cache_swa[o].reshape(s, cache_swa.shape[2], -1).astype(F32)
            sa, sb, ctx_s, wkv_s, sh_s = odd_mixer(xs.reshape(s, n_tok, d), norm_mix[l], w_odd, swa_buf,
                                                   state_rwkv_wkv[o], state_rwkv_shift[o], False, swa_sinks[o], tab,
                                                   rw, hsum)
            kv_shape = (2, SWA_KV_HEADS, HEAD_DIM)
            outs["swa_p"].append(ctx_p.reshape(b, -1, *kv_shape))
            outs["swa_s"].append(ctx_s.reshape(s, -1, *kv_shape))
            outs["wkv_p"].append(wkv_p)
            outs["wkv_s"].append(wkv_s)
            outs["sh_p"].append(sh_p)
            outs["sh_s"].append(sh_s)
            w_out = w_out_odd[o].astype(BF16)
        half = pa.shape[1]
        xp = out_proj(xp, pa, pb, w_out[:half], w_out[half:])
        xs = out_proj(xs, sa, sb, w_out[:half], w_out[half:])
        wg, wu, wd = ffn_w_gate[l].astype(BF16), ffn_w_up[l].astype(BF16), ffn_w_down[l].astype(BF16)
        last = l == depth - 1
        xp = ffn(xp, norm_ffn[l], norm_final, wg, wu, wd, last)
        xs = ffn(xs, norm_ffn[l], norm_final, wg, wu, wd, last)
    st = lambda name: jnp.stack(outs[name])
    return (xp.reshape(b, t, d), xs.reshape(s, n_tok, d), st("nsa_p"), st("nsa_s"), st("diff_p"), st("diff_s"),
            st("win_p"), st("win_s"), st("swa_p"), st("swa_s"), st("wkv_p"), st("wkv_s"), st("sh_p"), st("sh_s"))
```

```python
import functools
import math

import numpy as np
import jax
import jax.numpy as jnp
from jax import lax
from jax.experimental import pallas as pl
from jax.experimental.pallas import tpu as pltpu

F32 = jnp.float32
BF16 = jnp.bfloat16
HI = lax.Precision.HIGHEST

HEAD_DIM = 64
NSA_KV_HEADS = 2
NSA_GROUP = 4
NSA_HEADS = NSA_KV_HEADS * NSA_GROUP
CMP_STRIDE = 16
CMP_LEN = 32
SEL_BLOCK = 64
N_SELECT = 16
NSA_WINDOW = 512
DIFF_HEADS = 4
DIFF_VDIM = 128
SWA_HEADS = 8
SWA_KV_HEADS = 2
SWA_GROUP = 4
SWA_WINDOW = 128
RWKV_N = 64
RWKV_HEADS = 8
RWKV_WIDTH = RWKV_N * RWKV_HEADS
DECAY_LORA = 32
AAA_LORA = 32
GATE_LORA = 96
LORA_PAD = 256
N_BUCKETS = 32
MAX_DISTANCE = 128
NORM_EPS = 1e-6
SUBLN_EPS = 1e-5
RWKV_GN_EPS = 64e-5
NEG = -1e30
FORCE = 1e6
REMOVED = -3e38

LANES = 128
SUBLANES = 8
VMEM_LIMIT_BYTES = 56 * 1024 * 1024

TQ = 256
FAR_TILES = 4
ROW_TILE = 512
CMP_PAD = 16

NN = (((1,), (0,)), ((), ()))
NT = (((1,), (1,)), ((), ()))
TN = (((0,), (0,)), ((), ()))


def _cparams(sem):
    return pltpu.CompilerParams(dimension_semantics=sem, vmem_limit_bytes=VMEM_LIMIT_BYTES)


def _const_spec(shape):
    n = len(shape)
    return pl.BlockSpec(shape, lambda *_: (0,) * n)


def _smem_spec():
    return pl.BlockSpec(memory_space=pltpu.SMEM)


def _dot(a, b, precision=None):
    return jnp.dot(a, b, preferred_element_type=F32, precision=precision)


def _dot_nt(a, b, precision=None):
    return lax.dot_general(a, b, NT, preferred_element_type=F32, precision=precision)


def _split(x):
    hi = x.astype(BF16)
    return hi, (x - hi.astype(F32)).astype(BF16)


def _dot3(a, b, dims=NN):
    f = lambda x, y: lax.dot_general(x, y, dims, preferred_element_type=F32)
    return f(a[0], b[0]) + (f(a[1], b[0]) + f(a[0], b[1]))


def _dot2(a, b, dims=NN):
    f = lambda x, y: lax.dot_general(x, y, dims, preferred_element_type=F32)
    return f(a[0], b) + f(a[1], b)


def _norm_proj_body(x_ref, g_ref, w_ref, *out_refs, groups):
    x = x_ref[...]
    h = (x * lax.rsqrt(jnp.mean(x * x, axis=-1, keepdims=True) + NORM_EPS)) * g_ref[...]
    hb = h.astype(BF16)
    i = 0
    for off, wd, dts in groups:
        r = _dot(hb, w_ref[:, off:off + wd])
        for dt in dts:
            out_refs[i][...] = r.astype(dt)
            i += 1


def norm_proj(x2d, g, w_bf16, groups):
    m, d = x2d.shape
    tm = min(ROW_TILE, m)
    out_shape, out_specs = [], []
    for _, wd, dts in groups:
        for dt in dts:
            out_shape.append(jax.ShapeDtypeStruct((m, wd), dt))
            out_specs.append(pl.BlockSpec((tm, wd), lambda i: (i, 0)))
    return pl.pallas_call(
        functools.partial(_norm_proj_body, groups=groups),
        out_shape=out_shape,
        grid=(m // tm,),
        in_specs=[pl.BlockSpec((tm, d), lambda i: (i, 0)), _const_spec((1, d)), _const_spec(w_bf16.shape)],
        out_specs=out_specs,
        compiler_params=_cparams(("parallel",)),
        name="norm_proj",
    )(x2d, g.reshape(1, d), w_bf16)


def _out_proj_body(x_ref, a_ref, b_ref, wa_ref, wb_ref, o_ref):
    acc = _dot(a_ref[...].astype(BF16), wa_ref[...]) + _dot(b_ref[...].astype(BF16), wb_ref[...])
    o_ref[...] = x_ref[...] + acc


def out_proj(x2d, a, b, wa, wb):
    m, d = x2d.shape
    tm = min(ROW_TILE, m)
    return pl.pallas_call(
        _out_proj_body,
        out_shape=jax.ShapeDtypeStruct((m, d), F32),
        grid=(m // tm,),
        in_specs=[pl.BlockSpec((tm, d), lambda i: (i, 0)),
                  pl.BlockSpec((tm, a.shape[1]), lambda i: (i, 0)),
                  pl.BlockSpec((tm, b.shape[1]), lambda i: (i, 0)),
                  _const_spec(wa.shape), _const_spec(wb.shape)],
        out_specs=pl.BlockSpec((tm, d), lambda i: (i, 0)),
        compiler_params=_cparams(("parallel",)),
        name="out_proj",
    )(x2d, a, b, wa, wb)


def _ffn_body(x_ref, g_ref, gf_ref, wg_ref, wu_ref, wd_ref, o_ref, h_sc, acc_sc, *, final_norm):
    f = pl.program_id(1)

    @pl.when(f == 0)
    def _():
        x = x_ref[...]
        h = (x * lax.rsqrt(jnp.mean(x * x, axis=-1, keepdims=True) + NORM_EPS)) * g_ref[...]
        h_sc[...] = h.astype(BF16)
        acc_sc[...] = jnp.zeros_like(acc_sc)

    hb = h_sc[...]
    gate = _dot(hb, wg_ref[...])
    up = _dot(hb, wu_ref[...])
    act = (gate * jax.nn.sigmoid(gate)) * up
    acc_sc[...] += _dot(act.astype(BF16), wd_ref[...])

    @pl.when(f == pl.num_programs(1) - 1)
    def _():
        y = x_ref[...] + acc_sc[...]
        if final_norm:
            y = (y * lax.rsqrt(jnp.mean(y * y, axis=-1, keepdims=True) + NORM_EPS)) * gf_ref[...]
        o_ref[...] = y


def ffn(x2d, g, g_final, wg, wu, wd, final_norm):
    m, d = x2d.shape
    dff = wg.shape[1]
    tm = min(ROW_TILE, m)
    tf = dff // 2
    return pl.pallas_call(
        functools.partial(_ffn_body, final_norm=final_norm),
        out_shape=jax.ShapeDtypeStruct((m, d), F32),
        grid=(m // tm, dff // tf),
        in_specs=[pl.BlockSpec((tm, d), lambda i, f: (i, 0)), _const_spec((1, d)), _const_spec((1, d)),
                  pl.BlockSpec((d, tf), lambda i, f: (0, f)), pl.BlockSpec((d, tf), lambda i, f: (0, f)),
                  pl.BlockSpec((tf, d), lambda i, f: (f, 0))],
        out_specs=pl.BlockSpec((tm, d), lambda i, f: (i, 0)),
        scratch_shapes=[pltpu.VMEM((tm, d), BF16), pltpu.VMEM((tm, d), F32)],
        compiler_params=_cparams(("parallel", "arbitrary")),
        name="ffn",
    )(x2d, g.reshape(1, d), g_final.reshape(1, d), wg, wu, wd)


def _t5_bucket(dist):
    n = jnp.maximum(dist, 0)
    max_exact = N_BUCKETS // 2
    nf = jnp.maximum(n, 1).astype(F32)
    large = max_exact + (jnp.log(nf / max_exact) / math.log(MAX_DISTANCE / max_exact)
                         * (N_BUCKETS - max_exact)).astype(jnp.int32)
    large = jnp.minimum(large, N_BUCKETS - 1)
    return jnp.where(n < max_exact, n, large)


def _toeplitz_bias(tab, rows, cols, offset, valid_lo, valid_hi):
    length = rows + cols - 1
    d = jnp.arange(length, dtype=jnp.int32) - (cols - 1) + offset
    g = jnp.where((d >= valid_lo) & (d <= valid_hi), tab[_t5_bucket(d)].astype(F32).T, NEG)
    h = g[:, ::-1]
    flat = jnp.tile(h, (1, rows + 1))[:, :rows * (length + 1)].reshape(-1, rows, length + 1)[:, :, :cols]
    return flat[:, ::-1, :]


def _cmp_to_sel(n_cmp, n_sel, rows, cols):
    i = np.arange(n_cmp)[:, None]
    j = np.arange(n_sel)[None, :]
    m = (i * CMP_STRIDE < (j + 1) * SEL_BLOCK) & (i * CMP_STRIDE + CMP_LEN > j * SEL_BLOCK)
    out = np.zeros((rows, cols), np.float32)
    out[CMP_PAD:CMP_PAD + n_cmp, :n_sel] = m
    return out


def _block_expand(n_chunks, n_blocks, chunk_keys):
    c = np.arange(n_chunks)[:, None, None]
    j = np.arange(n_blocks)[None, :, None]
    l = np.arange(chunk_keys)[None, None, :]
    return (j == (c * chunk_keys + l) // SEL_BLOCK).astype(np.float32)


def _place_half(x64, half):
    z = jnp.zeros_like(x64)
    return jnp.concatenate([x64, z], axis=1) if half == 0 else jnp.concatenate([z, x64], axis=1)


def _lane_tile(x, width):
    reps = width // x.shape[1]
    return x if reps == 1 else jnp.concatenate([x] * reps, axis=1)


def _masked_softmax_parts(parts):
    m = None
    for s, msk in parts:
        mm = jnp.max(jnp.where(msk, s, NEG), axis=-1, keepdims=True)
        m = mm if m is None else jnp.maximum(m, mm)
    ps, den = [], None
    for s, msk in parts:
        p = jnp.where(msk, jnp.exp(jnp.where(msk, s, NEG) - m), 0.0)
        ps.append(p)
        d = jnp.sum(p, axis=-1, keepdims=True)
        den = d if den is None else den + d
    inv = 1.0 / jnp.maximum(den, 1e-30)
    return [p * inv for p in ps]


def _select_blocks(score, n_top):
    lane = lax.broadcasted_iota(jnp.int32, score.shape, 1)
    big = score.shape[1]
    sel = jnp.zeros(score.shape, F32)
    sc = score
    for _ in range(n_top):
        m = jnp.max(sc, axis=-1, keepdims=True)
        idx = jnp.min(jnp.where(sc == m, lane, big), axis=-1, keepdims=True)
        hit = lane == idx
        sel = jnp.where(hit & (m > 0.5 * NEG), 1.0, sel)
        sc = jnp.where(hit, REMOVED, sc)
    return sel


def _block_scores(imp, q_pos):
    j = lax.broadcasted_iota(jnp.int32, imp.shape, 1)
    cur = q_pos // SEL_BLOCK
    avail = j * SEL_BLOCK <= q_pos
    forced = (j == 0) | (j == cur) | (j == cur - 1)
    return jnp.where(avail, jnp.where(forced, FORCE, imp), NEG)


def _flash_init(h, m_sc, l_sc, acc_sc):
    m_sc[h] = jnp.full(m_sc.shape[1:], REMOVED, F32)
    l_sc[h] = jnp.zeros(l_sc.shape[1:], F32)
    acc_sc[h] = jnp.zeros(acc_sc.shape[1:], F32)


def _flash_update(h, s, vblk, m_sc, l_sc, acc_sc, shift=None):
    m_prev = m_sc[h]
    m_cur = jnp.max(s, axis=-1, keepdims=True)
    if shift is not None:
        m_cur = m_cur + shift
    m_next = jnp.maximum(m_prev, m_cur)
    alpha = jnp.exp(m_prev - m_next)
    sub = m_next if shift is None else m_next - shift
    p = jnp.exp(s - _lane_tile(sub, s.shape[1]))
    l_sc[h] = alpha * l_sc[h] + jnp.sum(p, axis=-1, keepdims=True)
    acc_sc[h] = alpha * acc_sc[h] + _dot(p.astype(BF16), vblk)
    m_sc[h] = m_next


def _causal_tiles(qi, step):
    n_far = jnp.maximum(qi - 1, 0)
    n_big = n_far // FAR_TILES

    def big_body(kb, carry):
        step(pl.multiple_of(kb * (FAR_TILES * TQ), FAR_TILES * TQ), FAR_TILES * TQ, None, kb)
        return carry

    def far_body(kt, carry):
        step(pl.multiple_of(kt * TQ, TQ), TQ, None, kt)
        return carry

    lax.fori_loop(0, n_big, big_body, 0)
    lax.fori_loop(n_big * FAR_TILES, n_far, far_body, 0)

    @pl.when(qi >= 1)
    def _():
        step(pl.multiple_of((qi - 1) * TQ, TQ), TQ, 0, qi - 1)

    step(pl.multiple_of(qi * TQ, TQ), TQ, 1, qi)


def _compress_body(x_ref, w1_ref, w2_ref, o_ref, *, nblk):
    x = x_ref[0].reshape(nblk, CMP_STRIDE, LANES)
    p1 = (x * w1_ref[0][None]).sum(axis=1)
    p2 = (x * w2_ref[0][None]).sum(axis=1)
    ck = p1 + pltpu.roll(p2, nblk - 1, 0)
    row = lax.broadcasted_iota(jnp.int32, ck.shape, 0)
    o_ref[0, 0] = jnp.zeros(o_ref.shape[2:], F32)
    o_ref[0, 0, CMP_PAD:CMP_PAD + nblk, :] = jnp.where(row < nblk - 1, ck, 0.0)


def compress_prompt(kv, w1, w2, cp):
    b, t, _ = kv.shape
    nblk = t // CMP_STRIDE
    return pl.pallas_call(
        functools.partial(_compress_body, nblk=nblk),
        out_shape=jax.ShapeDtypeStruct((b, 2, cp, LANES), F32),
        grid=(b, 2),
        in_specs=[pl.BlockSpec((1, t, LANES), lambda i, k: (i, 0, k)),
                  pl.BlockSpec((1, CMP_STRIDE, LANES), lambda i, k: (k, 0, 0)),
                  pl.BlockSpec((1, CMP_STRIDE, LANES), lambda i, k: (k, 0, 0))],
        out_specs=pl.BlockSpec((1, 1, cp, LANES), lambda i, k: (i, k, 0, 0)),
        compiler_params=_cparams(("parallel", "parallel")),
        name="compress_prompt",
    )(kv, w1, w2)


def _nsa_prompt_body(c31_ref, q_ref, gl_ref, ck_ref, cv_ref, c2s_ref, selk_ref, selv_ref,
                     wk0_ref, wk1_ref, wk2_ref, wv0_ref, wv1_ref, wv2_ref,
                     selb_ref, winb_ref, cnear_ref, e3_ref, e3big_ref, o_ref, m_sc, l_sc, acc_sc, *, cp):
    qi = pl.program_id(1)
    tq = q_ref.shape[1]
    near = tq // CMP_STRIDE + CMP_PAD
    near0 = pl.multiple_of(qi * (tq // CMP_STRIDE), tq // CMP_STRIDE)
    q = q_ref[0] * (HEAD_DIM ** -0.5)
    gates = jax.nn.sigmoid(gl_ref[0])
    ck = _split(ck_ref[0, 0])
    cv = cv_ref[0, 0].astype(BF16)
    ck_near = _split(ck_ref[0, 0, pl.ds(near0, near), :])
    cv_near = cv_ref[0, 0, pl.ds(near0, near), :].astype(BF16)
    c2s = c2s_ref[...].astype(BF16)
    c2s_near = c2s_ref[pl.ds(near0, near), :].astype(BF16)
    kcat = jnp.concatenate([wk0_ref[0], wk1_ref[0], wk2_ref[0]], axis=0)
    vcat = jnp.concatenate([wv0_ref[0], wv1_ref[0], wv2_ref[0]], axis=0)
    q_pos = qi * tq + lax.broadcasted_iota(jnp.int32, (tq, 1), 0)
    cp_idx = lax.broadcasted_iota(jnp.int32, (tq, cp), 1)
    far_mask = (cp_idx >= CMP_PAD) & (cp_idx < near0)
    near_lane = lax.broadcasted_iota(jnp.int32, (tq, near), 1)
    near_ok = (near_lane >= CMP_PAD) | (qi > 0)
    wcol = lax.broadcasted_iota(jnp.int32, (tq, 3 * tq), 1)
    win_ok = wcol >= (2 - qi) * tq

    qzb_all, o_cmp_all, imps = [], [], []
    for g in range(NSA_KV_HEADS):
        psum_far = jnp.zeros((tq, cp), F32)
        psum_near = jnp.zeros((tq, near), F32)
        for h in range(NSA_GROUP):
            hh = g * NSA_GROUP + h
            qz = _split(_place_half(q[:, hh * HEAD_DIM:(hh + 1) * HEAD_DIM], g))
            qzb_all.append(qz[0])
            s_far = _dot3(qz, ck, NT) + c31_ref[hh]
            nb = cnear_ref[hh]
            s_near = _dot3(qz, ck_near, NT) + nb
            p_far, p_near = _masked_softmax_parts([(s_far, far_mask), (s_near, (nb > 0.5 * NEG) & near_ok)])
            o_cmp_all.append(_dot(p_far.astype(BF16), cv) + _dot(p_near.astype(BF16), cv_near))
            psum_far = psum_far + p_far
            psum_near = psum_near + p_near
        imps.append(_dot2(_split(psum_far), c2s) + _dot2(_split(psum_near), c2s_near))
    sel_all = _select_blocks(_block_scores(jnp.concatenate(imps, axis=0), jnp.concatenate([q_pos] * len(imps), axis=0)),
                             N_SELECT).astype(BF16)

    pieces = []
    for g in range(NSA_KV_HEADS):
        qzb = qzb_all[g * NSA_GROUP:(g + 1) * NSA_GROUP]
        o_cmp = o_cmp_all[g * NSA_GROUP:(g + 1) * NSA_GROUP]
        sel = sel_all[g * tq:(g + 1) * tq]

        for h in range(NSA_GROUP):
            _flash_init(h, m_sc, l_sc, acc_sc)

        def sel_step(k0, size, kind, tile, g=g, qzb=qzb, sel=sel):
            kblk = selk_ref[0, pl.ds(k0, size), :]
            vblk = selv_ref[0, pl.ds(k0, size), :]
            expand = e3big_ref[tile] if size != tq else e3_ref[tile]
            madd = (_dot(sel, expand) - 1.0) * 1e30
            for h in range(NSA_GROUP):
                hh = g * NSA_GROUP + h
                s = _dot_nt(qzb[h], kblk)
                if kind is None:
                    _flash_update(h, s + madd, vblk, m_sc, l_sc, acc_sc, shift=c31_ref[hh])
                else:
                    _flash_update(h, s + (madd + selb_ref[hh, kind]), vblk, m_sc, l_sc, acc_sc)

        _causal_tiles(qi, sel_step)

        for h in range(NSA_GROUP):
            hh = g * NSA_GROUP + h
            wb = winb_ref[hh]
            s = _dot_nt(qzb[h], kcat) + wb
            msk = (wb > 0.5 * NEG) & win_ok
            (p,) = _masked_softmax_parts([(s, msk)])
            o_win = _dot(p.astype(BF16), vcat)
            o_sel = acc_sc[h] / l_sc[h]
            c = g * NSA_GROUP * 3 + h * 3
            o = gates[:, c:c + 1] * o_cmp[h] + gates[:, c + 1:c + 2] * o_sel + gates[:, c + 2:c + 3] * o_win
            pieces.append(o[:, g * HEAD_DIM:(g + 1) * HEAD_DIM])
    o_ref[0] = jnp.concatenate(pieces, axis=1)


def nsa_prompt(q, gl, ckv, kvb, kvwb, tabs):
    b, t, _ = q.shape
    nq = t // TQ
    cp = ckv.shape[2]

    def win_spec(back, col):
        return pl.BlockSpec((1, TQ, LANES), lambda i, j: (i, jnp.maximum(j - back, 0), col))

    return pl.pallas_call(
        functools.partial(_nsa_prompt_body, cp=cp),
        out_shape=jax.ShapeDtypeStruct((b, t, NSA_HEADS * HEAD_DIM), F32),
        grid=(b, nq),
        in_specs=[_smem_spec(),
                  pl.BlockSpec((1, TQ, NSA_HEADS * HEAD_DIM), lambda i, j: (i, j, 0)),
                  pl.BlockSpec((1, TQ, LANES), lambda i, j: (i, j, 0)),
                  pl.BlockSpec((1, 1, cp, LANES), lambda i, j: (i, 0, 0, 0)),
                  pl.BlockSpec((1, 1, cp, LANES), lambda i, j: (i, 1, 0, 0)),
                  _const_spec(tabs["c2s"].shape),
                  pl.BlockSpec((1, t, LANES), lambda i, j: (i, 0, 2)),
                  pl.BlockSpec((1, t, LANES), lambda i, j: (i, 0, 3)),
                  win_spec(2, 0), win_spec(1, 0), win_spec(0, 0),
                  win_spec(2, 1), win_spec(1, 1), win_spec(0, 1),
                  _const_spec(tabs["selb"].shape), _const_spec(tabs["winb"].shape),
                  _const_spec(tabs["cnear"].shape), _const_spec(tabs["e3"].shape),
                  _const_spec(tabs["e3big"].shape)],
        out_specs=pl.BlockSpec((1, TQ, NSA_HEADS * HEAD_DIM), lambda i, j: (i, j, 0)),
        scratch_shapes=[pltpu.VMEM((NSA_GROUP, TQ, LANES), F32), pltpu.VMEM((NSA_GROUP, TQ, LANES), F32),
                        pltpu.VMEM((NSA_GROUP, TQ, LANES), F32)],
        compiler_params=_cparams(("parallel", "parallel")),
        name="nsa_prompt",
    )(tabs["c31"], q, gl, ckv, ckv, tabs["c2s"], kvb, kvb, kvwb, kvwb, kvwb, kvwb, kvwb, kvwb,
      tabs["selb"], tabs["winb"], tabs["cnear"], tabs["e3"], tabs["e3big"])


def _prev_diag_bias(tab):
    big = 1 << 30
    return jnp.stack([_toeplitz_bias(tab, TQ, TQ, TQ, 0, big), _toeplitz_bias(tab, TQ, TQ, 0, 0, big)], axis=1)


def nsa_prompt_tables(tab_n, t):
    nq = t // TQ
    cp = t // CMP_STRIDE + LANES
    near = TQ // CMP_STRIDE + CMP_PAD
    cnear = _toeplitz_bias(tab_n, TQ, near * CMP_STRIDE, CMP_PAD * CMP_STRIDE - (CMP_LEN - 1), 0, 1 << 30)
    return dict(
        c31=tab_n[N_BUCKETS - 1].astype(F32),
        selb=_prev_diag_bias(tab_n),
        winb=_toeplitz_bias(tab_n, TQ, 3 * TQ, 2 * TQ, 0, NSA_WINDOW - 1),
        cnear=cnear[:, :, ::CMP_STRIDE],
        c2s=jnp.asarray(_cmp_to_sel(t // CMP_STRIDE - 1, t // SEL_BLOCK, cp, LANES)),
        e3=jnp.asarray(_block_expand(nq, LANES, TQ), BF16),
        e3big=jnp.asarray(_block_expand(max(nq // FAR_TILES, 1), LANES, FAR_TILES * TQ), BF16),
    )


def _diff_prompt_body(sc_ref, q_ref, k_ref, v_ref, bias_ref, sub_ref, o_ref, m_sc, l_sc, acc_sc):
    h = pl.program_id(1)
    qi = pl.program_id(2)
    tq = q_ref.shape[1]
    q = q_ref[0] * (HEAD_DIM ** -0.5)
    lane = lax.broadcasted_iota(jnp.int32, q.shape, 1)
    q2 = jnp.concatenate([jnp.where(lane < HEAD_DIM, q, 0.0), jnp.where(lane >= HEAD_DIM, q, 0.0)], axis=0).astype(BF16)
    c31 = sc_ref[2 + h]
    _flash_init(0, m_sc, l_sc, acc_sc)

    def step(k0, size, kind, tile):
        del tile
        s = _dot_nt(q2, k_ref[0, pl.ds(k0, size), :])
        vblk = v_ref[0, pl.ds(k0, size), :]
        if kind is None:
            _flash_update(0, s, vblk, m_sc, l_sc, acc_sc, shift=c31)
        else:
            bt = bias_ref[0, kind]
            _flash_update(0, s + jnp.concatenate([bt, bt], axis=0), vblk, m_sc, l_sc, acc_sc)

    _causal_tiles(qi, step)
    o12 = acc_sc[0] / l_sc[0]
    o = o12[:tq] - sc_ref[0] * o12[tq:]
    y = (o * lax.rsqrt(jnp.mean(o * o, axis=-1, keepdims=True) + SUBLN_EPS)) * sub_ref[...]
    o_ref[0] = y * sc_ref[1]


def diff_prompt(dq, dkvb, scal, bias, subln):
    b, t, _ = dq.shape
    nq = t // TQ
    return pl.pallas_call(
        _diff_prompt_body,
        out_shape=jax.ShapeDtypeStruct((b, t, DIFF_HEADS * DIFF_VDIM), F32),
        grid=(b, DIFF_HEADS, nq),
        in_specs=[_smem_spec(),
                  pl.BlockSpec((1, TQ, LANES), lambda i, h, j: (i, j, h)),
                  pl.BlockSpec((1, t, LANES), lambda i, h, j: (i, 0, h)),
                  pl.BlockSpec((1, t, LANES), lambda i, h, j: (i, 0, DIFF_HEADS + h)),
                  pl.BlockSpec((1, 2, TQ, TQ), lambda i, h, j: (h, 0, 0, 0)),
                  _const_spec((1, DIFF_VDIM))],
        out_specs=pl.BlockSpec((1, TQ, LANES), lambda i, h, j: (i, j, h)),
        scratch_shapes=[pltpu.VMEM((1, 2 * TQ, LANES), F32), pltpu.VMEM((1, 2 * TQ, LANES), F32),
                        pltpu.VMEM((1, 2 * TQ, LANES), F32)],
        compiler_params=_cparams(("parallel", "parallel", "parallel")),
        name="diff_prompt",
    )(scal, dq, dkvb, dkvb, bias, subln.reshape(1, DIFF_VDIM))


NSA_PAGES_PER_STEP = 16
DIFF_PAGES_PER_STEP = 8
NEW_ROWS = 8


def _pad_rows(x, rows):
    return jnp.concatenate([x, jnp.zeros((rows - x.shape[0], x.shape[1]), x.dtype)], axis=0)


def _nsa_dec_body(pt_ref, *refs, n_pg, n_ch, past, cpd, n_tok):
    del pt_ref
    pages = refs[:n_pg]
    (qz_ref, glr_ref, newkv_ref, wbuf_ref, neww_ref, w1_ref, w2_ref, c2s_ref, cmpb_ref, selnear_ref,
     winb_ref, c31_ref, e3_ref, rsum_ref, rexp_ref, o_ref, p1k, p2k, p1v, p2v, sc_sc, vt_sc) = refs[n_pg:]
    c = pl.program_id(1)
    page_rows = pages[0].shape[2]
    blk_per_page = page_rows // CMP_STRIDE
    qz = qz_ref[0] * (HEAD_DIM ** -0.5)
    qzb = qz.astype(BF16)

    @pl.when(c == 0)
    def _():
        for ref in (p1k, p2k, p1v, p2v):
            ref[...] = jnp.zeros(ref.shape, F32)

    for i in range(n_pg):
        page = pages[i][0]
        pg = c * n_pg + i
        blk0 = pl.multiple_of(CMP_PAD + pg * blk_per_page, SUBLANES)
        xk = page[0:LANES, :].T.reshape(blk_per_page, CMP_STRIDE, LANES)
        xv = page[LANES:2 * LANES, :].T.reshape(blk_per_page, CMP_STRIDE, LANES)
        p1k[pl.ds(blk0, blk_per_page), :] = (xk * w1_ref[0][None]).sum(axis=1)
        p2k[pl.ds(blk0, blk_per_page), :] = (xk * w2_ref[0][None]).sum(axis=1)
        p1v[pl.ds(blk0, blk_per_page), :] = (xv * w1_ref[1][None]).sum(axis=1)
        p2v[pl.ds(blk0, blk_per_page), :] = (xv * w2_ref[1][None]).sum(axis=1)
        sc_sc[c, :, i * page_rows:(i + 1) * page_rows] = _dot(qzb, page[2 * LANES:3 * LANES, :].astype(BF16))
        vt_sc[c, :, i * page_rows:(i + 1) * page_rows] = page[3 * LANES:4 * LANES, :].astype(BF16)

    @pl.when(c == n_ch - 1)
    def _():
        nk = newkv_ref[0]
        s_new = _dot_nt(qzb, _pad_rows(nk[:, 2 * LANES:3 * LANES], LANES).astype(BF16))
        v_new = _pad_rows(nk[:, 3 * LANES:4 * LANES], LANES).astype(BF16)

        ck = p1k[...] + pltpu.roll(p2k[...], cpd - 1, 0)
        cv = p1v[...] + pltpu.roll(p2v[...], cpd - 1, 0)
        cb = cmpb_ref[...]
        (p_cmp,) = _masked_softmax_parts([(_dot_nt(qz, ck, HI) + cb, cb > 0.5 * NEG)])
        o_cmp = _dot(p_cmp, cv, HI)
        imp = _dot(_dot(rsum_ref[...], p_cmp, HI), c2s_ref[...], HI)
        rows = lax.broadcasted_iota(jnp.int32, (imp.shape[0], 1), 0)
        sel = _select_blocks(_block_scores(imp, past + rows % n_tok), N_SELECT)
        sel = _dot(rexp_ref[...], sel).astype(BF16)

        c31 = c31_ref[...][:, :1]
        selnear = selnear_ref[...]
        chunk = n_pg * page_rows
        bpc = chunk // SEL_BLOCK
        expand = e3_ref[...]
        parts = []
        for ci in range(n_ch):
            s = sc_sc[ci] + (_dot(sel[:, ci * bpc:(ci + 1) * bpc], expand) - 1.0) * 1e30
            if ci < n_ch - 1:
                s = s + c31
            else:
                s = s + jnp.concatenate([jnp.broadcast_to(c31, (s.shape[0], chunk - LANES)), selnear[:, :LANES]], axis=1)
            parts.append(s)
        parts.append(s_new + (_dot(sel[:, n_ch * bpc:(n_ch + 1) * bpc], expand[:, :LANES]) - 1.0) * 1e30 + selnear[:, LANES:])
        m = parts[0].max(axis=-1, keepdims=True)
        for s in parts[1:]:
            m = jnp.maximum(m, s.max(axis=-1, keepdims=True))
        den = jnp.zeros_like(m)
        acc = jnp.zeros((m.shape[0], LANES), F32)
        for ci, s in enumerate(parts):
            p = jnp.exp(s - m)
            den = den + p.sum(axis=-1, keepdims=True)
            if ci < n_ch:
                acc = acc + _dot_nt(p.astype(BF16), vt_sc[ci])
            else:
                acc = acc + _dot(p.astype(BF16), v_new)
        o_sel = acc / den

        wb = wbuf_ref[0]
        nw = neww_ref[0]
        kcat = jnp.concatenate([wb[:, :LANES], _pad_rows(nw[:, :LANES], LANES)], axis=0).astype(BF16)
        vcat = jnp.concatenate([wb[:, LANES:], _pad_rows(nw[:, LANES:], LANES)], axis=0).astype(BF16)
        wbias = winb_ref[...]
        (p_win,) = _masked_softmax_parts([(_dot_nt(qzb, kcat) + wbias, wbias > 0.5 * NEG)])
        o_win = _dot(p_win.astype(BF16), vcat)
        gates = jax.nn.sigmoid(glr_ref[0])
        o_ref[0] = gates[:, 0:1] * o_cmp + gates[:, 1:2] * o_sel + gates[:, 2:3] * o_win


def _row_tables(table, head, tok):
    return table[head, tok]


def nsa_decode_tables(tab_n, past, n_tok, wlen, n_ch, n_pg, page_rows):
    rows = NSA_KV_HEADS * n_tok * NSA_GROUP
    g = np.arange(rows) // (n_tok * NSA_GROUP)
    t = (np.arange(rows) // NSA_GROUP) % n_tok
    head = g * NSA_GROUP + np.arange(rows) % NSA_GROUP
    cpd = past // CMP_STRIDE + LANES
    n_sel_pad = 2 * LANES
    big = 1 << 30
    cmpb = _toeplitz_bias(tab_n, n_tok, cpd * CMP_STRIDE, past + CMP_PAD * CMP_STRIDE - (CMP_LEN - 1), 0, big)
    cmpb = _row_tables(cmpb[:, :, ::CMP_STRIDE], head, t)
    valid_cp = (np.arange(cpd) >= CMP_PAD)[None, :]
    cmpb = jnp.where(valid_cp, cmpb, NEG)
    selnear = _row_tables(_toeplitz_bias(tab_n, n_tok, 2 * LANES, LANES, 0, big), head, t)
    winb = _row_tables(_toeplitz_bias(tab_n, n_tok, wlen + LANES, wlen, 0, NSA_WINDOW - 1), head, t)
    c31 = jnp.broadcast_to(tab_n[N_BUCKETS - 1, head].astype(F32)[:, None], (rows, LANES))
    n_cmp = (past + SEL_BLOCK) // CMP_STRIDE - 1
    n_sel = (past + SEL_BLOCK) // SEL_BLOCK
    rsum = (np.arange(rows)[None, :] // NSA_GROUP == np.arange(rows // NSA_GROUP)[:, None]).astype(np.float32)
    return dict(cmpb=cmpb, selnear=selnear, winb=winb, c31=c31,
                c2s=jnp.asarray(_cmp_to_sel(min(n_cmp, cpd - CMP_PAD), n_sel, cpd, n_sel_pad)),
                e3=jnp.asarray(_block_expand(1, n_pg * page_rows // SEL_BLOCK, n_pg * page_rows)[0], BF16),
                rsum=jnp.asarray(rsum), rexp=jnp.asarray(rsum.T))


def nsa_decode(page_table, pool_t, qz, glr, newkv, wbuf, neww, w1, w2, tabs, n_tok):
    s, n_pages = page_table.shape
    page_rows = pool_t.shape[2]
    n_pg = min(NSA_PAGES_PER_STEP, n_pages)
    n_ch = n_pages // n_pg
    past = n_pages * page_rows
    cpd = tabs["cmpb"].shape[1]
    rows = qz.shape[1]

    def page_spec(i):
        return pl.BlockSpec((1, pool_t.shape[1], page_rows), lambda b, c, pt: (pt[b, c * n_pg + i], 0, 0))

    def seq_spec(a):
        return pl.BlockSpec((1,) + a.shape[1:], lambda b, c, pt: (b,) + (0,) * (a.ndim - 1))

    def const(a):
        n = a.ndim
        return pl.BlockSpec(a.shape, lambda b, c, pt: (0,) * n)

    consts = [w1, w2, tabs["c2s"], tabs["cmpb"], tabs["selnear"], tabs["winb"], tabs["c31"], tabs["e3"],
              tabs["rsum"], tabs["rexp"]]
    seqs = [qz, glr, newkv, wbuf, neww]
    grid_spec = pltpu.PrefetchScalarGridSpec(
        num_scalar_prefetch=1,
        grid=(s, n_ch),
        in_specs=[page_spec(i) for i in range(n_pg)] + [seq_spec(a) for a in seqs] + [const(a) for a in consts],
        out_specs=pl.BlockSpec((1, rows, LANES), lambda b, c, pt: (b, 0, 0)),
        scratch_shapes=[pltpu.VMEM((cpd, LANES), F32)] * 4
        + [pltpu.VMEM((n_ch, rows, n_pg * page_rows), F32), pltpu.VMEM((n_ch, LANES, n_pg * page_rows), BF16)],
    )
    return pl.pallas_call(
        functools.partial(_nsa_dec_body, n_pg=n_pg, n_ch=n_ch, past=past, cpd=cpd, n_tok=n_tok),
        out_shape=jax.ShapeDtypeStruct((s, rows, LANES), F32),
        grid_spec=grid_spec,
        compiler_params=_cparams(("parallel", "arbitrary")),
        name="nsa_decode",
    )(page_table, *([pool_t] * n_pg), *seqs, *consts)


def _diff_dec_body(pt_ref, *refs, n_pg, n_ch, past, n_tok, page_rows):
    del pt_ref
    pages = refs[:n_pg]
    sc_ref, qd_ref, newd_ref, near_ref, sub_ref, o_ref, s_sc, v_sc = refs[n_pg:]
    c = pl.program_id(1)
    per_pos = 2 * DIFF_HEADS
    qd = (qd_ref[0] * (HEAD_DIM ** -0.5)).astype(BF16)
    for i in range(n_pg):
        pg = c * n_pg + i
        for h in range(DIFF_HEADS):
            kh = pages[i][0, pl.ds(h, page_rows, stride=per_pos), :]
            vh = pages[i][0, pl.ds(DIFF_HEADS + h, page_rows, stride=per_pos), :]
            s_sc[c, h, :, i * page_rows:(i + 1) * page_rows] = _dot_nt(qd[h], kh.astype(BF16))
            v_sc[h, pl.ds(pl.multiple_of(pg * page_rows, page_rows), page_rows), :] = vh.astype(BF16)

    @pl.when(c == n_ch - 1)
    def _():
        nd = newd_ref[0]
        chunk = n_pg * page_rows
        for h in range(DIFF_HEADS):
            near = near_ref[h]
            c31 = sc_ref[2 + h]
            s_new = _dot_nt(qd[h], _pad_rows(nd[:, h * LANES:(h + 1) * LANES], LANES).astype(BF16)) + near[:, LANES:]
            v_sc[h, pl.ds(past, LANES), :] = _pad_rows(nd[:, (DIFF_HEADS + h) * LANES:(DIFF_HEADS + h + 1) * LANES], LANES).astype(BF16)
            parts = []
            for ci in range(n_ch):
                s = s_sc[ci, h]
                if ci < n_ch - 1:
                    s = s + c31
                else:
                    s = s + jnp.concatenate([jnp.full((s.shape[0], chunk - LANES), c31, F32), near[:, :LANES]], axis=1)
                parts.append(s)
            parts.append(s_new)
            m = parts[0].max(axis=-1, keepdims=True)
            for s in parts[1:]:
                m = jnp.maximum(m, s.max(axis=-1, keepdims=True))
            den = jnp.zeros_like(m)
            acc = jnp.zeros((m.shape[0], LANES), F32)
            for ci, s in enumerate(parts):
                p = jnp.exp(s - m)
                den = den + p.sum(axis=-1, keepdims=True)
                acc = acc + _dot(p.astype(BF16), v_sc[h, ci * chunk:ci * chunk + s.shape[1], :])
            o12 = acc / den
            o = o12[:n_tok] - sc_ref[0] * o12[n_tok:]
            y = (o * lax.rsqrt(jnp.mean(o * o, axis=-1, keepdims=True) + SUBLN_EPS)) * sub_ref[...]
            o_ref[0, h] = y * sc_ref[1]


def diff_decode_tables(tab_d, n_tok):
    near = _toeplitz_bias(tab_d, n_tok, 2 * LANES, LANES, 0, 1 << 30)
    return jnp.concatenate([near, near], axis=1)


def diff_decode(page_table, pool_rows, qd, newd, scal, near, subln, n_tok, page_rows):
    s, n_pages = page_table.shape
    n_pg = min(DIFF_PAGES_PER_STEP, n_pages)
    n_ch = n_pages // n_pg
    past = n_pages * page_rows

    def page_spec(i):
        return pl.BlockSpec((1,) + pool_rows.shape[1:], lambda b, c, pt: (pt[b, c * n_pg + i], 0, 0))

    grid_spec = pltpu.PrefetchScalarGridSpec(
        num_scalar_prefetch=1,
        grid=(s, n_ch),
        in_specs=[page_spec(i) for i in range(n_pg)] + [
            pl.BlockSpec(memory_space=pltpu.SMEM),
            pl.BlockSpec((1,) + qd.shape[1:], lambda b, c, pt: (b, 0, 0, 0)),
            pl.BlockSpec((1,) + newd.shape[1:], lambda b, c, pt: (b, 0, 0)),
            pl.BlockSpec(near.shape, lambda b, c, pt: (0, 0, 0)),
            pl.BlockSpec((1, DIFF_VDIM), lambda b, c, pt: (0, 0))],
        out_specs=pl.BlockSpec((1, DIFF_HEADS, n_tok, LANES), lambda b, c, pt: (b, 0, 0, 0)),
        scratch_shapes=[pltpu.VMEM((n_ch, DIFF_HEADS, 2 * n_tok, n_pg * page_rows), F32),
                        pltpu.VMEM((DIFF_HEADS, past + LANES, LANES), BF16)],
    )
    return pl.pallas_call(
        functools.partial(_diff_dec_body, n_pg=n_pg, n_ch=n_ch, past=past, n_tok=n_tok, page_rows=page_rows),
        out_shape=jax.ShapeDtypeStruct((s, DIFF_HEADS, n_tok, LANES), F32),
        grid_spec=grid_spec,
        compiler_params=_cparams(("parallel", "arbitrary")),
        name="diff_decode",
    )(page_table, *([pool_rows] * n_pg), scal, qd, newd, near, subln.reshape(1, DIFF_VDIM))


def _sink_attention(qzb, kcat, vcat, bias, ok, sink):
    s = _dot_nt(qzb, kcat) + bias
    msk = (bias > 0.5 * NEG) & ok
    sm = jnp.where(msk, s, NEG)
    m = jnp.maximum(jnp.max(sm, axis=-1, keepdims=True), sink)
    p = jnp.where(msk, jnp.exp(sm - m), 0.0)
    den = jnp.sum(p, axis=-1, keepdims=True) + jnp.exp(sink - m)
    p = p / jnp.maximum(den, 1e-30)
    return _dot(p.astype(BF16), vcat)


def _swa_prompt_body(sink_ref, q_ref, kp_ref, kc_ref, vp_ref, vc_ref, bias_ref, o_ref):
    qi = pl.program_id(1)
    tq = q_ref.shape[1]
    q = q_ref[0] * (HEAD_DIM ** -0.5)
    kcat = jnp.concatenate([kp_ref[0], kc_ref[0]], axis=0)
    vcat = jnp.concatenate([vp_ref[0], vc_ref[0]], axis=0)
    col = lax.broadcasted_iota(jnp.int32, (tq, kcat.shape[0]), 1)
    ok = (col >= SWA_WINDOW) | (qi > 0)
    pieces = []
    for g in range(SWA_KV_HEADS):
        for h in range(SWA_GROUP):
            hh = g * SWA_GROUP + h
            qzb = _place_half(q[:, hh * HEAD_DIM:(hh + 1) * HEAD_DIM], g).astype(BF16)
            o = _sink_attention(qzb, kcat, vcat, bias_ref[hh], ok, sink_ref[hh])
            pieces.append(o[:, g * HEAD_DIM:(g + 1) * HEAD_DIM])
    o_ref[0] = jnp.concatenate(pieces, axis=1)


def swa_prompt(q, kvb, sinks, tab_s):
    b, t, _ = q.shape
    bias = _toeplitz_bias(tab_s, TQ, SWA_WINDOW + TQ, SWA_WINDOW, 0, SWA_WINDOW - 1)
    per = TQ // SWA_WINDOW

    def prev_spec(col):
        return pl.BlockSpec((1, SWA_WINDOW, LANES), lambda i, j: (i, jnp.maximum(per * j - 1, 0), col))

    def cur_spec(col):
        return pl.BlockSpec((1, TQ, LANES), lambda i, j: (i, j, col))

    return pl.pallas_call(
        _swa_prompt_body,
        out_shape=jax.ShapeDtypeStruct((b, t, SWA_HEADS * HEAD_DIM), F32),
        grid=(b, t // TQ),
        in_specs=[_smem_spec(), pl.BlockSpec((1, TQ, SWA_HEADS * HEAD_DIM), lambda i, j: (i, j, 0)),
                  prev_spec(0), cur_spec(0), prev_spec(1), cur_spec(1), _const_spec(bias.shape)],
        out_specs=pl.BlockSpec((1, TQ, SWA_HEADS * HEAD_DIM), lambda i, j: (i, j, 0)),
        compiler_params=_cparams(("parallel", "parallel")),
        name="swa_prompt",
    )(sinks.astype(F32), q, kvb, kvb, kvb, kvb, bias)


def _swa_sample_body(qz_ref, buf_ref, new_ref, bias_ref, sink_ref, o_ref):
    qzb = (qz_ref[0] * (HEAD_DIM ** -0.5)).astype(BF16)
    buf = buf_ref[0]
    new = new_ref[0]
    kcat = jnp.concatenate([buf[:, :LANES], _pad_rows(new[:, :LANES], LANES)], axis=0).astype(BF16)
    vcat = jnp.concatenate([buf[:, LANES:], _pad_rows(new[:, LANES:], LANES)], axis=0).astype(BF16)
    o_ref[0] = _sink_attention(qzb, kcat, vcat, bias_ref[...], True, sink_ref[...][:, :1])


def swa_sample(qz, buf, new, sinks, tab_s, n_tok):
    s, rows, _ = qz.shape
    wlen = buf.shape[1]
    g = np.arange(rows) // (n_tok * SWA_GROUP)
    t = (np.arange(rows) // SWA_GROUP) % n_tok
    head = g * SWA_GROUP + np.arange(rows) % SWA_GROUP
    bias = _row_tables(_toeplitz_bias(tab_s, n_tok, wlen + LANES, wlen, 0, SWA_WINDOW - 1), head, t)
    sink_rows = jnp.broadcast_to(sinks.astype(F32)[head][:, None], (rows, LANES))
    return pl.pallas_call(
        _swa_sample_body,
        out_shape=jax.ShapeDtypeStruct((s, rows, LANES), F32),
        grid=(s,),
        in_specs=[pl.BlockSpec((1, rows, LANES), lambda i: (i, 0, 0)),
                  pl.BlockSpec((1, wlen, 2 * LANES), lambda i: (i, 0, 0)),
                  pl.BlockSpec((1, NEW_ROWS, 2 * LANES), lambda i: (i, 0, 0)),
                  _const_spec(bias.shape), _const_spec(sink_rows.shape)],
        out_specs=pl.BlockSpec((1, rows, LANES), lambda i: (i, 0, 0)),
        compiler_params=_cparams(("parallel",)),
        name="swa_sample",
    )(qz, buf, new, bias, sink_rows)


RWKV_CHUNK = 64
RWKV_SEQS_PER_STEP = 4


def _head_sum(x, hsum_ref):
    return _dot2(_split(x), hsum_ref[...])


def _rwkv_prep_body(ur_ref, uk_ref, uv_ref, ul_ref, pr_ref, pk_ref, pv_ref, pl_ref,
                    mur_ref, muk_ref, muv_ref, mul_ref, w0_ref, w2_ref, a0_ref, a2_ref, g2_ref,
                    kk_ref, ka_ref, hsum_ref, r_o, lw_o, k_o, v_o, na_o, b_o, g_o):
    def mix(u_ref, p_ref, mu_ref):
        u = u_ref[...]
        return u + (p_ref[...] - u) * mu_ref[...]

    r = mix(ur_ref, pr_ref, mur_ref)
    k = mix(uk_ref, pk_ref, muk_ref)
    v = mix(uv_ref, pv_ref, muv_ref)
    lo = mix(ul_ref, pl_ref, mul_ref)
    z = -(w0_ref[...] + _dot3(_split(jnp.tanh(lo)), _split(w2_ref[...])))
    softplus = jnp.maximum(z, 0.0) + jnp.log(1.0 + jnp.exp(-jnp.abs(z)))
    wlog = -softplus - 0.5
    a = jax.nn.sigmoid(a0_ref[...] + _dot3(_split(lo), _split(a2_ref[...])))
    g = _dot3(_split(jax.nn.sigmoid(lo)), _split(g2_ref[...]))
    kk = k * kk_ref[...]
    kk = kk / jnp.maximum(jnp.sqrt(_head_sum(kk * kk, hsum_ref)), 1e-12)
    r_o[...] = r
    lw_o[...] = -jnp.exp(wlog)
    k_o[...] = k * (1.0 + (a - 1.0) * ka_ref[...])
    v_o[...] = v
    na_o[...] = -kk
    b_o[...] = kk * a
    g_o[...] = g


def rwkv_prep(us, prevs, mus, w0, w2p, a0, a2p, g2p, k_k, k_a, hsum):
    m = us[0].shape[0]
    tm = min(ROW_TILE, m)
    row = lambda a: pl.BlockSpec((tm, a.shape[1]), lambda i: (i, 0))
    vec = lambda a: a.reshape(1, -1)
    consts = [vec(x) for x in mus] + [vec(w0), w2p, vec(a0), a2p, g2p, vec(k_k), vec(k_a), hsum]
    return pl.pallas_call(
        _rwkv_prep_body,
        out_shape=[jax.ShapeDtypeStruct((m, RWKV_WIDTH), F32)] * 7,
        grid=(m // tm,),
        in_specs=[row(a) for a in us] + [row(a) for a in prevs] + [_const_spec(c.shape) for c in consts],
        out_specs=[pl.BlockSpec((tm, RWKV_WIDTH), lambda i: (i, 0))] * 7,
        compiler_params=_cparams(("parallel",)),
        name="rwkv_prep",
    )(*us, *prevs, *consts)


BNN = (((2,), (1,)), ((0,), (0,)))
BNT = (((2,), (2,)), ((0,), (0,)))
BTN = (((1,), (1,)), ((0,), (0,)))


def _rwkv_chunk_body(r_ref, lw_ref, k_ref, v_ref, a_ref, b_ref, s0_ref, y_ref, sT_ref, st_sc):
    ci = pl.program_id(1)
    nb, c, _ = r_ref.shape
    n = RWKV_N

    def to_batch(ref):
        x = ref[...]
        return jnp.concatenate([x[:, :, h * n:(h + 1) * n] for h in range(RWKV_HEADS)], axis=0)

    @pl.when(ci == 0)
    def _():
        s0 = s0_ref[...]
        st_sc[...] = jnp.concatenate([s0[:, h] for h in range(RWKV_HEADS)], axis=0)

    row = lax.broadcasted_iota(jnp.int32, (c, c), 0)
    col = lax.broadcasted_iota(jnp.int32, (c, c), 1)
    incl = (row >= col)[None]
    strict = (row > col)[None]
    eye_c = (row == col).astype(F32)[None]
    eye_n = (lax.broadcasted_iota(jnp.int32, (n, n), 0) == lax.broadcasted_iota(jnp.int32, (n, n), 1)).astype(F32)[None]
    r, lw, k, v, a, b = (to_batch(ref) for ref in (r_ref, lw_ref, k_ref, v_ref, a_ref, b_ref))
    nbat = r.shape[0]
    ones_incl = jnp.broadcast_to(incl.astype(BF16), (nbat, c, c))
    lw_hi, lw_lo = _split(lw)
    lw_lo2 = (lw - lw_hi.astype(F32) - lw_lo.astype(F32)).astype(BF16)
    csum = lambda y: lax.dot_general(ones_incl, y, BNN, preferred_element_type=F32)
    cs = csum(lw_hi) + (csum(lw_lo) + csum(lw_lo2))
    gam = jnp.exp(cs)
    ginv = jnp.exp(-cs)
    to_end = jnp.exp(cs[:, c - 1:c, :] - cs)
    at = _split(a * jnp.exp(cs - lw))
    rt_f = r * gam
    rt = _split(rt_f)
    bt = _split(b * ginv)
    kt = _split(k * ginv)
    vs = _split(v)
    lb = jnp.where(strict, _dot3(at, bt, BNT), 0.0)
    lk = jnp.where(strict, _dot3(at, kt, BNT), 0.0)
    pb = _split(jnp.where(incl, _dot3(rt, bt, BNT), 0.0))
    pk = _split(jnp.where(incl, _dot3(rt, kt, BNT), 0.0))
    tinv = eye_c + lb
    lp = _split(lb)
    covered = 2
    while covered < c:
        lp = _split(_dot3(lp, lp, BNN))
        tinv = tinv + _dot3(_split(tinv), lp, BNN)
        covered *= 2
    tinv = _split(tinv)
    w = _split(_dot3(tinv, at, BNN))
    uv = _split(_dot3(tinv, _split(_dot3(_split(lk), vs, BNN)), BNN))
    q = rt_f + _dot3(pb, w, BNN)
    bh = _split(b * to_end)
    kh = _split(k * to_end)
    tm = eye_n * gam[:, c - 1:c, :] + _dot3(bh, w, BTN)
    bm = _dot3(bh, uv, BTN) + _dot3(kh, vs, BTN)
    x = _dot3(_split(jnp.concatenate([q, tm], axis=1)), _split(st_sc[...]), BNN)
    y = x[:, :c] + _dot3(pb, uv, BNN) + _dot3(pk, vs, BNN)
    st_sc[...] = x[:, c:] + bm
    y_ref[...] = jnp.concatenate([y[h * nb:(h + 1) * nb] for h in range(RWKV_HEADS)], axis=2)

    @pl.when(ci == pl.num_programs(1) - 1)
    def _():
        st = st_sc[...]
        for h in range(RWKV_HEADS):
            sT_ref[:, h] = st[h * nb:(h + 1) * nb]


def rwkv_chunk(seqs, s0t, chunk, nb):
    b, t, width = seqs[0].shape
    h, n = s0t.shape[1:3]
    seq_spec = pl.BlockSpec((nb, chunk, width), lambda i, j: (i, j, 0))
    st_spec = pl.BlockSpec((nb, h, n, n), lambda i, j: (i, 0, 0, 0))
    return pl.pallas_call(
        _rwkv_chunk_body,
        out_shape=[jax.ShapeDtypeStruct((b, t, width), F32), jax.ShapeDtypeStruct((b, h, n, n), F32)],
        grid=(b // nb, t // chunk),
        in_specs=[seq_spec] * 6 + [st_spec],
        out_specs=[seq_spec, st_spec],
        scratch_shapes=[pltpu.VMEM((h * nb, n, n), F32)],
        compiler_params=_cparams(("parallel", "arbitrary")),
        name="rwkv_chunk",
    )(*seqs, s0t)


def _rwkv_post_body(y_ref, r_ref, k_ref, v_ref, g_ref, rk_ref, lnw_ref, lnb_ref, hsum_ref, o_ref):
    y = y_ref[...]
    inv_n = 1.0 / RWKV_N
    mean = _head_sum(y, hsum_ref) * inv_n
    yc = y - mean
    var = _head_sum(yc * yc, hsum_ref) * inv_n
    yn = yc * lax.rsqrt(var + RWKV_GN_EPS) * lnw_ref[...] + lnb_ref[...]
    rk = _head_sum(r_ref[...] * k_ref[...] * rk_ref[...], hsum_ref)
    o_ref[...] = (yn + rk * v_ref[...]) * g_ref[...]


def rwkv_post(y, r, k, v, g, r_k, ln_w, ln_b, hsum):
    m = y.shape[0]
    tm = min(ROW_TILE, m)
    row = pl.BlockSpec((tm, RWKV_WIDTH), lambda i: (i, 0))
    vec = lambda a: a.reshape(1, -1)
    return pl.pallas_call(
        _rwkv_post_body,
        out_shape=jax.ShapeDtypeStruct((m, RWKV_WIDTH), F32),
        grid=(m // tm,),
        in_specs=[row] * 5 + [_const_spec((1, RWKV_WIDTH))] * 3 + [_const_spec(hsum.shape)],
        out_specs=row,
        compiler_params=_cparams(("parallel",)),
        name="rwkv_post",
    )(y, r, k, v, g, vec(r_k), vec(ln_w), vec(ln_b), hsum)


RWKV_SIZES = (RWKV_WIDTH, RWKV_WIDTH, RWKV_WIDTH, DECAY_LORA, AAA_LORA, GATE_LORA)
LORA_WIDTH = DECAY_LORA + AAA_LORA + GATE_LORA
ODD_SIZES = (SWA_HEADS * HEAD_DIM, 2 * SWA_KV_HEADS * HEAD_DIM, 3 * RWKV_WIDTH + LORA_WIDTH)
ODD_GROUPS = ((0, 512, (F32,)), (512, 256, (F32, BF16)), (768, 512, (F32,)), (1280, 512, (F32,)),
              (1792, 512, (F32,)), (2304, LORA_PAD, (F32,)))


def _split_cols(w, sizes):
    offs = np.cumsum([0] + list(sizes))
    return [w[..., int(offs[i]):int(offs[i + 1])] for i in range(len(sizes))]


def odd_weights(w_in):
    return jnp.pad(w_in, ((0, 0), (0, LORA_PAD - LORA_WIDTH))).astype(BF16)


def _lora_rows(w, first):
    return jnp.pad(w.astype(F32), ((first, LORA_PAD - first - w.shape[0]), (0, 0)))


def odd_mixer(x, g, w_odd, swa_buf, wkv0, shift0, prompt, sinks, tab, rw, hsum):
    b, t, d = x.shape
    mu, w0, w2, a0, a2, g2, k_k, k_a, r_k, ln_w, ln_b = rw
    q, kv, kvb, ur, uk, uv, ul = norm_proj(x.reshape(b * t, d), g, w_odd, ODD_GROUPS)
    r3 = lambda a: a.reshape(b, t, a.shape[-1])
    tab_s = tab[:, :SWA_HEADS]
    if prompt:
        o_swa = swa_prompt(r3(q), r3(kvb), sinks, tab_s).reshape(b * t, -1)
        ctx = r3(kv)[:, t - min(SWA_WINDOW, t):]
        shift0 = jnp.zeros((b, 3 * RWKV_WIDTH + LORA_WIDTH), F32)
        wkv0 = jnp.zeros((b, RWKV_HEADS, RWKV_N, RWKV_N), F32)
        chunk = RWKV_CHUNK
        t_pad = t
        nb = 1
    else:
        rows = SWA_KV_HEADS * t * SWA_GROUP
        q5 = q.reshape(b, t, SWA_KV_HEADS, SWA_GROUP, HEAD_DIM).transpose(0, 2, 1, 3, 4)
        z = jnp.zeros_like(q5[:, 0])
        qz = jnp.stack([jnp.concatenate([q5[:, 0], z], -1), jnp.concatenate([z, q5[:, 1]], -1)], axis=1).reshape(b, rows, LANES)
        new = jnp.pad(r3(kv), ((0, 0), (0, NEW_ROWS - t), (0, 0)))
        o_rows = swa_sample(qz, swa_buf, new, sinks, tab_s, t)
        o6 = o_rows.reshape(b, SWA_KV_HEADS, t, SWA_GROUP, 2, HEAD_DIM)
        o_swa = jnp.stack([o6[:, 0, :, :, 0], o6[:, 1, :, :, 1]], axis=2).reshape(b * t, SWA_HEADS * HEAD_DIM)
        ctx = jnp.concatenate([swa_buf, r3(kv)], axis=1)[:, t:]
        chunk = NEW_ROWS
        t_pad = NEW_ROWS
        nb = math.gcd(b, RWKV_SEQS_PER_STEP)

    s_r, s_k, s_v, s_l = _split_cols(shift0.astype(F32), (RWKV_WIDTH,) * 3 + (LORA_WIDTH,))
    s_l = jnp.pad(s_l, ((0, 0), (0, LORA_PAD - LORA_WIDTH)))
    us = [ur, uk, uv, ul]
    prevs = [jnp.concatenate([s[:, None], r3(u)[:, :-1]], axis=1).reshape(b * t, -1) for s, u in zip((s_r, s_k, s_v, s_l), us)]
    mu_r, mu_k, mu_v, mu_l = _split_cols(mu.astype(F32), (RWKV_WIDTH,) * 3 + (LORA_WIDTH,))
    mu_l = jnp.pad(mu_l, (0, LORA_PAD - LORA_WIDTH))
    r, lw, k, v, na, bb, gate = rwkv_prep(us, prevs, [mu_r, mu_k, mu_v, mu_l], w0, _lora_rows(w2, 0), a0,
                                          _lora_rows(a2, DECAY_LORA), _lora_rows(g2, DECAY_LORA + AAA_LORA), k_k, k_a, hsum)

    def seq(a):
        return jnp.pad(a.reshape(b, t, RWKV_WIDTH), ((0, 0), (0, t_pad - t), (0, 0)))

    y, st = rwkv_chunk([seq(a) for a in (r, lw, k, v, na, bb)], jnp.swapaxes(wkv0.astype(F32), -1, -2), chunk, nb)
    y = y[:, :t].reshape(b * t, RWKV_WIDTH)
    o_rwkv = rwkv_post(y, r, k, v, gate, r_k, ln_w, ln_b, hsum)
    shift = jnp.concatenate([r3(ur)[:, -1], r3(uk)[:, -1], r3(uv)[:, -1], r3(ul)[:, -1, :LORA_WIDTH]], axis=-1)
    return o_swa, o_rwkv, ctx, jnp.swapaxes(st, -1, -2), shift


EVEN_SIZES = (NSA_HEADS * HEAD_DIM, 4 * NSA_KV_HEADS * HEAD_DIM, 2 * NSA_KV_HEADS * HEAD_DIM, 3 * NSA_HEADS,
              2 * DIFF_HEADS * HEAD_DIM, 2 * DIFF_HEADS * DIFF_VDIM)
EVEN_GROUPS = ((0, 512, (F32,)), (512, 512, (F32, BF16)), (1024, 256, (F32, BF16)), (1280, 512, (F32,)),
               (1792, 1024, (F32, BF16)), (2816, 128, (F32,)))


def even_weights(w_in):
    wq, wkv, wkvw, wgl, wdq, wdkv = _split_cols(w_in, EVEN_SIZES)
    wgl = jnp.pad(wgl, ((0, 0), (0, LANES - wgl.shape[1])))
    return jnp.concatenate([wq, wkv, wkvw, wdq, wdkv, wgl], axis=1).astype(BF16)


def cmp_lane_weights(cmp_w):
    wt = jax.nn.softmax(cmp_w.astype(F32), axis=-1)
    wl = jnp.repeat(jnp.swapaxes(wt, 1, 2), HEAD_DIM, axis=2)
    return wl[:, :CMP_STRIDE], wl[:, CMP_STRIDE:]


def diff_scalars(lq, layer, tab_d):
    lam_init = 0.8 - 0.6 * math.exp(-0.3 * layer)
    lq = lq.astype(F32)
    lam = jnp.exp(jnp.sum(lq[0] * lq[1])) - jnp.exp(jnp.sum(lq[2] * lq[3])) + lam_init
    return jnp.concatenate([jnp.stack([lam, jnp.asarray(1.0 - lam_init, F32)]), tab_d[N_BUCKETS - 1].astype(F32)])


def even_prompt(x, g, w_even, w1, w2, tab, scal, subln):
    b, t, d = x.shape
    q, kv, kvb, kvw, kvwb, dq, dkv, dkvb, gl = norm_proj(x.reshape(b * t, d), g, w_even, EVEN_GROUPS)
    r3 = lambda a: a.reshape(b, t, a.shape[-1])
    tab_n, tab_d = tab[:, :NSA_HEADS], tab[:, NSA_HEADS:NSA_HEADS + DIFF_HEADS]
    tabs = nsa_prompt_tables(tab_n, t)
    ckv = compress_prompt(r3(kv), w1, w2, tabs["c2s"].shape[0])
    o_nsa = nsa_prompt(r3(q), r3(gl), ckv, r3(kvb), r3(kvwb), tabs)
    o_diff = diff_prompt(r3(dq), r3(dkvb), scal, _prev_diag_bias(tab_d), subln)
    return o_nsa.reshape(b * t, -1), o_diff.reshape(b * t, -1), kv, kvw, dkv


def even_sample(x, g, w_even, nsa_pool, diff_pool, win_buf, page_table, w1, w2, tab, scal, subln):
    s, n_tok, d = x.shape
    q, kv, _, kvw, _, dq, dkv, _, gl = norm_proj(x.reshape(s * n_tok, d), g, w_even, EVEN_GROUPS)
    tab_n, tab_d = tab[:, :NSA_HEADS], tab[:, NSA_HEADS:NSA_HEADS + DIFF_HEADS]
    n_pages = page_table.shape[1]
    n_pool, page_rows = nsa_pool.shape[:2]
    past = n_pages * page_rows
    wlen = win_buf.shape[1]
    n_pg = min(NSA_PAGES_PER_STEP, n_pages)
    rows = NSA_KV_HEADS * n_tok * NSA_GROUP

    q5 = q.reshape(s, n_tok, NSA_KV_HEADS, NSA_GROUP, HEAD_DIM).transpose(0, 2, 1, 3, 4)
    z = jnp.zeros_like(q5[:, 0])
    qz = jnp.stack([jnp.concatenate([q5[:, 0], z], -1), jnp.concatenate([z, q5[:, 1]], -1)], axis=1).reshape(s, rows, LANES)
    glr = gl[:, :3 * NSA_HEADS].reshape(s, n_tok, NSA_KV_HEADS, NSA_GROUP, 3).transpose(0, 2, 1, 3, 4).reshape(s, rows, 3)
    glr = jnp.pad(glr, ((0, 0), (0, 0), (0, LANES - 3)))
    pad_tok = lambda a: jnp.pad(a.reshape(s, n_tok, a.shape[-1]), ((0, 0), (0, NEW_ROWS - n_tok), (0, 0)))
    tabs = nsa_decode_tables(tab_n, past, n_tok, wlen, n_pages // n_pg, n_pg, page_rows)
    pool_t = jnp.transpose(nsa_pool, (0, 2, 3, 4, 1)).reshape(n_pool, -1, page_rows)
    o_rows = nsa_decode(page_table, pool_t, qz, glr, pad_tok(kv), win_buf.reshape(s, wlen, -1), pad_tok(kvw),
                        w1, w2, tabs, n_tok)
    o6 = o_rows.reshape(s, NSA_KV_HEADS, n_tok, NSA_GROUP, 2, HEAD_DIM)
    o_nsa = jnp.stack([o6[:, 0, :, :, 0], o6[:, 1, :, :, 1]], axis=2).reshape(s * n_tok, NSA_HEADS * HEAD_DIM)

    dq5 = dq.reshape(s, n_tok, DIFF_HEADS, 2, HEAD_DIM).transpose(0, 2, 3, 1, 4)
    zd = jnp.zeros_like(dq5[:, :, 0])
    qd = jnp.concatenate([jnp.concatenate([dq5[:, :, 0], zd], -1), jnp.concatenate([zd, dq5[:, :, 1]], -1)], axis=2)
    pool_rows = diff_pool.reshape(n_pool, page_rows * 2 * DIFF_HEADS, DIFF_VDIM)
    o_d = diff_decode(page_table, pool_rows, qd, pad_tok(dkv), scal, diff_decode_tables(tab_d, n_tok), subln,
                      n_tok, page_rows)
    o_diff = o_d.transpose(0, 2, 1, 3).reshape(s * n_tok, DIFF_HEADS * DIFF_VDIM)
    return o_nsa, o_diff, kv, kvw, dkv


def kernel(x_prompt, x_sample, cache_nsa_kv, cache_diff_kv, cache_nsa_win, cache_swa, state_rwkv_wkv, state_rwkv_shift, page_table, rel_bias, norm_mix, norm_ffn, norm_final, w_in_even, w_out_even, nsa_cmp_w, diff_lambda, diff_subln, w_in_odd, w_out_odd, swa_sinks, rwkv_mu, rwkv_w0, rwkv_w2, rwkv_a0, rwkv_a2, rwkv_g2, rwkv_k_k, rwkv_k_a, rwkv_r_k, rwkv_ln_w, rwkv_ln_b, ffn_w_gate, ffn_w_up, ffn_w_down):
    b, t, d = x_prompt.shape
    s, n_tok, _ = x_sample.shape
    depth = norm_mix.shape[0]
    assert NSA_WINDOW == 2 * TQ and t % TQ == 0 and TQ >= MAX_DISTANCE and n_tok <= NEW_ROWS and depth > 0
    tab = rel_bias.astype(F32)
    tab_d = tab[:, NSA_HEADS:NSA_HEADS + DIFF_HEADS]
    head_id = np.arange(RWKV_WIDTH) // RWKV_N
    hsum = jnp.asarray((head_id[:, None] == head_id[None, :]).astype(np.float32), BF16)
    xp = x_prompt.reshape(b * t, d)
    xs = x_sample.reshape(s * n_tok, d)
    outs = {name: [] for name in ("nsa_p", "nsa_s", "diff_p", "diff_s", "win_p", "win_s",
                                  "swa_p", "swa_s", "wkv_p", "wkv_s", "sh_p", "sh_s")}
    for l in range(depth):
        if l % 2 == 0:
            e = l // 2
            w_even = even_weights(w_in_even[e])
            w1, w2 = cmp_lane_weights(nsa_cmp_w[e])
            scal = diff_scalars(diff_lambda[e], l, tab_d)
            pa, pb, kv, kvw, dkv = even_prompt(xp.reshape(b, t, d), norm_mix[l], w_even, w1, w2, tab, scal, diff_subln[e])
            sa, sb, skv, skvw, sdkv = even_sample(xs.reshape(s, n_tok, d), norm_mix[l], w_even, cache_nsa_kv[e],
                                                  cache_diff_kv[e], cache_nsa_win[e], page_table, w1, w2, tab, scal,
                                                  diff_subln[e])
            outs["nsa_p"].append(kv.reshape(b, t, 4, NSA_KV_HEADS, HEAD_DIM))
            outs["nsa_s"].append(skv.reshape(s, n_tok, 4, NSA_KV_HEADS, HEAD_DIM))
            outs["diff_p"].append(dkv.reshape(b, t, 2, DIFF_HEADS, DIFF_VDIM))
            outs["diff_s"].append(sdkv.reshape(s, n_tok, 2, DIFF_HEADS, DIFF_VDIM))
            outs["win_p"].append(kvw.reshape(b, t, 2, NSA_KV_HEADS, HEAD_DIM)[:, t - min(NSA_WINDOW, t):])
            new_win = skvw.reshape(s, n_tok, 2, NSA_KV_HEADS, HEAD_DIM).astype(cache_nsa_win.dtype)
            outs["win_s"].append(jnp.concatenate([cache_nsa_win[e], new_win], axis=1)[:, n_tok:])
            w_out = w_out_even[e].astype(BF16)
        else:
            o = l // 2
            rw = (rwkv_mu[o], rwkv_w0[o], rwkv_w2[o], rwkv_a0[o], rwkv_a2[o], rwkv_g2[o], rwkv_k_k[o],
                  rwkv_k_a[o], rwkv_r_k[o].reshape(-1), rwkv_ln_w[o], rwkv_ln_b[o])
            w_odd = odd_weights(w_in_odd[o])
            pa, pb, ctx_p, wkv_p, sh_p = odd_mixer(xp.reshape(b, t, d), norm_mix[l], w_odd, None, None, None, True,
                                                   swa_sinks[o], tab, rw, hsum)
            swa_buf = cache_swa[o].reshape(s, cache_swa.shape[2], -1).astype(F32)
            sa, sb, ctx_s, wkv_s, sh_s = odd_mixer(xs.reshape(s, n_tok, d), norm_mix[l], w_odd, swa_buf,
                                                   state_rwkv_wkv[o], state_rwkv_shift[o], False, swa_sinks[o], tab,
                                                   rw, hsum)
            kv_shape = (2, SWA_KV_HEADS, HEAD_DIM)
            outs["swa_p"].append(ctx_p.reshape(b, -1, *kv_shape))
            outs["swa_s"].append(ctx_s.reshape(s, -1, *kv_shape))
            outs["wkv_p"].append(wkv_p)
            outs["wkv_s"].append(wkv_s)
            outs["sh_p"].append(sh_p)
            outs["sh_s"].append(sh_s)
            w_out = w_out_odd[o].astype(BF16)
        half = pa.shape[1]
        xp = out_proj(xp, pa, pb, w_out[:half], w_out[half:])
        xs = out_proj(xs, sa, sb, w_out[:half], w_out[half:])
        wg, wu, wd = ffn_w_gate[l].astype(BF16), ffn_w_up[l].astype(BF16), ffn_w_down[l].astype(BF16)
        last = l == depth - 1
        xp = ffn(xp, norm_ffn[l], norm_final, wg, wu, wd, last)
        xs = ffn(xs, norm_ffn[l], norm_final, wg, wu, wd, last)
    st = lambda name: jnp.stack(outs[name])
    return (xp.reshape(b, t, d), xs.reshape(s, n_tok, d), st("nsa_p"), st("nsa_s"), st("diff_p"), st("diff_s"),
            st("win_p"), st("win_s"), st("swa_p"), st("swa_s"), st("wkv_p"), st("wkv_s"), st("sh_p"), st("sh_s"))
```

```python
import functools
import math

import numpy as np
import jax
import jax.numpy as jnp
from jax import lax
from jax.experimental import pallas as pl
from jax.experimental.pallas import tpu as pltpu

F32 = jnp.float32
BF16 = jnp.bfloat16
HI = lax.Precision.HIGHEST

HEAD_DIM = 64
NSA_KV_HEADS = 2
NSA_GROUP = 4
NSA_HEADS = NSA_KV_HEADS * NSA_GROUP
CMP_STRIDE = 16
CMP_LEN = 32
SEL_BLOCK = 64
N_SELECT = 16
NSA_WINDOW = 512
DIFF_HEADS = 4
DIFF_VDIM = 128
SWA_HEADS = 8
SWA_KV_HEADS = 2
SWA_GROUP = 4
SWA_WINDOW = 128
RWKV_N = 64
RWKV_HEADS = 8
RWKV_WIDTH = RWKV_N * RWKV_HEADS
DECAY_LORA = 32
AAA_LORA = 32
GATE_LORA = 96
LORA_PAD = 256
N_BUCKETS = 32
MAX_DISTANCE = 128
NORM_EPS = 1e-6
SUBLN_EPS = 1e-5
RWKV_GN_EPS = 64e-5
NEG = -1e30
FORCE = 1e6
REMOVED = -3e38

LANES = 128
SUBLANES = 8
VMEM_LIMIT_BYTES = 56 * 1024 * 1024

TQ = 256
FAR_TILES = 4
ROW_TILE = 512
CMP_PAD = 16

NN = (((1,), (0,)), ((), ()))
NT = (((1,), (1,)), ((), ()))
TN = (((0,), (0,)), ((), ()))


def _cparams(sem):
    return pltpu.CompilerParams(dimension_semantics=sem, vmem_limit_bytes=VMEM_LIMIT_BYTES)


def _const_spec(shape):
    n = len(shape)
    return pl.BlockSpec(shape, lambda *_: (0,) * n)


def _smem_spec():
    return pl.BlockSpec(memory_space=pltpu.SMEM)


def _dot(a, b, precision=None):
    return jnp.dot(a, b, preferred_element_type=F32, precision=precision)


def _dot_nt(a, b, precision=None):
    return lax.dot_general(a, b, NT, preferred_element_type=F32, precision=precision)


def _split(x):
    hi = x.astype(BF16)
    return hi, (x - hi.astype(F32)).astype(BF16)


def _dot3(a, b, dims=NN):
    f = lambda x, y: lax.dot_general(x, y, dims, preferred_element_type=F32)
    return f(a[0], b[0]) + (f(a[1], b[0]) + f(a[0], b[1]))


def _dot2(a, b, dims=NN):
    f = lambda x, y: lax.dot_general(x, y, dims, preferred_element_type=F32)
    return f(a[0], b) + f(a[1], b)


def _norm_proj_body(x_ref, g_ref, w_ref, *out_refs, groups):
    x = x_ref[...]
    h = (x * lax.rsqrt(jnp.mean(x * x, axis=-1, keepdims=True) + NORM_EPS)) * g_ref[...]
    hb = h.astype(BF16)
    i = 0
    for off, wd, dts in groups:
        r = _dot(hb, w_ref[:, off:off + wd])
        for dt in dts:
            out_refs[i][...] = r.astype(dt)
            i += 1


def norm_proj(x2d, g, w_bf16, groups):
    m, d = x2d.shape
    tm = min(ROW_TILE, m)
    out_shape, out_specs = [], []
    for _, wd, dts in groups:
        for dt in dts:
            out_shape.append(jax.ShapeDtypeStruct((m, wd), dt))
            out_specs.append(pl.BlockSpec((tm, wd), lambda i: (i, 0)))
    return pl.pallas_call(
        functools.partial(_norm_proj_body, groups=groups),
        out_shape=out_shape,
        grid=(m // tm,),
        in_specs=[pl.BlockSpec((tm, d), lambda i: (i, 0)), _const_spec((1, d)), _const_spec(w_bf16.shape)],
        out_specs=out_specs,
        compiler_params=_cparams(("parallel",)),
        name="norm_proj",
    )(x2d, g.reshape(1, d), w_bf16)


def _out_proj_body(x_ref, a_ref, b_ref, wa_ref, wb_ref, o_ref):
    acc = _dot(a_ref[...].astype(BF16), wa_ref[...]) + _dot(b_ref[...].astype(BF16), wb_ref[...])
    o_ref[...] = x_ref[...] + acc


def out_proj(x2d, a, b, wa, wb):
    m, d = x2d.shape
    tm = min(ROW_TILE, m)
    return pl.pallas_call(
        _out_proj_body,
        out_shape=jax.ShapeDtypeStruct((m, d), F32),
        grid=(m // tm,),
        in_specs=[pl.BlockSpec((tm, d), lambda i: (i, 0)),
                  pl.BlockSpec((tm, a.shape[1]), lambda i: (i, 0)),
                  pl.BlockSpec((tm, b.shape[1]), lambda i: (i, 0)),
                  _const_spec(wa.shape), _const_spec(wb.shape)],
        out_specs=pl.BlockSpec((tm, d), lambda i: (i, 0)),
        compiler_params=_cparams(("parallel",)),
        name="out_proj",
    )(x2d, a, b, wa, wb)


def _ffn_body(x_ref, g_ref, gf_ref, wg_ref, wu_ref, wd_ref, o_ref, h_sc, acc_sc, *, final_norm):
    f = pl.program_id(1)

    @pl.when(f == 0)
    def _():
        x = x_ref[...]
        h = (x * lax.rsqrt(jnp.mean(x * x, axis=-1, keepdims=True) + NORM_EPS)) * g_ref[...]
        h_sc[...] = h.astype(BF16)
        acc_sc[...] = jnp.zeros_like(acc_sc)

    hb = h_sc[...]
    gate = _dot(hb, wg_ref[...])
    up = _dot(hb, wu_ref[...])
    act = (gate * jax.nn.sigmoid(gate)) * up
    acc_sc[...] += _dot(act.astype(BF16), wd_ref[...])

    @pl.when(f == pl.num_programs(1) - 1)
    def _():
        y = x_ref[...] + acc_sc[...]
        if final_norm:
            y = (y * lax.rsqrt(jnp.mean(y * y, axis=-1, keepdims=True) + NORM_EPS)) * gf_ref[...]
        o_ref[...] = y


def ffn(x2d, g, g_final, wg, wu, wd, final_norm):
    m, d = x2d.shape
    dff = wg.shape[1]
    tm = min(ROW_TILE, m)
    tf = dff // 2
    return pl.pallas_call(
        functools.partial(_ffn_body, final_norm=final_norm),
        out_shape=jax.ShapeDtypeStruct((m, d), F32),
        grid=(m // tm, dff // tf),
        in_specs=[pl.BlockSpec((tm, d), lambda i, f: (i, 0)), _const_spec((1, d)), _const_spec((1, d)),
                  pl.BlockSpec((d, tf), lambda i, f: (0, f)), pl.BlockSpec((d, tf), lambda i, f: (0, f)),
                  pl.BlockSpec((tf, d), lambda i, f: (f, 0))],
        out_specs=pl.BlockSpec((tm, d), lambda i, f: (i, 0)),
        scratch_shapes=[pltpu.VMEM((tm, d), BF16), pltpu.VMEM((tm, d), F32)],
        compiler_params=_cparams(("parallel", "arbitrary")),
        name="ffn",
    )(x2d, g.reshape(1, d), g_final.reshape(1, d), wg, wu, wd)


def _t5_bucket(dist):
    n = jnp.maximum(dist, 0)
    max_exact = N_BUCKETS // 2
    nf = jnp.maximum(n, 1).astype(F32)
    large = max_exact + (jnp.log(nf / max_exact) / math.log(MAX_DISTANCE / max_exact)
                         * (N_BUCKETS - max_exact)).astype(jnp.int32)
    large = jnp.minimum(large, N_BUCKETS - 1)
    return jnp.where(n < max_exact, n, large)


def _toeplitz_bias(tab, rows, cols, offset, valid_lo, valid_hi):
    length = rows + cols - 1
    d = jnp.arange(length, dtype=jnp.int32) - (cols - 1) + offset
    g = jnp.where((d >= valid_lo) & (d <= valid_hi), tab[_t5_bucket(d)].astype(F32).T, NEG)
    h = g[:, ::-1]
    flat = jnp.tile(h, (1, rows + 1))[:, :rows * (length + 1)].reshape(-1, rows, length + 1)[:, :, :cols]
    return flat[:, ::-1, :]


def _cmp_to_sel(n_cmp, n_sel, rows, cols):
    i = np.arange(n_cmp)[:, None]
    j = np.arange(n_sel)[None, :]
    m = (i * CMP_STRIDE < (j + 1) * SEL_BLOCK) & (i * CMP_STRIDE + CMP_LEN > j * SEL_BLOCK)
    out = np.zeros((rows, cols), np.float32)
    out[CMP_PAD:CMP_PAD + n_cmp, :n_sel] = m
    return out


def _block_expand(n_chunks, n_blocks, chunk_keys):
    c = np.arange(n_chunks)[:, None, None]
    j = np.arange(n_blocks)[None, :, None]
    l = np.arange(chunk_keys)[None, None, :]
    return (j == (c * chunk_keys + l) // SEL_BLOCK).astype(np.float32)


def _place_half(x64, half):
    z = jnp.zeros_like(x64)
    return jnp.concatenate([x64, z], axis=1) if half == 0 else jnp.concatenate([z, x64], axis=1)


def _lane_tile(x, width):
    reps = width // x.shape[1]
    return x if reps == 1 else jnp.concatenate([x] * reps, axis=1)


def _masked_softmax_parts(parts):
    m = None
    for s, msk in parts:
        mm = jnp.max(jnp.where(msk, s, NEG), axis=-1, keepdims=True)
        m = mm if m is None else jnp.maximum(m, mm)
    ps, den = [], None
    for s, msk in parts:
        p = jnp.where(msk, jnp.exp(jnp.where(msk, s, NEG) - m), 0.0)
        ps.append(p)
        d = jnp.sum(p, axis=-1, keepdims=True)
        den = d if den is None else den + d
    inv = 1.0 / jnp.maximum(den, 1e-30)
    return [p * inv for p in ps]


def _select_blocks(score, n_top):
    lane = lax.broadcasted_iota(jnp.int32, score.shape, 1)
    big = score.shape[1]
    sel = jnp.zeros(score.shape, F32)
    sc = score
    for _ in range(n_top):
        m = jnp.max(sc, axis=-1, keepdims=True)
        idx = jnp.min(jnp.where(sc == m, lane, big), axis=-1, keepdims=True)
        hit = lane == idx
        sel = jnp.where(hit & (m > 0.5 * NEG), 1.0, sel)
        sc = jnp.where(hit, REMOVED, sc)
    return sel


def _block_scores(imp, q_pos):
    j = lax.broadcasted_iota(jnp.int32, imp.shape, 1)
    cur = q_pos // SEL_BLOCK
    avail = j * SEL_BLOCK <= q_pos
    forced = (j == 0) | (j == cur) | (j == cur - 1)
    return jnp.where(avail, jnp.where(forced, FORCE, imp), NEG)


def _flash_init(h, m_sc, l_sc, acc_sc):
    m_sc[h] = jnp.full(m_sc.shape[1:], REMOVED, F32)
    l_sc[h] = jnp.zeros(l_sc.shape[1:], F32)
    acc_sc[h] = jnp.zeros(acc_sc.shape[1:], F32)


def _flash_update(h, s, vblk, m_sc, l_sc, acc_sc, shift=None):
    m_prev = m_sc[h]
    m_cur = jnp.max(s, axis=-1, keepdims=True)
    if shift is not None:
        m_cur = m_cur + shift
    m_next = jnp.maximum(m_prev, m_cur)
    alpha = jnp.exp(m_prev - m_next)
    sub = m_next if shift is None else m_next - shift
    p = jnp.exp(s - _lane_tile(sub, s.shape[1]))
    l_sc[h] = alpha * l_sc[h] + jnp.sum(p, axis=-1, keepdims=True)
    acc_sc[h] = alpha * acc_sc[h] + _dot(p.astype(BF16), vblk)
    m_sc[h] = m_next


def _causal_tiles(qi, step):
    n_far = jnp.maximum(qi - 1, 0)
    n_big = n_far // FAR_TILES

    def big_body(kb, carry):
        step(pl.multiple_of(kb * (FAR_TILES * TQ), FAR_TILES * TQ), FAR_TILES * TQ, None, kb)
        return carry

    def far_body(kt, carry):
        step(pl.multiple_of(kt * TQ, TQ), TQ, None, kt)
        return carry

    lax.fori_loop(0, n_big, big_body, 0)
    lax.fori_loop(n_big * FAR_TILES, n_far, far_body, 0)

    @pl.when(qi >= 1)
    def _():
        step(pl.multiple_of((qi - 1) * TQ, TQ), TQ, 0, qi - 1)

    step(pl.multiple_of(qi * TQ, TQ), TQ, 1, qi)


def _compress_body(x_ref, w1_ref, w2_ref, o_ref, *, nblk):
    x = x_ref[0].reshape(nblk, CMP_STRIDE, LANES)
    p1 = (x * w1_ref[0][None]).sum(axis=1)
    p2 = (x * w2_ref[0][None]).sum(axis=1)
    ck = p1 + pltpu.roll(p2, nblk - 1, 0)
    row = lax.broadcasted_iota(jnp.int32, ck.shape, 0)
    o_ref[0, 0] = jnp.zeros(o_ref.shape[2:], F32)
    o_ref[0, 0, CMP_PAD:CMP_PAD + nblk, :] = jnp.where(row < nblk - 1, ck, 0.0)


def compress_prompt(kv, w1, w2, cp):
    b, t, _ = kv.shape
    nblk = t // CMP_STRIDE
    return pl.pallas_call(
        functools.partial(_compress_body, nblk=nblk),
        out_shape=jax.ShapeDtypeStruct((b, 2, cp, LANES), F32),
        grid=(b, 2),
        in_specs=[pl.BlockSpec((1, t, LANES), lambda i, k: (i, 0, k)),
                  pl.BlockSpec((1, CMP_STRIDE, LANES), lambda i, k: (k, 0, 0)),
                  pl.BlockSpec((1, CMP_STRIDE, LANES), lambda i, k: (k, 0, 0))],
        out_specs=pl.BlockSpec((1, 1, cp, LANES), lambda i, k: (i, k, 0, 0)),
        compiler_params=_cparams(("parallel", "parallel")),
        name="compress_prompt",
    )(kv, w1, w2)


def _nsa_prompt_body(c31_ref, q_ref, gl_ref, ck_ref, cv_ref, c2s_ref, selk_ref, selv_ref,
                     wk0_ref, wk1_ref, wk2_ref, wv0_ref, wv1_ref, wv2_ref,
                     selb_ref, winb_ref, cnear_ref, e3_ref, e3big_ref, o_ref, m_sc, l_sc, acc_sc, *, cp):
    qi = pl.program_id(1)
    tq = q_ref.shape[1]
    near = tq // CMP_STRIDE + CMP_PAD
    near0 = pl.multiple_of(qi * (tq // CMP_STRIDE), tq // CMP_STRIDE)
    q = q_ref[0] * (HEAD_DIM ** -0.5)
    gates = jax.nn.sigmoid(gl_ref[0])
    ck = _split(ck_ref[0, 0])
    cv = cv_ref[0, 0].astype(BF16)
    ck_near = _split(ck_ref[0, 0, pl.ds(near0, near), :])
    cv_near = cv_ref[0, 0, pl.ds(near0, near), :].astype(BF16)
    c2s = c2s_ref[...].astype(BF16)
    c2s_near = c2s_ref[pl.ds(near0, near), :].astype(BF16)
    kcat = jnp.concatenate([wk0_ref[0], wk1_ref[0], wk2_ref[0]], axis=0)
    vcat = jnp.concatenate([wv0_ref[0], wv1_ref[0], wv2_ref[0]], axis=0)
    q_pos = qi * tq + lax.broadcasted_iota(jnp.int32, (tq, 1), 0)
    cp_idx = lax.broadcasted_iota(jnp.int32, (tq, cp), 1)
    far_mask = (cp_idx >= CMP_PAD) & (cp_idx < near0)
    near_lane = lax.broadcasted_iota(jnp.int32, (tq, near), 1)
    near_ok = (near_lane >= CMP_PAD) | (qi > 0)
    wcol = lax.broadcasted_iota(jnp.int32, (tq, 3 * tq), 1)
    win_ok = wcol >= (2 - qi) * tq

    qzb_all, o_cmp_all, imps = [], [], []
    for g in range(NSA_KV_HEADS):
        psum_far = jnp.zeros((tq, cp), F32)
        psum_near = jnp.zeros((tq, near), F32)
        for h in range(NSA_GROUP):
            hh = g * NSA_GROUP + h
            qz = _split(_place_half(q[:, hh * HEAD_DIM:(hh + 1) * HEAD_DIM], g))
            qzb_all.append(qz[0])
            s_far = _dot3(qz, ck, NT) + c31_ref[hh]
            nb = cnear_ref[hh]
            s_near = _dot3(qz, ck_near, NT) + nb
            p_far, p_near = _masked_softmax_parts([(s_far, far_mask), (s_near, (nb > 0.5 * NEG) & near_ok)])
            o_cmp_all.append(_dot(p_far.astype(BF16), cv) + _dot(p_near.astype(BF16), cv_near))
            psum_far = psum_far + p_far
            psum_near = psum_near + p_near
        imps.append(_dot2(_split(psum_far), c2s) + _dot2(_split(psum_near), c2s_near))
    sel_all = _select_blocks(_block_scores(jnp.concatenate(imps, axis=0), jnp.concatenate([q_pos] * len(imps), axis=0)),
                             N_SELECT).astype(BF16)

    pieces = []
    for g in range(NSA_KV_HEADS):
        qzb = qzb_all[g * NSA_GROUP:(g + 1) * NSA_GROUP]
        o_cmp = o_cmp_all[g * NSA_GROUP:(g + 1) * NSA_GROUP]
        sel = sel_all[g * tq:(g + 1) * tq]

        for h in range(NSA_GROUP):
            _flash_init(h, m_sc, l_sc, acc_sc)

        def sel_step(k0, size, kind, tile, g=g, qzb=qzb, sel=sel):
            kblk = selk_ref[0, pl.ds(k0, size), :]
            vblk = selv_ref[0, pl.ds(k0, size), :]
            expand = e3big_ref[tile] if size != tq else e3_ref[tile]
            madd = (_dot(sel, expand) - 1.0) * 1e30
            for h in range(NSA_GROUP):
                hh = g * NSA_GROUP + h
                s = _dot_nt(qzb[h], kblk)
                if kind is None:
                    _flash_update(h, s + madd, vblk, m_sc, l_sc, acc_sc, shift=c31_ref[hh])
                else:
                    _flash_update(h, s + (madd + selb_ref[hh, kind]), vblk, m_sc, l_sc, acc_sc)

        _causal_tiles(qi, sel_step)

        for h in range(NSA_GROUP):
            hh = g * NSA_GROUP + h
            wb = winb_ref[hh]
            s = _dot_nt(qzb[h], kcat) + wb
            msk = (wb > 0.5 * NEG) & win_ok
            (p,) = _masked_softmax_parts([(s, msk)])
            o_win = _dot(p.astype(BF16), vcat)
            o_sel = acc_sc[h] / l_sc[h]
            c = g * NSA_GROUP * 3 + h * 3
            o = gates[:, c:c + 1] * o_cmp[h] + gates[:, c + 1:c + 2] * o_sel + gates[:, c + 2:c + 3] * o_win
            pieces.append(o[:, g * HEAD_DIM:(g + 1) * HEAD_DIM])
    o_ref[0] = jnp.concatenate(pieces, axis=1)


def nsa_prompt(q, gl, ckv, kvb, kvwb, tabs):
    b, t, _ = q.shape
    nq = t // TQ
    cp = ckv.shape[2]

    def win_spec(back, col):
        return pl.BlockSpec((1, TQ, LANES), lambda i, j: (i, jnp.maximum(j - back, 0), col))

    return pl.pallas_call(
        functools.partial(_nsa_prompt_body, cp=cp),
        out_shape=jax.ShapeDtypeStruct((b, t, NSA_HEADS * HEAD_DIM), F32),
        grid=(b, nq),
        in_specs=[_smem_spec(),
                  pl.BlockSpec((1, TQ, NSA_HEADS * HEAD_DIM), lambda i, j: (i, j, 0)),
                  pl.BlockSpec((1, TQ, LANES), lambda i, j: (i, j, 0)),
                  pl.BlockSpec((1, 1, cp, LANES), lambda i, j: (i, 0, 0, 0)),
                  pl.BlockSpec((1, 1, cp, LANES), lambda i, j: (i, 1, 0, 0)),
                  _const_spec(tabs["c2s"].shape),
                  pl.BlockSpec((1, t, LANES), lambda i, j: (i, 0, 2)),
                  pl.BlockSpec((1, t, LANES), lambda i, j: (i, 0, 3)),
                  win_spec(2, 0), win_spec(1, 0), win_spec(0, 0),
                  win_spec(2, 1), win_spec(1, 1), win_spec(0, 1),
                  _const_spec(tabs["selb"].shape), _const_spec(tabs["winb"].shape),
                  _const_spec(tabs["cnear"].shape), _const_spec(tabs["e3"].shape),
                  _const_spec(tabs["e3big"].shape)],
        out_specs=pl.BlockSpec((1, TQ, NSA_HEADS * HEAD_DIM), lambda i, j: (i, j, 0)),
        scratch_shapes=[pltpu.VMEM((NSA_GROUP, TQ, LANES), F32), pltpu.VMEM((NSA_GROUP, TQ, LANES), F32),
                        pltpu.VMEM((NSA_GROUP, TQ, LANES), F32)],
        compiler_params=_cparams(("parallel", "parallel")),
        name="nsa_prompt",
    )(tabs["c31"], q, gl, ckv, ckv, tabs["c2s"], kvb, kvb, kvwb, kvwb, kvwb, kvwb, kvwb, kvwb,
      tabs["selb"], tabs["winb"], tabs["cnear"], tabs["e3"], tabs["e3big"])


def _prev_diag_bias(tab):
    big = 1 << 30
    return jnp.stack([_toeplitz_bias(tab, TQ, TQ, TQ, 0, big), _toeplitz_bias(tab, TQ, TQ, 0, 0, big)], axis=1)


def nsa_prompt_tables(tab_n, t):
    nq = t // TQ
    cp = t // CMP_STRIDE + LANES
    near = TQ // CMP_STRIDE + CMP_PAD
    cnear = _toeplitz_bias(tab_n, TQ, near * CMP_STRIDE, CMP_PAD * CMP_STRIDE - (CMP_LEN - 1), 0, 1 << 30)
    return dict(
        c31=tab_n[N_BUCKETS - 1].astype(F32),
        selb=_prev_diag_bias(tab_n),
        winb=_toeplitz_bias(tab_n, TQ, 3 * TQ, 2 * TQ, 0, NSA_WINDOW - 1),
        cnear=cnear[:, :, ::CMP_STRIDE],
        c2s=jnp.asarray(_cmp_to_sel(t // CMP_STRIDE - 1, t // SEL_BLOCK, cp, LANES)),
        e3=jnp.asarray(_block_expand(nq, LANES, TQ), BF16),
        e3big=jnp.asarray(_block_expand(max(nq // FAR_TILES, 1), LANES, FAR_TILES * TQ), BF16),
    )


DIFF_HEADS_PER_STEP = 2


def _diff_prompt_body(sc_ref, q_ref, k_ref, v_ref, bias_ref, sub_ref, o_ref, m_sc, l_sc, acc_sc):
    hp = pl.program_id(1)
    qi = pl.program_id(2)
    tq = q_ref.shape[1]
    nh = DIFF_HEADS_PER_STEP
    q = q_ref[0] * (HEAD_DIM ** -0.5)
    q2, c31 = [], []
    for h in range(nh):
        qh = q[:, h * LANES:(h + 1) * LANES]
        lane = lax.broadcasted_iota(jnp.int32, qh.shape, 1)
        q2.append(jnp.concatenate([jnp.where(lane < HEAD_DIM, qh, 0.0), jnp.where(lane >= HEAD_DIM, qh, 0.0)],
                                  axis=0).astype(BF16))
        c31.append(sc_ref[2 + hp * nh + h])
        _flash_init(h, m_sc, l_sc, acc_sc)

    def step(k0, size, kind, tile):
        del tile
        for h in range(nh):
            s = _dot_nt(q2[h], k_ref[0, pl.ds(k0, size), h * LANES:(h + 1) * LANES])
            vblk = v_ref[0, pl.ds(k0, size), h * LANES:(h + 1) * LANES]
            if kind is None:
                _flash_update(h, s, vblk, m_sc, l_sc, acc_sc, shift=c31[h])
            else:
                bt = bias_ref[h, kind]
                _flash_update(h, s + jnp.concatenate([bt, bt], axis=0), vblk, m_sc, l_sc, acc_sc)

    _causal_tiles(qi, step)
    outs = []
    for h in range(nh):
        o12 = acc_sc[h] / l_sc[h]
        o = o12[:tq] - sc_ref[0] * o12[tq:]
        y = (o * lax.rsqrt(jnp.mean(o * o, axis=-1, keepdims=True) + SUBLN_EPS)) * sub_ref[...]
        outs.append(y * sc_ref[1])
    o_ref[0] = jnp.concatenate(outs, axis=1)


def diff_prompt(dq, dkvb, scal, bias, subln):
    b, t, _ = dq.shape
    nq = t // TQ
    nh = DIFF_HEADS_PER_STEP
    wide = nh * LANES
    return pl.pallas_call(
        _diff_prompt_body,
        out_shape=jax.ShapeDtypeStruct((b, t, DIFF_HEADS * DIFF_VDIM), F32),
        grid=(b, DIFF_HEADS // nh, nq),
        in_specs=[_smem_spec(),
                  pl.BlockSpec((1, TQ, wide), lambda i, h, j: (i, j, h)),
                  pl.BlockSpec((1, t, wide), lambda i, h, j: (i, 0, h)),
                  pl.BlockSpec((1, t, wide), lambda i, h, j: (i, 0, DIFF_HEADS // nh + h)),
                  pl.BlockSpec((nh, 2, TQ, TQ), lambda i, h, j: (h, 0, 0, 0)),
                  _const_spec((1, DIFF_VDIM))],
        out_specs=pl.BlockSpec((1, TQ, wide), lambda i, h, j: (i, j, h)),
        scratch_shapes=[pltpu.VMEM((nh, 2 * TQ, LANES), F32), pltpu.VMEM((nh, 2 * TQ, LANES), F32),
                        pltpu.VMEM((nh, 2 * TQ, LANES), F32)],
        compiler_params=_cparams(("parallel", "parallel", "parallel")),
        name="diff_prompt",
    )(scal, dq, dkvb, dkvb, bias, subln.reshape(1, DIFF_VDIM))


NSA_PAGES_PER_STEP = 32
DIFF_PAGES_PER_STEP = 16
NEW_ROWS = 8


def _pad_rows(x, rows):
    return jnp.concatenate([x, jnp.zeros((rows - x.shape[0], x.shape[1]), x.dtype)], axis=0)


def _nsa_dec_body(pt_ref, *refs, n_pg, n_ch, past, cpd, n_tok):
    del pt_ref
    pages = refs[:n_pg]
    (qz_ref, glr_ref, newkv_ref, wbuf_ref, neww_ref, w1_ref, w2_ref, c2s_ref, cmpb_ref, selnear_ref,
     winb_ref, c31_ref, e3_ref, rsum_ref, rexp_ref, o_ref, p1k, p2k, p1v, p2v, sc_sc, vt_sc) = refs[n_pg:]
    c = pl.program_id(1)
    page_rows = pages[0].shape[2]
    blk_per_page = page_rows // CMP_STRIDE
    qz = qz_ref[0] * (HEAD_DIM ** -0.5)
    qzb = qz.astype(BF16)

    @pl.when(c == 0)
    def _():
        for ref in (p1k, p2k, p1v, p2v):
            ref[...] = jnp.zeros(ref.shape, F32)

    for i in range(n_pg):
        page = pages[i][0]
        pg = c * n_pg + i
        blk0 = pl.multiple_of(CMP_PAD + pg * blk_per_page, SUBLANES)
        xk = page[0:LANES, :].T.reshape(blk_per_page, CMP_STRIDE, LANES)
        xv = page[LANES:2 * LANES, :].T.reshape(blk_per_page, CMP_STRIDE, LANES)
        p1k[pl.ds(blk0, blk_per_page), :] = (xk * w1_ref[0][None]).sum(axis=1)
        p2k[pl.ds(blk0, blk_per_page), :] = (xk * w2_ref[0][None]).sum(axis=1)
        p1v[pl.ds(blk0, blk_per_page), :] = (xv * w1_ref[1][None]).sum(axis=1)
        p2v[pl.ds(blk0, blk_per_page), :] = (xv * w2_ref[1][None]).sum(axis=1)
        sc_sc[c, :, i * page_rows:(i + 1) * page_rows] = _dot(qzb, page[2 * LANES:3 * LANES, :].astype(BF16))
        vt_sc[c, :, i * page_rows:(i + 1) * page_rows] = page[3 * LANES:4 * LANES, :].astype(BF16)

    @pl.when(c == n_ch - 1)
    def _():
        nk = newkv_ref[0]
        s_new = _dot_nt(qzb, _pad_rows(nk[:, 2 * LANES:3 * LANES], LANES).astype(BF16))
        v_new = _pad_rows(nk[:, 3 * LANES:4 * LANES], LANES).astype(BF16)

        ck = p1k[...] + pltpu.roll(p2k[...], cpd - 1, 0)
        cv = p1v[...] + pltpu.roll(p2v[...], cpd - 1, 0)
        cb = cmpb_ref[...]
        (p_cmp,) = _masked_softmax_parts([(_dot_nt(qz, ck, HI) + cb, cb > 0.5 * NEG)])
        o_cmp = _dot(p_cmp, cv, HI)
        imp = _dot(_dot(rsum_ref[...], p_cmp, HI), c2s_ref[...], HI)
        rows = lax.broadcasted_iota(jnp.int32, (imp.shape[0], 1), 0)
        sel = _select_blocks(_block_scores(imp, past + rows % n_tok), N_SELECT)
        sel = _dot(rexp_ref[...], sel).astype(BF16)

        c31 = c31_ref[...][:, :1]
        selnear = selnear_ref[...]
        chunk = n_pg * page_rows
        bpc = chunk // SEL_BLOCK
        expand = e3_ref[...]
        parts = []
        for ci in range(n_ch):
            s = sc_sc[ci] + (_dot(sel[:, ci * bpc:(ci + 1) * bpc], expand) - 1.0) * 1e30
            if ci < n_ch - 1:
                s = s + c31
            else:
                s = s + jnp.concatenate([jnp.broadcast_to(c31, (s.shape[0], chunk - LANES)), selnear[:, :LANES]], axis=1)
            parts.append(s)
        parts.append(s_new + (_dot(sel[:, n_ch * bpc:(n_ch + 1) * bpc], expand[:, :LANES]) - 1.0) * 1e30 + selnear[:, LANES:])
        m = parts[0].max(axis=-1, keepdims=True)
        for s in parts[1:]:
            m = jnp.maximum(m, s.max(axis=-1, keepdims=True))
        den = jnp.zeros_like(m)
        acc = jnp.zeros((m.shape[0], LANES), F32)
        for ci, s in enumerate(parts):
            p = jnp.exp(s - m)
            den = den + p.sum(axis=-1, keepdims=True)
            if ci < n_ch:
                acc = acc + _dot_nt(p.astype(BF16), vt_sc[ci])
            else:
                acc = acc + _dot(p.astype(BF16), v_new)
        o_sel = acc / den

        wb = wbuf_ref[0]
        nw = neww_ref[0]
        kcat = jnp.concatenate([wb[:, :LANES], _pad_rows(nw[:, :LANES], LANES)], axis=0).astype(BF16)
        vcat = jnp.concatenate([wb[:, LANES:], _pad_rows(nw[:, LANES:], LANES)], axis=0).astype(BF16)
        wbias = winb_ref[...]
        (p_win,) = _masked_softmax_parts([(_dot_nt(qzb, kcat) + wbias, wbias > 0.5 * NEG)])
        o_win = _dot(p_win.astype(BF16), vcat)
        gates = jax.nn.sigmoid(glr_ref[0])
        o_ref[0] = gates[:, 0:1] * o_cmp + gates[:, 1:2] * o_sel + gates[:, 2:3] * o_win


def _row_tables(table, head, tok):
    return table[head, tok]


def nsa_decode_tables(tab_n, past, n_tok, wlen, n_ch, n_pg, page_rows):
    rows = NSA_KV_HEADS * n_tok * NSA_GROUP
    g = np.arange(rows) // (n_tok * NSA_GROUP)
    t = (np.arange(rows) // NSA_GROUP) % n_tok
    head = g * NSA_GROUP + np.arange(rows) % NSA_GROUP
    cpd = past // CMP_STRIDE + LANES
    n_sel_pad = 2 * LANES
    big = 1 << 30
    cmpb = _toeplitz_bias(tab_n, n_tok, cpd * CMP_STRIDE, past + CMP_PAD * CMP_STRIDE - (CMP_LEN - 1), 0, big)
    cmpb = _row_tables(cmpb[:, :, ::CMP_STRIDE], head, t)
    valid_cp = (np.arange(cpd) >= CMP_PAD)[None, :]
    cmpb = jnp.where(valid_cp, cmpb, NEG)
    selnear = _row_tables(_toeplitz_bias(tab_n, n_tok, 2 * LANES, LANES, 0, big), head, t)
    winb = _row_tables(_toeplitz_bias(tab_n, n_tok, wlen + LANES, wlen, 0, NSA_WINDOW - 1), head, t)
    c31 = jnp.broadcast_to(tab_n[N_BUCKETS - 1, head].astype(F32)[:, None], (rows, LANES))
    n_cmp = (past + SEL_BLOCK) // CMP_STRIDE - 1
    n_sel = (past + SEL_BLOCK) // SEL_BLOCK
    rsum = (np.arange(rows)[None, :] // NSA_GROUP == np.arange(rows // NSA_GROUP)[:, None]).astype(np.float32)
    return dict(cmpb=cmpb, selnear=selnear, winb=winb, c31=c31,
                c2s=jnp.asarray(_cmp_to_sel(min(n_cmp, cpd - CMP_PAD), n_sel, cpd, n_sel_pad)),
                e3=jnp.asarray(_block_expand(1, n_pg * page_rows // SEL_BLOCK, n_pg * page_rows)[0], BF16),
                rsum=jnp.asarray(rsum), rexp=jnp.asarray(rsum.T))


def nsa_decode(page_table, pool_t, qz, glr, newkv, wbuf, neww, w1, w2, tabs, n_tok):
    s, n_pages = page_table.shape
    page_rows = pool_t.shape[2]
    n_pg = min(NSA_PAGES_PER_STEP, n_pages)
    n_ch = n_pages // n_pg
    past = n_pages * page_rows
    cpd = tabs["cmpb"].shape[1]
    rows = qz.shape[1]

    def page_spec(i):
        return pl.BlockSpec((1, pool_t.shape[1], page_rows), lambda b, c, pt: (pt[b, c * n_pg + i], 0, 0))

    def seq_spec(a):
        return pl.BlockSpec((1,) + a.shape[1:], lambda b, c, pt: (b,) + (0,) * (a.ndim - 1))

    def const(a):
        n = a.ndim
        return pl.BlockSpec(a.shape, lambda b, c, pt: (0,) * n)

    consts = [w1, w2, tabs["c2s"], tabs["cmpb"], tabs["selnear"], tabs["winb"], tabs["c31"], tabs["e3"],
              tabs["rsum"], tabs["rexp"]]
    seqs = [qz, glr, newkv, wbuf, neww]
    grid_spec = pltpu.PrefetchScalarGridSpec(
        num_scalar_prefetch=1,
        grid=(s, n_ch),
        in_specs=[page_spec(i) for i in range(n_pg)] + [seq_spec(a) for a in seqs] + [const(a) for a in consts],
        out_specs=pl.BlockSpec((1, rows, LANES), lambda b, c, pt: (b, 0, 0)),
        scratch_shapes=[pltpu.VMEM((cpd, LANES), F32)] * 4
        + [pltpu.VMEM((n_ch, rows, n_pg * page_rows), F32), pltpu.VMEM((n_ch, LANES, n_pg * page_rows), BF16)],
    )
    return pl.pallas_call(
        functools.partial(_nsa_dec_body, n_pg=n_pg, n_ch=n_ch, past=past, cpd=cpd, n_tok=n_tok),
        out_shape=jax.ShapeDtypeStruct((s, rows, LANES), F32),
        grid_spec=grid_spec,
        compiler_params=_cparams(("parallel", "arbitrary")),
        name="nsa_decode",
    )(page_table, *([pool_t] * n_pg), *seqs, *consts)


def _diff_dec_body(pt_ref, *refs, n_pg, n_ch, past, n_tok, page_rows):
    del pt_ref
    pages = refs[:n_pg]
    sc_ref, qd_ref, newd_ref, near_ref, sub_ref, o_ref, s_sc, v_sc = refs[n_pg:]
    c = pl.program_id(1)
    per_pos = 2 * DIFF_HEADS
    qd = (qd_ref[0] * (HEAD_DIM ** -0.5)).astype(BF16)
    for i in range(n_pg):
        pg = c * n_pg + i
        for h in range(DIFF_HEADS):
            kh = pages[i][0, pl.ds(h, page_rows, stride=per_pos), :]
            vh = pages[i][0, pl.ds(DIFF_HEADS + h, page_rows, stride=per_pos), :]
            s_sc[c, h, :, i * page_rows:(i + 1) * page_rows] = _dot_nt(qd[h], kh.astype(BF16))
            v_sc[h, pl.ds(pl.multiple_of(pg * page_rows, page_rows), page_rows), :] = vh.astype(BF16)

    @pl.when(c == n_ch - 1)
    def _():
        nd = newd_ref[0]
        chunk = n_pg * page_rows
        for h in range(DIFF_HEADS):
            near = near_ref[h]
            c31 = sc_ref[2 + h]
            s_new = _dot_nt(qd[h], _pad_rows(nd[:, h * LANES:(h + 1) * LANES], LANES).astype(BF16)) + near[:, LANES:]
            v_sc[h, pl.ds(past, LANES), :] = _pad_rows(nd[:, (DIFF_HEADS + h) * LANES:(DIFF_HEADS + h + 1) * LANES], LANES).astype(BF16)
            parts = []
            for ci in range(n_ch):
                s = s_sc[ci, h]
                if ci < n_ch - 1:
                    s = s + c31
                else:
                    s = s + jnp.concatenate([jnp.full((s.shape[0], chunk - LANES), c31, F32), near[:, :LANES]], axis=1)
                parts.append(s)
            parts.append(s_new)
            m = parts[0].max(axis=-1, keepdims=True)
            for s in parts[1:]:
                m = jnp.maximum(m, s.max(axis=-1, keepdims=True))
            den = jnp.zeros_like(m)
            acc = jnp.zeros((m.shape[0], LANES), F32)
            for ci, s in enumerate(parts):
                p = jnp.exp(s - m)
                den = den + p.sum(axis=-1, keepdims=True)
                acc = acc + _dot(p.astype(BF16), v_sc[h, ci * chunk:ci * chunk + s.shape[1], :])
            o12 = acc / den
            o = o12[:n_tok] - sc_ref[0] * o12[n_tok:]
            y = (o * lax.rsqrt(jnp.mean(o * o, axis=-1, keepdims=True) + SUBLN_EPS)) * sub_ref[...]
            o_ref[0, h] = y * sc_ref[1]


def diff_decode_tables(tab_d, n_tok):
    near = _toeplitz_bias(tab_d, n_tok, 2 * LANES, LANES, 0, 1 << 30)
    return jnp.concatenate([near, near], axis=1)


def diff_decode(page_table, pool_rows, qd, newd, scal, near, subln, n_tok, page_rows):
    s, n_pages = page_table.shape
    n_pg = min(DIFF_PAGES_PER_STEP, n_pages)
    n_ch = n_pages // n_pg
    past = n_pages * page_rows

    def page_spec(i):
        return pl.BlockSpec((1,) + pool_rows.shape[1:], lambda b, c, pt: (pt[b, c * n_pg + i], 0, 0))

    grid_spec = pltpu.PrefetchScalarGridSpec(
        num_scalar_prefetch=1,
        grid=(s, n_ch),
        in_specs=[page_spec(i) for i in range(n_pg)] + [
            pl.BlockSpec(memory_space=pltpu.SMEM),
            pl.BlockSpec((1,) + qd.shape[1:], lambda b, c, pt: (b, 0, 0, 0)),
            pl.BlockSpec((1,) + newd.shape[1:], lambda b, c, pt: (b, 0, 0)),
            pl.BlockSpec(near.shape, lambda b, c, pt: (0, 0, 0)),
            pl.BlockSpec((1, DIFF_VDIM), lambda b, c, pt: (0, 0))],
        out_specs=pl.BlockSpec((1, DIFF_HEADS, n_tok, LANES), lambda b, c, pt: (b, 0, 0, 0)),
        scratch_shapes=[pltpu.VMEM((n_ch, DIFF_HEADS, 2 * n_tok, n_pg * page_rows), F32),
                        pltpu.VMEM((DIFF_HEADS, past + LANES, LANES), BF16)],
    )
    return pl.pallas_call(
        functools.partial(_diff_dec_body, n_pg=n_pg, n_ch=n_ch, past=past, n_tok=n_tok, page_rows=page_rows),
        out_shape=jax.ShapeDtypeStruct((s, DIFF_HEADS, n_tok, LANES), F32),
        grid_spec=grid_spec,
        compiler_params=_cparams(("parallel", "arbitrary")),
        name="diff_decode",
    )(page_table, *([pool_rows] * n_pg), scal, qd, newd, near, subln.reshape(1, DIFF_VDIM))


def _sink_attention(qzb, kcat, vcat, bias, ok, sink):
    s = _dot_nt(qzb, kcat) + bias
    msk = (bias > 0.5 * NEG) & ok
    sm = jnp.where(msk, s, NEG)
    m = jnp.maximum(jnp.max(sm, axis=-1, keepdims=True), sink)
    p = jnp.where(msk, jnp.exp(sm - m), 0.0)
    den = jnp.sum(p, axis=-1, keepdims=True) + jnp.exp(sink - m)
    p = p / jnp.maximum(den, 1e-30)
    return _dot(p.astype(BF16), vcat)


def _swa_prompt_body(sink_ref, q_ref, kp_ref, kc_ref, vp_ref, vc_ref, bias_ref, o_ref):
    qi = pl.program_id(1)
    tq = q_ref.shape[1]
    q = q_ref[0] * (HEAD_DIM ** -0.5)
    kcat = jnp.concatenate([kp_ref[0], kc_ref[0]], axis=0)
    vcat = jnp.concatenate([vp_ref[0], vc_ref[0]], axis=0)
    col = lax.broadcasted_iota(jnp.int32, (tq, kcat.shape[0]), 1)
    ok = (col >= SWA_WINDOW) | (qi > 0)
    pieces = []
    for g in range(SWA_KV_HEADS):
        for h in range(SWA_GROUP):
            hh = g * SWA_GROUP + h
            qzb = _place_half(q[:, hh * HEAD_DIM:(hh + 1) * HEAD_DIM], g).astype(BF16)
            o = _sink_attention(qzb, kcat, vcat, bias_ref[hh], ok, sink_ref[hh])
            pieces.append(o[:, g * HEAD_DIM:(g + 1) * HEAD_DIM])
    o_ref[0] = jnp.concatenate(pieces, axis=1)


def swa_prompt(q, kvb, sinks, tab_s):
    b, t, _ = q.shape
    bias = _toeplitz_bias(tab_s, TQ, SWA_WINDOW + TQ, SWA_WINDOW, 0, SWA_WINDOW - 1)
    per = TQ // SWA_WINDOW

    def prev_spec(col):
        return pl.BlockSpec((1, SWA_WINDOW, LANES), lambda i, j: (i, jnp.maximum(per * j - 1, 0), col))

    def cur_spec(col):
        return pl.BlockSpec((1, TQ, LANES), lambda i, j: (i, j, col))

    return pl.pallas_call(
        _swa_prompt_body,
        out_shape=jax.ShapeDtypeStruct((b, t, SWA_HEADS * HEAD_DIM), F32),
        grid=(b, t // TQ),
        in_specs=[_smem_spec(), pl.BlockSpec((1, TQ, SWA_HEADS * HEAD_DIM), lambda i, j: (i, j, 0)),
                  prev_spec(0), cur_spec(0), prev_spec(1), cur_spec(1), _const_spec(bias.shape)],
        out_specs=pl.BlockSpec((1, TQ, SWA_HEADS * HEAD_DIM), lambda i, j: (i, j, 0)),
        compiler_params=_cparams(("parallel", "parallel")),
        name="swa_prompt",
    )(sinks.astype(F32), q, kvb, kvb, kvb, kvb, bias)


def _swa_sample_body(qz_ref, buf_ref, new_ref, bias_ref, sink_ref, o_ref):
    qzb = (qz_ref[0] * (HEAD_DIM ** -0.5)).astype(BF16)
    buf = buf_ref[0]
    new = new_ref[0]
    kcat = jnp.concatenate([buf[:, :LANES], _pad_rows(new[:, :LANES], LANES)], axis=0).astype(BF16)
    vcat = jnp.concatenate([buf[:, LANES:], _pad_rows(new[:, LANES:], LANES)], axis=0).astype(BF16)
    o_ref[0] = _sink_attention(qzb, kcat, vcat, bias_ref[...], True, sink_ref[...][:, :1])


def swa_sample(qz, buf, new, sinks, tab_s, n_tok):
    s, rows, _ = qz.shape
    wlen = buf.shape[1]
    g = np.arange(rows) // (n_tok * SWA_GROUP)
    t = (np.arange(rows) // SWA_GROUP) % n_tok
    head = g * SWA_GROUP + np.arange(rows) % SWA_GROUP
    bias = _row_tables(_toeplitz_bias(tab_s, n_tok, wlen + LANES, wlen, 0, SWA_WINDOW - 1), head, t)
    sink_rows = jnp.broadcast_to(sinks.astype(F32)[head][:, None], (rows, LANES))
    return pl.pallas_call(
        _swa_sample_body,
        out_shape=jax.ShapeDtypeStruct((s, rows, LANES), F32),
        grid=(s,),
        in_specs=[pl.BlockSpec((1, rows, LANES), lambda i: (i, 0, 0)),
                  pl.BlockSpec((1, wlen, 2 * LANES), lambda i: (i, 0, 0)),
                  pl.BlockSpec((1, NEW_ROWS, 2 * LANES), lambda i: (i, 0, 0)),
                  _const_spec(bias.shape), _const_spec(sink_rows.shape)],
        out_specs=pl.BlockSpec((1, rows, LANES), lambda i: (i, 0, 0)),
        compiler_params=_cparams(("parallel",)),
        name="swa_sample",
    )(qz, buf, new, bias, sink_rows)


RWKV_CHUNK = 64
RWKV_SEQS_PER_STEP = 4
RWKV_PROMPTS_PER_STEP = 2


def _head_sum(x, hsum_ref):
    return _dot2(_split(x), hsum_ref[...])


def _rwkv_prep_body(ur_ref, uk_ref, uv_ref, ul_ref, pr_ref, pk_ref, pv_ref, pl_ref,
                    mur_ref, muk_ref, muv_ref, mul_ref, w0_ref, w2_ref, a0_ref, a2_ref, g2_ref,
                    kk_ref, ka_ref, hsum_ref, r_o, lw_o, k_o, v_o, na_o, b_o, g_o):
    def mix(u_ref, p_ref, mu_ref):
        u = u_ref[...]
        return u + (p_ref[...] - u) * mu_ref[...]

    r = mix(ur_ref, pr_ref, mur_ref)
    k = mix(uk_ref, pk_ref, muk_ref)
    v = mix(uv_ref, pv_ref, muv_ref)
    lo = mix(ul_ref, pl_ref, mul_ref)
    z = -(w0_ref[...] + _dot3(_split(jnp.tanh(lo)), _split(w2_ref[...])))
    softplus = jnp.maximum(z, 0.0) + jnp.log(1.0 + jnp.exp(-jnp.abs(z)))
    wlog = -softplus - 0.5
    a = jax.nn.sigmoid(a0_ref[...] + _dot3(_split(lo), _split(a2_ref[...])))
    g = _dot3(_split(jax.nn.sigmoid(lo)), _split(g2_ref[...]))
    kk = k * kk_ref[...]
    kk = kk / jnp.maximum(jnp.sqrt(_head_sum(kk * kk, hsum_ref)), 1e-12)
    r_o[...] = r
    lw_o[...] = -jnp.exp(wlog)
    k_o[...] = k * (1.0 + (a - 1.0) * ka_ref[...])
    v_o[...] = v
    na_o[...] = -kk
    b_o[...] = kk * a
    g_o[...] = g


def rwkv_prep(us, prevs, mus, w0, w2p, a0, a2p, g2p, k_k, k_a, hsum):
    m = us[0].shape[0]
    tm = min(ROW_TILE, m)
    row = lambda a: pl.BlockSpec((tm, a.shape[1]), lambda i: (i, 0))
    vec = lambda a: a.reshape(1, -1)
    consts = [vec(x) for x in mus] + [vec(w0), w2p, vec(a0), a2p, g2p, vec(k_k), vec(k_a), hsum]
    return pl.pallas_call(
        _rwkv_prep_body,
        out_shape=[jax.ShapeDtypeStruct((m, RWKV_WIDTH), F32)] * 7,
        grid=(m // tm,),
        in_specs=[row(a) for a in us] + [row(a) for a in prevs] + [_const_spec(c.shape) for c in consts],
        out_specs=[pl.BlockSpec((tm, RWKV_WIDTH), lambda i: (i, 0))] * 7,
        compiler_params=_cparams(("parallel",)),
        name="rwkv_prep",
    )(*us, *prevs, *consts)


BNN = (((2,), (1,)), ((0,), (0,)))
BNT = (((2,), (2,)), ((0,), (0,)))
BTN = (((1,), (1,)), ((0,), (0,)))


def _rwkv_chunk_body(r_ref, lw_ref, k_ref, v_ref, a_ref, b_ref, s0_ref, y_ref, sT_ref, st_sc):
    ci = pl.program_id(1)
    nb, c, _ = r_ref.shape
    n = RWKV_N

    def to_batch(ref):
        x = ref[...]
        return jnp.concatenate([x[:, :, h * n:(h + 1) * n] for h in range(RWKV_HEADS)], axis=0)

    @pl.when(ci == 0)
    def _():
        s0 = s0_ref[...]
        st_sc[...] = jnp.concatenate([s0[:, h] for h in range(RWKV_HEADS)], axis=0)

    row = lax.broadcasted_iota(jnp.int32, (c, c), 0)
    col = lax.broadcasted_iota(jnp.int32, (c, c), 1)
    incl = (row >= col)[None]
    strict = (row > col)[None]
    eye_c = (row == col).astype(F32)[None]
    eye_n = (lax.broadcasted_iota(jnp.int32, (n, n), 0) == lax.broadcasted_iota(jnp.int32, (n, n), 1)).astype(F32)[None]
    r, lw, k, v, a, b = (to_batch(ref) for ref in (r_ref, lw_ref, k_ref, v_ref, a_ref, b_ref))
    nbat = r.shape[0]
    ones_incl = jnp.broadcast_to(incl.astype(BF16), (nbat, c, c))
    lw_hi, lw_lo = _split(lw)
    lw_lo2 = (lw - lw_hi.astype(F32) - lw_lo.astype(F32)).astype(BF16)
    csum = lambda y: lax.dot_general(ones_incl, y, BNN, preferred_element_type=F32)
    cs = csum(lw_hi) + (csum(lw_lo) + csum(lw_lo2))
    gam = jnp.exp(cs)
    ginv = jnp.exp(-cs)
    to_end = jnp.exp(cs[:, c - 1:c, :] - cs)
    at = _split(a * jnp.exp(cs - lw))
    rt_f = r * gam
    rt = _split(rt_f)
    bt = _split(b * ginv)
    kt = _split(k * ginv)
    vs = _split(v)
    lb = jnp.where(strict, _dot3(at, bt, BNT), 0.0)
    lk = jnp.where(strict, _dot3(at, kt, BNT), 0.0)
    pb = _split(jnp.where(incl, _dot3(rt, bt, BNT), 0.0))
    pk = _split(jnp.where(incl, _dot3(rt, kt, BNT), 0.0))
    tinv = eye_c + lb
    lp = _split(lb)
    covered = 2
    while covered < c:
        lp = _split(_dot3(lp, lp, BNN))
        tinv = tinv + _dot3(_split(tinv), lp, BNN)
        covered *= 2
    tinv = _split(tinv)
    w = _split(_dot3(tinv, at, BNN))
    uv = _split(_dot3(tinv, _split(_dot3(_split(lk), vs, BNN)), BNN))
    q = rt_f + _dot3(pb, w, BNN)
    bh = _split(b * to_end)
    kh = _split(k * to_end)
    tm = eye_n * gam[:, c - 1:c, :] + _dot3(bh, w, BTN)
    bm = _dot3(bh, uv, BTN) + _dot3(kh, vs, BTN)
    x = _dot3(_split(jnp.concatenate([q, tm], axis=1)), _split(st_sc[...]), BNN)
    y = x[:, :c] + _dot3(pb, uv, BNN) + _dot3(pk, vs, BNN)
    st_sc[...] = x[:, c:] + bm
    y_ref[...] = jnp.concatenate([y[h * nb:(h + 1) * nb] for h in range(RWKV_HEADS)], axis=2)

    @pl.when(ci == pl.num_programs(1) - 1)
    def _():
        st = st_sc[...]
        for h in range(RWKV_HEADS):
            sT_ref[:, h] = st[h * nb:(h + 1) * nb]


def rwkv_chunk(seqs, s0t, chunk, nb):
    b, t, width = seqs[0].shape
    h, n = s0t.shape[1:3]
    seq_spec = pl.BlockSpec((nb, chunk, width), lambda i, j: (i, j, 0))
    st_spec = pl.BlockSpec((nb, h, n, n), lambda i, j: (i, 0, 0, 0))
    return pl.pallas_call(
        _rwkv_chunk_body,
        out_shape=[jax.ShapeDtypeStruct((b, t, width), F32), jax.ShapeDtypeStruct((b, h, n, n), F32)],
        grid=(b // nb, t // chunk),
        in_specs=[seq_spec] * 6 + [st_spec],
        out_specs=[seq_spec, st_spec],
        scratch_shapes=[pltpu.VMEM((h * nb, n, n), F32)],
        compiler_params=_cparams(("parallel", "arbitrary")),
        name="rwkv_chunk",
    )(*seqs, s0t)


def _rwkv_post_body(y_ref, r_ref, k_ref, v_ref, g_ref, rk_ref, lnw_ref, lnb_ref, hsum_ref, o_ref):
    y = y_ref[...]
    inv_n = 1.0 / RWKV_N
    mean = _head_sum(y, hsum_ref) * inv_n
    yc = y - mean
    var = _head_sum(yc * yc, hsum_ref) * inv_n
    yn = yc * lax.rsqrt(var + RWKV_GN_EPS) * lnw_ref[...] + lnb_ref[...]
    rk = _head_sum(r_ref[...] * k_ref[...] * rk_ref[...], hsum_ref)
    o_ref[...] = (yn + rk * v_ref[...]) * g_ref[...]


def rwkv_post(y, r, k, v, g, r_k, ln_w, ln_b, hsum):
    m = y.shape[0]
    tm = min(ROW_TILE, m)
    row = pl.BlockSpec((tm, RWKV_WIDTH), lambda i: (i, 0))
    vec = lambda a: a.reshape(1, -1)
    return pl.pallas_call(
        _rwkv_post_body,
        out_shape=jax.ShapeDtypeStruct((m, RWKV_WIDTH), F32),
        grid=(m // tm,),
        in_specs=[row] * 5 + [_const_spec((1, RWKV_WIDTH))] * 3 + [_const_spec(hsum.shape)],
        out_specs=row,
        compiler_params=_cparams(("parallel",)),
        name="rwkv_post",
    )(y, r, k, v, g, vec(r_k), vec(ln_w), vec(ln_b), hsum)


RWKV_SIZES = (RWKV_WIDTH, RWKV_WIDTH, RWKV_WIDTH, DECAY_LORA, AAA_LORA, GATE_LORA)
LORA_WIDTH = DECAY_LORA + AAA_LORA + GATE_LORA
ODD_SIZES = (SWA_HEADS * HEAD_DIM, 2 * SWA_KV_HEADS * HEAD_DIM, 3 * RWKV_WIDTH + LORA_WIDTH)
ODD_GROUPS = ((0, 512, (F32,)), (512, 256, (F32, BF16)), (768, 512, (F32,)), (1280, 512, (F32,)),
              (1792, 512, (F32,)), (2304, LORA_PAD, (F32,)))


def _split_cols(w, sizes):
    offs = np.cumsum([0] + list(sizes))
    return [w[..., int(offs[i]):int(offs[i + 1])] for i in range(len(sizes))]


def odd_weights(w_in):
    return jnp.pad(w_in, ((0, 0), (0, LORA_PAD - LORA_WIDTH))).astype(BF16)


def _lora_rows(w, first):
    return jnp.pad(w.astype(F32), ((first, LORA_PAD - first - w.shape[0]), (0, 0)))


def odd_mixer(x, g, w_odd, swa_buf, wkv0, shift0, prompt, sinks, tab, rw, hsum):
    b, t, d = x.shape
    mu, w0, w2, a0, a2, g2, k_k, k_a, r_k, ln_w, ln_b = rw
    q, kv, kvb, ur, uk, uv, ul = norm_proj(x.reshape(b * t, d), g, w_odd, ODD_GROUPS)
    r3 = lambda a: a.reshape(b, t, a.shape[-1])
    tab_s = tab[:, :SWA_HEADS]
    if prompt:
        o_swa = swa_prompt(r3(q), r3(kvb), sinks, tab_s).reshape(b * t, -1)
        ctx = r3(kv)[:, t - min(SWA_WINDOW, t):]
        shift0 = jnp.zeros((b, 3 * RWKV_WIDTH + LORA_WIDTH), F32)
        wkv0 = jnp.zeros((b, RWKV_HEADS, RWKV_N, RWKV_N), F32)
        chunk = RWKV_CHUNK
        t_pad = t
        nb = math.gcd(b, RWKV_PROMPTS_PER_STEP)
    else:
        rows = SWA_KV_HEADS * t * SWA_GROUP
        q5 = q.reshape(b, t, SWA_KV_HEADS, SWA_GROUP, HEAD_DIM).transpose(0, 2, 1, 3, 4)
        z = jnp.zeros_like(q5[:, 0])
        qz = jnp.stack([jnp.concatenate([q5[:, 0], z], -1), jnp.concatenate([z, q5[:, 1]], -1)], axis=1).reshape(b, rows, LANES)
        new = jnp.pad(r3(kv), ((0, 0), (0, NEW_ROWS - t), (0, 0)))
        o_rows = swa_sample(qz, swa_buf, new, sinks, tab_s, t)
        o6 = o_rows.reshape(b, SWA_KV_HEADS, t, SWA_GROUP, 2, HEAD_DIM)
        o_swa = jnp.stack([o6[:, 0, :, :, 0], o6[:, 1, :, :, 1]], axis=2).reshape(b * t, SWA_HEADS * HEAD_DIM)
        ctx = jnp.concatenate([swa_buf, r3(kv)], axis=1)[:, t:]
        chunk = NEW_ROWS
        t_pad = NEW_ROWS
        nb = math.gcd(b, RWKV_SEQS_PER_STEP)

    s_r, s_k, s_v, s_l = _split_cols(shift0.astype(F32), (RWKV_WIDTH,) * 3 + (LORA_WIDTH,))
    s_l = jnp.pad(s_l, ((0, 0), (0, LORA_PAD - LORA_WIDTH)))
    us = [ur, uk, uv, ul]
    prevs = [jnp.concatenate([s[:, None], r3(u)[:, :-1]], axis=1).reshape(b * t, -1) for s, u in zip((s_r, s_k, s_v, s_l), us)]
    mu_r, mu_k, mu_v, mu_l = _split_cols(mu.astype(F32), (RWKV_WIDTH,) * 3 + (LORA_WIDTH,))
    mu_l = jnp.pad(mu_l, (0, LORA_PAD - LORA_WIDTH))
    r, lw, k, v, na, bb, gate = rwkv_prep(us, prevs, [mu_r, mu_k, mu_v, mu_l], w0, _lora_rows(w2, 0), a0,
                                          _lora_rows(a2, DECAY_LORA), _lora_rows(g2, DECAY_LORA + AAA_LORA), k_k, k_a, hsum)

    def seq(a):
        return jnp.pad(a.reshape(b, t, RWKV_WIDTH), ((0, 0), (0, t_pad - t), (0, 0)))

    y, st = rwkv_chunk([seq(a) for a in (r, lw, k, v, na, bb)], jnp.swapaxes(wkv0.astype(F32), -1, -2), chunk, nb)
    y = y[:, :t].reshape(b * t, RWKV_WIDTH)
    o_rwkv = rwkv_post(y, r, k, v, gate, r_k, ln_w, ln_b, hsum)
    shift = jnp.concatenate([r3(ur)[:, -1], r3(uk)[:, -1], r3(uv)[:, -1], r3(ul)[:, -1, :LORA_WIDTH]], axis=-1)
    return o_swa, o_rwkv, ctx, jnp.swapaxes(st, -1, -2), shift


EVEN_SIZES = (NSA_HEADS * HEAD_DIM, 4 * NSA_KV_HEADS * HEAD_DIM, 2 * NSA_KV_HEADS * HEAD_DIM, 3 * NSA_HEADS,
              2 * DIFF_HEADS * HEAD_DIM, 2 * DIFF_HEADS * DIFF_VDIM)
EVEN_GROUPS = ((0, 512, (F32,)), (512, 512, (F32, BF16)), (1024, 256, (F32, BF16)), (1280, 512, (F32,)),
               (1792, 1024, (F32, BF16)), (2816, 128, (F32,)))


def even_weights(w_in):
    wq, wkv, wkvw, wgl, wdq, wdkv = _split_cols(w_in, EVEN_SIZES)
    wgl = jnp.pad(wgl, ((0, 0), (0, LANES - wgl.shape[1])))
    return jnp.concatenate([wq, wkv, wkvw, wdq, wdkv, wgl], axis=1).astype(BF16)


def cmp_lane_weights(cmp_w):
    wt = jax.nn.softmax(cmp_w.astype(F32), axis=-1)
    wl = jnp.repeat(jnp.swapaxes(wt, 1, 2), HEAD_DIM, axis=2)
    return wl[:, :CMP_STRIDE], wl[:, CMP_STRIDE:]


def diff_scalars(lq, layer, tab_d):
    lam_init = 0.8 - 0.6 * math.exp(-0.3 * layer)
    lq = lq.astype(F32)
    lam = jnp.exp(jnp.sum(lq[0] * lq[1])) - jnp.exp(jnp.sum(lq[2] * lq[3])) + lam_init
    return jnp.concatenate([jnp.stack([lam, jnp.asarray(1.0 - lam_init, F32)]), tab_d[N_BUCKETS - 1].astype(F32)])


def even_prompt(x, g, w_even, w1, w2, tab, scal, subln):
    b, t, d = x.shape
    q, kv, kvb, kvw, kvwb, dq, dkv, dkvb, gl = norm_proj(x.reshape(b * t, d), g, w_even, EVEN_GROUPS)
    r3 = lambda a: a.reshape(b, t, a.shape[-1])
    tab_n, tab_d = tab[:, :NSA_HEADS], tab[:, NSA_HEADS:NSA_HEADS + DIFF_HEADS]
    tabs = nsa_prompt_tables(tab_n, t)
    ckv = compress_prompt(r3(kv), w1, w2, tabs["c2s"].shape[0])
    o_nsa = nsa_prompt(r3(q), r3(gl), ckv, r3(kvb), r3(kvwb), tabs)
    o_diff = diff_prompt(r3(dq), r3(dkvb), scal, _prev_diag_bias(tab_d), subln)
    return o_nsa.reshape(b * t, -1), o_diff.reshape(b * t, -1), kv, kvw, dkv


def even_sample(x, g, w_even, nsa_pool, diff_pool, win_buf, page_table, w1, w2, tab, scal, subln):
    s, n_tok, d = x.shape
    q, kv, _, kvw, _, dq, dkv, _, gl = norm_proj(x.reshape(s * n_tok, d), g, w_even, EVEN_GROUPS)
    tab_n, tab_d = tab[:, :NSA_HEADS], tab[:, NSA_HEADS:NSA_HEADS + DIFF_HEADS]
    n_pages = page_table.shape[1]
    n_pool, page_rows = nsa_pool.shape[:2]
    past = n_pages * page_rows
    wlen = win_buf.shape[1]
    n_pg = min(NSA_PAGES_PER_STEP, n_pages)
    rows = NSA_KV_HEADS * n_tok * NSA_GROUP

    q5 = q.reshape(s, n_tok, NSA_KV_HEADS, NSA_GROUP, HEAD_DIM).transpose(0, 2, 1, 3, 4)
    z = jnp.zeros_like(q5[:, 0])
    qz = jnp.stack([jnp.concatenate([q5[:, 0], z], -1), jnp.concatenate([z, q5[:, 1]], -1)], axis=1).reshape(s, rows, LANES)
    glr = gl[:, :3 * NSA_HEADS].reshape(s, n_tok, NSA_KV_HEADS, NSA_GROUP, 3).transpose(0, 2, 1, 3, 4).reshape(s, rows, 3)
    glr = jnp.pad(glr, ((0, 0), (0, 0), (0, LANES - 3)))
    pad_tok = lambda a: jnp.pad(a.reshape(s, n_tok, a.shape[-1]), ((0, 0), (0, NEW_ROWS - n_tok), (0, 0)))
    tabs = nsa_decode_tables(tab_n, past, n_tok, wlen, n_pages // n_pg, n_pg, page_rows)
    pool_t = jnp.transpose(nsa_pool, (0, 2, 3, 4, 1)).reshape(n_pool, -1, page_rows)
    o_rows = nsa_decode(page_table, pool_t, qz, glr, pad_tok(kv), win_buf.reshape(s, wlen, -1), pad_tok(kvw),
                        w1, w2, tabs, n_tok)
    o6 = o_rows.reshape(s, NSA_KV_HEADS, n_tok, NSA_GROUP, 2, HEAD_DIM)
    o_nsa = jnp.stack([o6[:, 0, :, :, 0], o6[:, 1, :, :, 1]], axis=2).reshape(s * n_tok, NSA_HEADS * HEAD_DIM)

    dq5 = dq.reshape(s, n_tok, DIFF_HEADS, 2, HEAD_DIM).transpose(0, 2, 3, 1, 4)
    zd = jnp.zeros_like(dq5[:, :, 0])
    qd = jnp.concatenate([jnp.concatenate([dq5[:, :, 0], zd], -1), jnp.concatenate([zd, dq5[:, :, 1]], -1)], axis=2)
    pool_rows = diff_pool.reshape(n_pool, page_rows * 2 * DIFF_HEADS, DIFF_VDIM)
    o_d = diff_decode(page_table, pool_rows, qd, pad_tok(dkv), scal, diff_decode_tables(tab_d, n_tok), subln,
                      n_tok, page_rows)
    o_diff = o_d.transpose(0, 2, 1, 3).reshape(s * n_tok, DIFF_HEADS * DIFF_VDIM)
    return o_nsa, o_diff, kv, kvw, dkv


def kernel(x_prompt, x_sample, cache_nsa_kv, cache_diff_kv, cache_nsa_win, cache_swa, state_rwkv_wkv, state_rwkv_shift, page_table, rel_bias, norm_mix, norm_ffn, norm_final, w_in_even, w_out_even, nsa_cmp_w, diff_lambda, diff_subln, w_in_odd, w_out_odd, swa_sinks, rwkv_mu, rwkv_w0, rwkv_w2, rwkv_a0, rwkv_a2, rwkv_g2, rwkv_k_k, rwkv_k_a, rwkv_r_k, rwkv_ln_w, rwkv_ln_b, ffn_w_gate, ffn_w_up, ffn_w_down):
    b, t, d = x_prompt.shape
    s, n_tok, _ = x_sample.shape
    depth = norm_mix.shape[0]
    assert NSA_WINDOW == 2 * TQ and t % TQ == 0 and TQ >= MAX_DISTANCE and n_tok <= NEW_ROWS and depth > 0
    tab = rel_bias.astype(F32)
    tab_d = tab[:, NSA_HEADS:NSA_HEADS + DIFF_HEADS]
    head_id = np.arange(RWKV_WIDTH) // RWKV_N
    hsum = jnp.asarray((head_id[:, None] == head_id[None, :]).astype(np.float32), BF16)
    xp = x_prompt.reshape(b * t, d)
    xs = x_sample.reshape(s * n_tok, d)
    outs = {name: [] for name in ("nsa_p", "nsa_s", "diff_p", "diff_s", "win_p", "win_s",
                                  "swa_p", "swa_s", "wkv_p", "wkv_s", "sh_p", "sh_s")}
    for l in range(depth):
        if l % 2 == 0:
            e = l // 2
            w_even = even_weights(w_in_even[e])
            w1, w2 = cmp_lane_weights(nsa_cmp_w[e])
            scal = diff_scalars(diff_lambda[e], l, tab_d)
            pa, pb, kv, kvw, dkv = even_prompt(xp.reshape(b, t, d), norm_mix[l], w_even, w1, w2, tab, scal, diff_subln[e])
            sa, sb, skv, skvw, sdkv = even_sample(xs.reshape(s, n_tok, d), norm_mix[l], w_even, cache_nsa_kv[e],
                                                  cache_diff_kv[e], cache_nsa_win[e], page_table, w1, w2, tab, scal,
                                                  diff_subln[e])
            outs["nsa_p"].append(kv.reshape(b, t, 4, NSA_KV_HEADS, HEAD_DIM))
            outs["nsa_s"].append(skv.reshape(s, n_tok, 4, NSA_KV_HEADS, HEAD_DIM))
            outs["diff_p"].append(dkv.reshape(b, t, 2, DIFF_HEADS, DIFF_VDIM))
            outs["diff_s"].append(sdkv.reshape(s, n_tok, 2, DIFF_HEADS, DIFF_VDIM))
            outs["win_p"].append(kvw.reshape(b, t, 2, NSA_KV_HEADS, HEAD_DIM)[:, t - min(NSA_WINDOW, t):])
            new_win = skvw.reshape(s, n_tok, 2, NSA_KV_HEADS, HEAD_DIM).astype(cache_nsa_win.dtype)
            outs["win_s"].append(jnp.concatenate([cache_nsa_win[e], new_win], axis=1)[:, n_tok:])
            w_out = w_out_even[e].astype(BF16)
        else:
            o = l // 2
            rw = (rwkv_mu[o], rwkv_w0[o], rwkv_w2[o], rwkv_a0[o], rwkv_a2[o], rwkv_g2[o], rwkv_k_k[o],
                  rwkv_k_a[o], rwkv_r_k[o].reshape(-1), rwkv_ln_w[o], rwkv_ln_b[o])
            w_odd = odd_weights(w_in_odd[o])
            pa, pb, ctx_p, wkv_p, sh_p = odd_mixer(xp.reshape(b, t, d), norm_mix[l], w_odd, None, None, None, True,
                                                   swa_sinks[o], tab, rw, hsum)
            swa_buf = cache_swa[o].reshape(s, cache_swa.shape[2], -1).astype(F32)
            sa, sb, ctx_s, wkv_s, sh_s = odd_mixer(xs.reshape(s, n_tok, d), norm_mix[l], w_odd, swa_buf,
                                                   state_rwkv_wkv[o], state_rwkv_shift[o], False, swa_sinks[o], tab,
                                                   rw, hsum)
            kv_shape = (2, SWA_KV_HEADS, HEAD_DIM)
            outs["swa_p"].append(ctx_p.reshape(b, -1, *kv_shape))
            outs["swa_s"].append(ctx_s.reshape(s, -1, *kv_shape))
            outs["wkv_p"].append(wkv_p)
            outs["wkv_s"].append(wkv_s)
            outs["sh_p"].append(sh_p)
            outs["sh_s"].append(sh_s)
            w_out = w_out_odd[o].astype(BF16)
        half = pa.shape[1]
        xp = out_proj(xp, pa, pb, w_out[:half], w_out[half:])
        xs = out_proj(xs, sa, sb, w_out[:half], w_out[half:])
        wg, wu, wd = ffn_w_gate[l].astype(BF16), ffn_w_up[l].astype(BF16), ffn_w_down[l].astype(BF16)
        last = l == depth - 1
        xp = ffn(xp, norm_ffn[l], norm_final, wg, wu, wd, last)
        xs = ffn(xs, norm_ffn[l], norm_final, wg, wu, wd, last)
    st = lambda name: jnp.stack(outs[name])
    return (xp.reshape(b, t, d), xs.reshape(s, n_tok, d), st("nsa_p"), st("nsa_s"), st("diff_p"), st("diff_s"),
            st("win_p"), st("win_s"), st("swa_p"), st("swa_s"), st("wkv_p"), st("wkv_s"), st("sh_p"), st("sh_s"))
```

```python
import functools
import math

import numpy as np
import jax
import jax.numpy as jnp
from jax import lax
from jax.experimental import pallas as pl
from jax.experimental.pallas import tpu as pltpu

F32 = jnp.float32
BF16 = jnp.bfloat16
HI = lax.Precision.HIGHEST

HEAD_DIM = 64
NSA_KV_HEADS = 2
NSA_GROUP = 4
NSA_HEADS = NSA_KV_HEADS * NSA_GROUP
CMP_STRIDE = 16
CMP_LEN = 32
SEL_BLOCK = 64
N_SELECT = 16
NSA_WINDOW = 512
DIFF_HEADS = 4
DIFF_VDIM = 128
SWA_HEADS = 8
SWA_KV_HEADS = 2
SWA_GROUP = 4
SWA_WINDOW = 128
RWKV_N = 64
RWKV_HEADS = 8
RWKV_WIDTH = RWKV_N * RWKV_HEADS
DECAY_LORA = 32
AAA_LORA = 32
GATE_LORA = 96
LORA_PAD = 256
N_BUCKETS = 32
MAX_DISTANCE = 128
NORM_EPS = 1e-6
SUBLN_EPS = 1e-5
RWKV_GN_EPS = 64e-5
NEG = -1e30
FORCE = 1e6
REMOVED = -3e38

LANES = 128
SUBLANES = 8
VMEM_LIMIT_BYTES = 56 * 1024 * 1024

TQ = 256
FAR_TILES = 4
ROW_TILE = 512
CMP_PAD = 16

NN = (((1,), (0,)), ((), ()))
NT = (((1,), (1,)), ((), ()))
TN = (((0,), (0,)), ((), ()))


def _cparams(sem):
    return pltpu.CompilerParams(dimension_semantics=sem, vmem_limit_bytes=VMEM_LIMIT_BYTES)


def _const_spec(shape):
    n = len(shape)
    return pl.BlockSpec(shape, lambda *_: (0,) * n)


def _smem_spec():
    return pl.BlockSpec(memory_space=pltpu.SMEM)


def _dot(a, b, precision=None):
    return jnp.dot(a, b, preferred_element_type=F32, precision=precision)


def _dot_nt(a, b, precision=None):
    return lax.dot_general(a, b, NT, preferred_element_type=F32, precision=precision)


def _split(x):
    hi = x.astype(BF16)
    return hi, (x - hi.astype(F32)).astype(BF16)


def _dot3(a, b, dims=NN):
    f = lambda x, y: lax.dot_general(x, y, dims, preferred_element_type=F32)
    return f(a[0], b[0]) + (f(a[1], b[0]) + f(a[0], b[1]))


def _dot2(a, b, dims=NN):
    f = lambda x, y: lax.dot_general(x, y, dims, preferred_element_type=F32)
    return f(a[0], b) + f(a[1], b)


def _norm_proj_body(x_ref, g_ref, w_ref, *out_refs, groups):
    x = x_ref[...]
    h = (x * lax.rsqrt(jnp.mean(x * x, axis=-1, keepdims=True) + NORM_EPS)) * g_ref[...]
    hb = h.astype(BF16)
    i = 0
    for off, wd, dts in groups:
        r = _dot(hb, w_ref[:, off:off + wd])
        for dt in dts:
            out_refs[i][...] = r.astype(dt)
            i += 1


def norm_proj(x2d, g, w_bf16, groups):
    m, d = x2d.shape
    tm = min(ROW_TILE, m)
    out_shape, out_specs = [], []
    for _, wd, dts in groups:
        for dt in dts:
            out_shape.append(jax.ShapeDtypeStruct((m, wd), dt))
            out_specs.append(pl.BlockSpec((tm, wd), lambda i: (i, 0)))
    return pl.pallas_call(
        functools.partial(_norm_proj_body, groups=groups),
        out_shape=out_shape,
        grid=(m // tm,),
        in_specs=[pl.BlockSpec((tm, d), lambda i: (i, 0)), _const_spec((1, d)), _const_spec(w_bf16.shape)],
        out_specs=out_specs,
        compiler_params=_cparams(("parallel",)),
        name="norm_proj",
    )(x2d, g.reshape(1, d), w_bf16)


def _out_proj_body(x_ref, a_ref, b_ref, wa_ref, wb_ref, o_ref):
    acc = _dot(a_ref[...].astype(BF16), wa_ref[...]) + _dot(b_ref[...].astype(BF16), wb_ref[...])
    o_ref[...] = x_ref[...] + acc


def out_proj(x2d, a, b, wa, wb):
    m, d = x2d.shape
    tm = min(ROW_TILE, m)
    return pl.pallas_call(
        _out_proj_body,
        out_shape=jax.ShapeDtypeStruct((m, d), F32),
        grid=(m // tm,),
        in_specs=[pl.BlockSpec((tm, d), lambda i: (i, 0)),
                  pl.BlockSpec((tm, a.shape[1]), lambda i: (i, 0)),
                  pl.BlockSpec((tm, b.shape[1]), lambda i: (i, 0)),
                  _const_spec(wa.shape), _const_spec(wb.shape)],
        out_specs=pl.BlockSpec((tm, d), lambda i: (i, 0)),
        compiler_params=_cparams(("parallel",)),
        name="out_proj",
    )(x2d, a, b, wa, wb)


def _ffn_body(x_ref, g_ref, gf_ref, wg_ref, wu_ref, wd_ref, o_ref, h_sc, acc_sc, *, final_norm):
    f = pl.program_id(1)

    @pl.when(f == 0)
    def _():
        x = x_ref[...]
        h = (x * lax.rsqrt(jnp.mean(x * x, axis=-1, keepdims=True) + NORM_EPS)) * g_ref[...]
        h_sc[...] = h.astype(BF16)
        acc_sc[...] = jnp.zeros_like(acc_sc)

    hb = h_sc[...]
    gate = _dot(hb, wg_ref[...])
    up = _dot(hb, wu_ref[...])
    act = (gate * jax.nn.sigmoid(gate)) * up
    acc_sc[...] += _dot(act.astype(BF16), wd_ref[...])

    @pl.when(f == pl.num_programs(1) - 1)
    def _():
        y = x_ref[...] + acc_sc[...]
        if final_norm:
            y = (y * lax.rsqrt(jnp.mean(y * y, axis=-1, keepdims=True) + NORM_EPS)) * gf_ref[...]
        o_ref[...] = y


def ffn(x2d, g, g_final, wg, wu, wd, final_norm):
    m, d = x2d.shape
    dff = wg.shape[1]
    tm = min(ROW_TILE, m)
    tf = dff // 2
    return pl.pallas_call(
        functools.partial(_ffn_body, final_norm=final_norm),
        out_shape=jax.ShapeDtypeStruct((m, d), F32),
        grid=(m // tm, dff // tf),
        in_specs=[pl.BlockSpec((tm, d), lambda i, f: (i, 0)), _const_spec((1, d)), _const_spec((1, d)),
                  pl.BlockSpec((d, tf), lambda i, f: (0, f)), pl.BlockSpec((d, tf), lambda i, f: (0, f)),
                  pl.BlockSpec((tf, d), lambda i, f: (f, 0))],
        out_specs=pl.BlockSpec((tm, d), lambda i, f: (i, 0)),
        scratch_shapes=[pltpu.VMEM((tm, d), BF16), pltpu.VMEM((tm, d), F32)],
        compiler_params=_cparams(("parallel", "arbitrary")),
        name="ffn",
    )(x2d, g.reshape(1, d), g_final.reshape(1, d), wg, wu, wd)


def _t5_bucket(dist):
    n = jnp.maximum(dist, 0)
    max_exact = N_BUCKETS // 2
    nf = jnp.maximum(n, 1).astype(F32)
    large = max_exact + (jnp.log(nf / max_exact) / math.log(MAX_DISTANCE / max_exact)
                         * (N_BUCKETS - max_exact)).astype(jnp.int32)
    large = jnp.minimum(large, N_BUCKETS - 1)
    return jnp.where(n < max_exact, n, large)


def _toeplitz_bias(tab, rows, cols, offset, valid_lo, valid_hi):
    length = rows + cols - 1
    d = jnp.arange(length, dtype=jnp.int32) - (cols - 1) + offset
    g = jnp.where((d >= valid_lo) & (d <= valid_hi), tab[_t5_bucket(d)].astype(F32).T, NEG)
    h = g[:, ::-1]
    flat = jnp.tile(h, (1, rows + 1))[:, :rows * (length + 1)].reshape(-1, rows, length + 1)[:, :, :cols]
    return flat[:, ::-1, :]


def _cmp_to_sel(n_cmp, n_sel, rows, cols):
    i = np.arange(n_cmp)[:, None]
    j = np.arange(n_sel)[None, :]
    m = (i * CMP_STRIDE < (j + 1) * SEL_BLOCK) & (i * CMP_STRIDE + CMP_LEN > j * SEL_BLOCK)
    out = np.zeros((rows, cols), np.float32)
    out[CMP_PAD:CMP_PAD + n_cmp, :n_sel] = m
    return out


def _block_expand(n_chunks, n_blocks, chunk_keys):
    c = np.arange(n_chunks)[:, None, None]
    j = np.arange(n_blocks)[None, :, None]
    l = np.arange(chunk_keys)[None, None, :]
    return (j == (c * chunk_keys + l) // SEL_BLOCK).astype(np.float32)


def _place_half(x64, half):
    z = jnp.zeros_like(x64)
    return jnp.concatenate([x64, z], axis=1) if half == 0 else jnp.concatenate([z, x64], axis=1)


def _lane_tile(x, width):
    reps = width // x.shape[1]
    return x if reps == 1 else jnp.concatenate([x] * reps, axis=1)


def _masked_softmax_parts(parts):
    m = None
    for s, msk in parts:
        mm = jnp.max(jnp.where(msk, s, NEG), axis=-1, keepdims=True)
        m = mm if m is None else jnp.maximum(m, mm)
    ps, den = [], None
    for s, msk in parts:
        p = jnp.where(msk, jnp.exp(jnp.where(msk, s, NEG) - m), 0.0)
        ps.append(p)
        d = jnp.sum(p, axis=-1, keepdims=True)
        den = d if den is None else den + d
    inv = 1.0 / jnp.maximum(den, 1e-30)
    return [p * inv for p in ps]


def _select_blocks(score, n_top):
    lane = lax.broadcasted_iota(jnp.int32, score.shape, 1)
    big = score.shape[1]
    sel = jnp.zeros(score.shape, F32)
    sc = score
    for _ in range(n_top):
        m = jnp.max(sc, axis=-1, keepdims=True)
        idx = jnp.min(jnp.where(sc == m, lane, big), axis=-1, keepdims=True)
        hit = lane == idx
        sel = jnp.where(hit & (m > 0.5 * NEG), 1.0, sel)
        sc = jnp.where(hit, REMOVED, sc)
    return sel


def _select_blocks_few_rows(score, n_top):
    r, nb = score.shape
    by_block = jnp.concatenate([score, jnp.zeros((LANES - r, nb), F32)], axis=0).T
    first = lax.broadcasted_iota(jnp.int32, (nb, nb), 0)
    second = lax.broadcasted_iota(jnp.int32, (nb, nb), 1)
    sels = []
    for i in range(r):
        a = by_block[:, i:i + 1]
        b = score[i:i + 1, :]
        precedes = (a > b) | ((a == b) & (first < second))
        rank = jnp.sum(jnp.where(precedes, 1.0, 0.0), axis=0, keepdims=True)
        sels.append(jnp.where((rank < n_top) & (b > 0.5 * NEG), 1.0, 0.0))
    return jnp.concatenate(sels, axis=0)


def _block_scores(imp, q_pos):
    j = lax.broadcasted_iota(jnp.int32, imp.shape, 1)
    cur = q_pos // SEL_BLOCK
    avail = j * SEL_BLOCK <= q_pos
    forced = (j == 0) | (j == cur) | (j == cur - 1)
    return jnp.where(avail, jnp.where(forced, FORCE, imp), NEG)


def _flash_init(h, m_sc, l_sc, acc_sc):
    m_sc[h] = jnp.full(m_sc.shape[1:], REMOVED, F32)
    l_sc[h] = jnp.zeros(l_sc.shape[1:], F32)
    acc_sc[h] = jnp.zeros(acc_sc.shape[1:], F32)


def _flash_update(h, s, vblk, m_sc, l_sc, acc_sc, shift=None):
    m_prev = m_sc[h]
    m_cur = jnp.max(s, axis=-1, keepdims=True)
    if shift is not None:
        m_cur = m_cur + shift
    m_next = jnp.maximum(m_prev, m_cur)
    alpha = jnp.exp(m_prev - m_next)
    sub = m_next if shift is None else m_next - shift
    p = jnp.exp(s - _lane_tile(sub, s.shape[1]))
    l_sc[h] = alpha * l_sc[h] + jnp.sum(p, axis=-1, keepdims=True)
    acc_sc[h] = alpha * acc_sc[h] + _dot(p.astype(BF16), vblk)
    m_sc[h] = m_next


def _causal_tiles(qi, step):
    n_far = jnp.maximum(qi - 1, 0)
    n_big = n_far // FAR_TILES

    def big_body(kb, carry):
        step(pl.multiple_of(kb * (FAR_TILES * TQ), FAR_TILES * TQ), FAR_TILES * TQ, None, kb)
        return carry

    def far_body(kt, carry):
        step(pl.multiple_of(kt * TQ, TQ), TQ, None, kt)
        return carry

    lax.fori_loop(0, n_big, big_body, 0)
    lax.fori_loop(n_big * FAR_TILES, n_far, far_body, 0)

    @pl.when(qi >= 1)
    def _():
        step(pl.multiple_of((qi - 1) * TQ, TQ), TQ, 0, qi - 1)

    step(pl.multiple_of(qi * TQ, TQ), TQ, 1, qi)


def _compress_body(x_ref, w1_ref, w2_ref, o_ref, *, nblk):
    x = x_ref[0].reshape(nblk, CMP_STRIDE, LANES)
    p1 = (x * w1_ref[0][None]).sum(axis=1)
    p2 = (x * w2_ref[0][None]).sum(axis=1)
    ck = p1 + pltpu.roll(p2, nblk - 1, 0)
    row = lax.broadcasted_iota(jnp.int32, ck.shape, 0)
    o_ref[0, 0] = jnp.zeros(o_ref.shape[2:], F32)
    o_ref[0, 0, CMP_PAD:CMP_PAD + nblk, :] = jnp.where(row < nblk - 1, ck, 0.0)


def compress_prompt(kv, w1, w2, cp):
    b, t, _ = kv.shape
    nblk = t // CMP_STRIDE
    return pl.pallas_call(
        functools.partial(_compress_body, nblk=nblk),
        out_shape=jax.ShapeDtypeStruct((b, 2, cp, LANES), F32),
        grid=(b, 2),
        in_specs=[pl.BlockSpec((1, t, LANES), lambda i, k: (i, 0, k)),
                  pl.BlockSpec((1, CMP_STRIDE, LANES), lambda i, k: (k, 0, 0)),
                  pl.BlockSpec((1, CMP_STRIDE, LANES), lambda i, k: (k, 0, 0))],
        out_specs=pl.BlockSpec((1, 1, cp, LANES), lambda i, k: (i, k, 0, 0)),
        compiler_params=_cparams(("parallel", "parallel")),
        name="compress_prompt",
    )(kv, w1, w2)


def _nsa_prompt_body(c31_ref, q_ref, gl_ref, ck_ref, cv_ref, c2s_ref, selk_ref, selv_ref,
                     wk0_ref, wk1_ref, wk2_ref, wv0_ref, wv1_ref, wv2_ref,
                     selb_ref, winb_ref, cnear_ref, e3_ref, e3big_ref, o_ref, m_sc, l_sc, acc_sc, *, cp):
    qi = pl.program_id(1)
    tq = q_ref.shape[1]
    near = tq // CMP_STRIDE + CMP_PAD
    near0 = pl.multiple_of(qi * (tq // CMP_STRIDE), tq // CMP_STRIDE)
    q = q_ref[0] * (HEAD_DIM ** -0.5)
    gates = jax.nn.sigmoid(gl_ref[0])
    n_far = cp - LANES
    ck = _split(ck_ref[0, 0, CMP_PAD:CMP_PAD + n_far, :])
    cv = cv_ref[0, 0, CMP_PAD:CMP_PAD + n_far, :].astype(BF16)
    ck_near = _split(ck_ref[0, 0, pl.ds(near0, near), :])
    cv_near = cv_ref[0, 0, pl.ds(near0, near), :].astype(BF16)
    c2s = c2s_ref[CMP_PAD:CMP_PAD + n_far, :].astype(BF16)
    c2s_near = c2s_ref[pl.ds(near0, near), :].astype(BF16)
    kcat = jnp.concatenate([wk0_ref[0], wk1_ref[0], wk2_ref[0]], axis=0)
    vcat = jnp.concatenate([wv0_ref[0], wv1_ref[0], wv2_ref[0]], axis=0)
    q_pos = qi * tq + lax.broadcasted_iota(jnp.int32, (tq, 1), 0)
    far_mask = lax.broadcasted_iota(jnp.int32, (tq, n_far), 1) < near0 - CMP_PAD
    near_lane = lax.broadcasted_iota(jnp.int32, (tq, near), 1)
    near_ok = (near_lane >= CMP_PAD) | (qi > 0)
    wcol = lax.broadcasted_iota(jnp.int32, (tq, 3 * tq), 1)
    win_ok = wcol >= (2 - qi) * tq

    qzb_all, o_cmp_all, imps = [], [], []
    for g in range(NSA_KV_HEADS):
        psum_far = jnp.zeros((tq, n_far), F32)
        psum_near = jnp.zeros((tq, near), F32)
        for h in range(NSA_GROUP):
            hh = g * NSA_GROUP + h
            qz = _split(_place_half(q[:, hh * HEAD_DIM:(hh + 1) * HEAD_DIM], g))
            qzb_all.append(qz[0])
            s_far = _dot3(qz, ck, NT) + c31_ref[hh]
            nb = cnear_ref[hh]
            s_near = _dot3(qz, ck_near, NT) + nb
            p_far, p_near = _masked_softmax_parts([(s_far, far_mask), (s_near, (nb > 0.5 * NEG) & near_ok)])
            o_cmp_all.append(_dot(p_far.astype(BF16), cv) + _dot(p_near.astype(BF16), cv_near))
            psum_far = psum_far + p_far
            psum_near = psum_near + p_near
        imps.append(_dot2(_split(psum_far), c2s) + _dot2(_split(psum_near), c2s_near))
    sel_all = _select_blocks(_block_scores(jnp.concatenate(imps, axis=0), jnp.concatenate([q_pos] * len(imps), axis=0)),
                             N_SELECT).astype(BF16)

    pieces = []
    for g in range(NSA_KV_HEADS):
        qzb = qzb_all[g * NSA_GROUP:(g + 1) * NSA_GROUP]
        o_cmp = o_cmp_all[g * NSA_GROUP:(g + 1) * NSA_GROUP]
        sel = sel_all[g * tq:(g + 1) * tq]

        for h in range(NSA_GROUP):
            _flash_init(h, m_sc, l_sc, acc_sc)

        def sel_step(k0, size, kind, tile, g=g, qzb=qzb, sel=sel):
            kblk = selk_ref[0, pl.ds(k0, size), :]
            vblk = selv_ref[0, pl.ds(k0, size), :]
            expand = e3big_ref[tile] if size != tq else e3_ref[tile]
            madd = (_dot(sel, expand) - 1.0) * 1e30
            for h in range(NSA_GROUP):
                hh = g * NSA_GROUP + h
                s = _dot_nt(qzb[h], kblk)
                if kind is None:
                    _flash_update(h, s + madd, vblk, m_sc, l_sc, acc_sc, shift=c31_ref[hh])
                else:
                    _flash_update(h, s + (madd + selb_ref[hh, kind]), vblk, m_sc, l_sc, acc_sc)

        _causal_tiles(qi, sel_step)

        for h in range(NSA_GROUP):
            hh = g * NSA_GROUP + h
            wb = winb_ref[hh]
            s = _dot_nt(qzb[h], kcat) + wb
            msk = (wb > 0.5 * NEG) & win_ok
            (p,) = _masked_softmax_parts([(s, msk)])
            o_win = _dot(p.astype(BF16), vcat)
            o_sel = acc_sc[h] / l_sc[h]
            c = g * NSA_GROUP * 3 + h * 3
            o = gates[:, c:c + 1] * o_cmp[h] + gates[:, c + 1:c + 2] * o_sel + gates[:, c + 2:c + 3] * o_win
            pieces.append(o[:, g * HEAD_DIM:(g + 1) * HEAD_DIM])
    o_ref[0] = jnp.concatenate(pieces, axis=1)


def nsa_prompt(q, gl, ckv, kvb, kvwb, tabs):
    b, t, _ = q.shape
    nq = t // TQ
    cp = ckv.shape[2]

    def win_spec(back, col):
        return pl.BlockSpec((1, TQ, LANES), lambda i, j: (i, jnp.maximum(j - back, 0), col))

    return pl.pallas_call(
        functools.partial(_nsa_prompt_body, cp=cp),
        out_shape=jax.ShapeDtypeStruct((b, t, NSA_HEADS * HEAD_DIM), F32),
        grid=(b, nq),
        in_specs=[_smem_spec(),
                  pl.BlockSpec((1, TQ, NSA_HEADS * HEAD_DIM), lambda i, j: (i, j, 0)),
                  pl.BlockSpec((1, TQ, LANES), lambda i, j: (i, j, 0)),
                  pl.BlockSpec((1, 1, cp, LANES), lambda i, j: (i, 0, 0, 0)),
                  pl.BlockSpec((1, 1, cp, LANES), lambda i, j: (i, 1, 0, 0)),
                  _const_spec(tabs["c2s"].shape),
                  pl.BlockSpec((1, t, LANES), lambda i, j: (i, 0, 2)),
                  pl.BlockSpec((1, t, LANES), lambda i, j: (i, 0, 3)),
                  win_spec(2, 0), win_spec(1, 0), win_spec(0, 0),
                  win_spec(2, 1), win_spec(1, 1), win_spec(0, 1),
                  _const_spec(tabs["selb"].shape), _const_spec(tabs["winb"].shape),
                  _const_spec(tabs["cnear"].shape), _const_spec(tabs["e3"].shape),
                  _const_spec(tabs["e3big"].shape)],
        out_specs=pl.BlockSpec((1, TQ, NSA_HEADS * HEAD_DIM), lambda i, j: (i, j, 0)),
        scratch_shapes=[pltpu.VMEM((NSA_GROUP, TQ, LANES), F32), pltpu.VMEM((NSA_GROUP, TQ, LANES), F32),
                        pltpu.VMEM((NSA_GROUP, TQ, LANES), F32)],
        compiler_params=_cparams(("parallel", "parallel")),
        name="nsa_prompt",
    )(tabs["c31"], q, gl, ckv, ckv, tabs["c2s"], kvb, kvb, kvwb, kvwb, kvwb, kvwb, kvwb, kvwb,
      tabs["selb"], tabs["winb"], tabs["cnear"], tabs["e3"], tabs["e3big"])


def _prev_diag_bias(tab):
    big = 1 << 30
    return jnp.stack([_toeplitz_bias(tab, TQ, TQ, TQ, 0, big), _toeplitz_bias(tab, TQ, TQ, 0, 0, big)], axis=1)


def nsa_prompt_tables(tab_n, t):
    nq = t // TQ
    cp = t // CMP_STRIDE + LANES
    near = TQ // CMP_STRIDE + CMP_PAD
    cnear = _toeplitz_bias(tab_n, TQ, near * CMP_STRIDE, CMP_PAD * CMP_STRIDE - (CMP_LEN - 1), 0, 1 << 30)
    return dict(
        c31=tab_n[N_BUCKETS - 1].astype(F32),
        selb=_prev_diag_bias(tab_n),
        winb=_toeplitz_bias(tab_n, TQ, 3 * TQ, 2 * TQ, 0, NSA_WINDOW - 1),
        cnear=cnear[:, :, ::CMP_STRIDE],
        c2s=jnp.asarray(_cmp_to_sel(t // CMP_STRIDE - 1, t // SEL_BLOCK, cp, LANES)),
        e3=jnp.asarray(_block_expand(nq, LANES, TQ), BF16),
        e3big=jnp.asarray(_block_expand(max(nq // FAR_TILES, 1), LANES, FAR_TILES * TQ), BF16),
    )


DIFF_HEADS_PER_STEP = 2


def _diff_prompt_body(sc_ref, q_ref, k_ref, v_ref, bias_ref, sub_ref, o_ref, m_sc, l_sc, acc_sc):
    hp = pl.program_id(1)
    qi = pl.program_id(2)
    tq = q_ref.shape[1]
    nh = DIFF_HEADS_PER_STEP
    q = q_ref[0] * (HEAD_DIM ** -0.5)
    q2, c31 = [], []
    for h in range(nh):
        qh = q[:, h * LANES:(h + 1) * LANES]
        lane = lax.broadcasted_iota(jnp.int32, qh.shape, 1)
        q2.append(jnp.concatenate([jnp.where(lane < HEAD_DIM, qh, 0.0), jnp.where(lane >= HEAD_DIM, qh, 0.0)],
                                  axis=0).astype(BF16))
        c31.append(sc_ref[2 + hp * nh + h])
        _flash_init(h, m_sc, l_sc, acc_sc)

    def step(k0, size, kind, tile):
        del tile
        for h in range(nh):
            s = _dot_nt(q2[h], k_ref[0, pl.ds(k0, size), h * LANES:(h + 1) * LANES])
            vblk = v_ref[0, pl.ds(k0, size), h * LANES:(h + 1) * LANES]
            if kind is None:
                _flash_update(h, s, vblk, m_sc, l_sc, acc_sc, shift=c31[h])
            else:
                bt = bias_ref[h, kind]
                _flash_update(h, s + jnp.concatenate([bt, bt], axis=0), vblk, m_sc, l_sc, acc_sc)

    _causal_tiles(qi, step)
    outs = []
    for h in range(nh):
        o12 = acc_sc[h] / l_sc[h]
        o = o12[:tq] - sc_ref[0] * o12[tq:]
        y = (o * lax.rsqrt(jnp.mean(o * o, axis=-1, keepdims=True) + SUBLN_EPS)) * sub_ref[...]
        outs.append(y * sc_ref[1])
    o_ref[0] = jnp.concatenate(outs, axis=1)


def diff_prompt(dq, dkvb, scal, bias, subln):
    b, t, _ = dq.shape
    nq = t // TQ
    nh = DIFF_HEADS_PER_STEP
    wide = nh * LANES
    return pl.pallas_call(
        _diff_prompt_body,
        out_shape=jax.ShapeDtypeStruct((b, t, DIFF_HEADS * DIFF_VDIM), F32),
        grid=(b, DIFF_HEADS // nh, nq),
        in_specs=[_smem_spec(),
                  pl.BlockSpec((1, TQ, wide), lambda i, h, j: (i, j, h)),
                  pl.BlockSpec((1, t, wide), lambda i, h, j: (i, 0, h)),
                  pl.BlockSpec((1, t, wide), lambda i, h, j: (i, 0, DIFF_HEADS // nh + h)),
                  pl.BlockSpec((nh, 2, TQ, TQ), lambda i, h, j: (h, 0, 0, 0)),
                  _const_spec((1, DIFF_VDIM))],
        out_specs=pl.BlockSpec((1, TQ, wide), lambda i, h, j: (i, j, h)),
        scratch_shapes=[pltpu.VMEM((nh, 2 * TQ, LANES), F32), pltpu.VMEM((nh, 2 * TQ, LANES), F32),
                        pltpu.VMEM((nh, 2 * TQ, LANES), F32)],
        compiler_params=_cparams(("parallel", "parallel", "parallel")),
        name="diff_prompt",
    )(scal, dq, dkvb, dkvb, bias, subln.reshape(1, DIFF_VDIM))


NSA_PAGES_PER_STEP = 32
DIFF_PAGES_PER_STEP = 16
NEW_ROWS = 8


def _pad_rows(x, rows):
    return jnp.concatenate([x, jnp.zeros((rows - x.shape[0], x.shape[1]), x.dtype)], axis=0)


def _nsa_dec_body(pt_ref, *refs, n_pg, n_ch, past, cpd, n_tok):
    del pt_ref
    pages = refs[:n_pg]
    (qz_ref, glr_ref, newkv_ref, wbuf_ref, neww_ref, w1_ref, w2_ref, c2s_ref, cmpb_ref, selnear_ref,
     winb_ref, c31_ref, e3_ref, rsum_ref, rexp_ref, o_ref, p1k, p2k, p1v, p2v, sc_sc, vt_sc) = refs[n_pg:]
    c = pl.program_id(1)
    page_rows = pages[0].shape[2]
    blk_per_page = page_rows // CMP_STRIDE
    qz = qz_ref[0] * (HEAD_DIM ** -0.5)
    qzb = qz.astype(BF16)

    @pl.when(c == 0)
    def _():
        for ref in (p1k, p2k, p1v, p2v):
            ref[...] = jnp.zeros(ref.shape, F32)

    for i in range(n_pg):
        page = pages[i][0]
        pg = c * n_pg + i
        blk0 = pl.multiple_of(CMP_PAD + pg * blk_per_page, SUBLANES)
        xk = page[0:LANES, :].T.reshape(blk_per_page, CMP_STRIDE, LANES)
        xv = page[LANES:2 * LANES, :].T.reshape(blk_per_page, CMP_STRIDE, LANES)
        p1k[pl.ds(blk0, blk_per_page), :] = (xk * w1_ref[0][None]).sum(axis=1)
        p2k[pl.ds(blk0, blk_per_page), :] = (xk * w2_ref[0][None]).sum(axis=1)
        p1v[pl.ds(blk0, blk_per_page), :] = (xv * w1_ref[1][None]).sum(axis=1)
        p2v[pl.ds(blk0, blk_per_page), :] = (xv * w2_ref[1][None]).sum(axis=1)
        sc_sc[c, :, i * page_rows:(i + 1) * page_rows] = _dot(qzb, page[2 * LANES:3 * LANES, :].astype(BF16))
        vt_sc[c, :, i * page_rows:(i + 1) * page_rows] = page[3 * LANES:4 * LANES, :].astype(BF16)

    @pl.when(c == n_ch - 1)
    def _():
        nk = newkv_ref[0]
        s_new = _dot_nt(qzb, _pad_rows(nk[:, 2 * LANES:3 * LANES], LANES).astype(BF16))
        v_new = _pad_rows(nk[:, 3 * LANES:4 * LANES], LANES).astype(BF16)

        ck = p1k[...] + pltpu.roll(p2k[...], cpd - 1, 0)
        cv = p1v[...] + pltpu.roll(p2v[...], cpd - 1, 0)
        cb = cmpb_ref[...]
        (p_cmp,) = _masked_softmax_parts([(_dot3(_split(qz), _split(ck), NT) + cb, cb > 0.5 * NEG)])
        o_cmp = _dot(p_cmp.astype(BF16), cv.astype(BF16))
        p_hi, p_lo = _split(p_cmp)
        head_sum = rsum_ref[...].astype(BF16)
        imp = _dot2(_split(_dot(head_sum, p_hi) + _dot(head_sum, p_lo)), c2s_ref[...].astype(BF16))
        rows = lax.broadcasted_iota(jnp.int32, (imp.shape[0], 1), 0)
        sel = _select_blocks_few_rows(_block_scores(imp, past + rows % n_tok), N_SELECT)
        sel = _dot(rexp_ref[...], sel).astype(BF16)

        c31 = c31_ref[...][:, :1]
        selnear = selnear_ref[...]
        chunk = n_pg * page_rows
        bpc = chunk // SEL_BLOCK
        expand = e3_ref[...]
        parts = []
        for ci in range(n_ch):
            s = sc_sc[ci] + (_dot(sel[:, ci * bpc:(ci + 1) * bpc], expand) - 1.0) * 1e30
            if ci < n_ch - 1:
                s = s + c31
            else:
                s = s + jnp.concatenate([jnp.broadcast_to(c31, (s.shape[0], chunk - LANES)), selnear[:, :LANES]], axis=1)
            parts.append(s)
        parts.append(s_new + (_dot(sel[:, n_ch * bpc:(n_ch + 1) * bpc], expand[:, :LANES]) - 1.0) * 1e30 + selnear[:, LANES:])
        m = parts[0].max(axis=-1, keepdims=True)
        for s in parts[1:]:
            m = jnp.maximum(m, s.max(axis=-1, keepdims=True))
        den = jnp.zeros_like(m)
        acc = jnp.zeros((m.shape[0], LANES), F32)
        for ci, s in enumerate(parts):
            p = jnp.exp(s - m)
            den = den + p.sum(axis=-1, keepdims=True)
            if ci < n_ch:
                acc = acc + _dot_nt(p.astype(BF16), vt_sc[ci])
            else:
                acc = acc + _dot(p.astype(BF16), v_new)
        o_sel = acc / den

        wb = wbuf_ref[0]
        nw = neww_ref[0]
        kcat = jnp.concatenate([wb[:, :LANES], _pad_rows(nw[:, :LANES], LANES)], axis=0).astype(BF16)
        vcat = jnp.concatenate([wb[:, LANES:], _pad_rows(nw[:, LANES:], LANES)], axis=0).astype(BF16)
        wbias = winb_ref[...]
        (p_win,) = _masked_softmax_parts([(_dot_nt(qzb, kcat) + wbias, wbias > 0.5 * NEG)])
        o_win = _dot(p_win.astype(BF16), vcat)
        gates = jax.nn.sigmoid(glr_ref[0])
        o_ref[0] = gates[:, 0:1] * o_cmp + gates[:, 1:2] * o_sel + gates[:, 2:3] * o_win


def _row_tables(table, head, tok):
    return table[head, tok]


def nsa_decode_tables(tab_n, past, n_tok, wlen, n_ch, n_pg, page_rows):
    rows = NSA_KV_HEADS * n_tok * NSA_GROUP
    g = np.arange(rows) // (n_tok * NSA_GROUP)
    t = (np.arange(rows) // NSA_GROUP) % n_tok
    head = g * NSA_GROUP + np.arange(rows) % NSA_GROUP
    cpd = past // CMP_STRIDE + LANES
    n_sel_pad = 2 * LANES
    big = 1 << 30
    cmpb = _toeplitz_bias(tab_n, n_tok, cpd * CMP_STRIDE, past + CMP_PAD * CMP_STRIDE - (CMP_LEN - 1), 0, big)
    cmpb = _row_tables(cmpb[:, :, ::CMP_STRIDE], head, t)
    valid_cp = (np.arange(cpd) >= CMP_PAD)[None, :]
    cmpb = jnp.where(valid_cp, cmpb, NEG)
    selnear = _row_tables(_toeplitz_bias(tab_n, n_tok, 2 * LANES, LANES, 0, big), head, t)
    winb = _row_tables(_toeplitz_bias(tab_n, n_tok, wlen + LANES, wlen, 0, NSA_WINDOW - 1), head, t)
    c31 = jnp.broadcast_to(tab_n[N_BUCKETS - 1, head].astype(F32)[:, None], (rows, LANES))
    n_cmp = (past + SEL_BLOCK) // CMP_STRIDE - 1
    n_sel = (past + SEL_BLOCK) // SEL_BLOCK
    rsum = (np.arange(rows)[None, :] // NSA_GROUP == np.arange(rows // NSA_GROUP)[:, None]).astype(np.float32)
    return dict(cmpb=cmpb, selnear=selnear, winb=winb, c31=c31,
                c2s=jnp.asarray(_cmp_to_sel(min(n_cmp, cpd - CMP_PAD), n_sel, cpd, n_sel_pad)),
                e3=jnp.asarray(_block_expand(1, n_pg * page_rows // SEL_BLOCK, n_pg * page_rows)[0], BF16),
                rsum=jnp.asarray(rsum), rexp=jnp.asarray(rsum.T))


def nsa_decode(page_table, pool_t, qz, glr, newkv, wbuf, neww, w1, w2, tabs, n_tok):
    s, n_pages = page_table.shape
    page_rows = pool_t.shape[2]
    n_pg = min(NSA_PAGES_PER_STEP, n_pages)
    n_ch = n_pages // n_pg
    past = n_pages * page_rows
    cpd = tabs["cmpb"].shape[1]
    rows = qz.shape[1]

    def page_spec(i):
        return pl.BlockSpec((1, pool_t.shape[1], page_rows), lambda b, c, pt: (pt[b, c * n_pg + i], 0, 0))

    def seq_spec(a):
        return pl.BlockSpec((1,) + a.shape[1:], lambda b, c, pt: (b,) + (0,) * (a.ndim - 1))

    def const(a):
        n = a.ndim
        return pl.BlockSpec(a.shape, lambda b, c, pt: (0,) * n)

    consts = [w1, w2, tabs["c2s"], tabs["cmpb"], tabs["selnear"], tabs["winb"], tabs["c31"], tabs["e3"],
              tabs["rsum"], tabs["rexp"]]
    seqs = [qz, glr, newkv, wbuf, neww]
    grid_spec = pltpu.PrefetchScalarGridSpec(
        num_scalar_prefetch=1,
        grid=(s, n_ch),
        in_specs=[page_spec(i) for i in range(n_pg)] + [seq_spec(a) for a in seqs] + [const(a) for a in consts],
        out_specs=pl.BlockSpec((1, rows, LANES), lambda b, c, pt: (b, 0, 0)),
        scratch_shapes=[pltpu.VMEM((cpd, LANES), F32)] * 4
        + [pltpu.VMEM((n_ch, rows, n_pg * page_rows), F32), pltpu.VMEM((n_ch, LANES, n_pg * page_rows), BF16)],
    )
    return pl.pallas_call(
        functools.partial(_nsa_dec_body, n_pg=n_pg, n_ch=n_ch, past=past, cpd=cpd, n_tok=n_tok),
        out_shape=jax.ShapeDtypeStruct((s, rows, LANES), F32),
        grid_spec=grid_spec,
        compiler_params=_cparams(("parallel", "arbitrary")),
        name="nsa_decode",
    )(page_table, *([pool_t] * n_pg), *seqs, *consts)


def _diff_dec_body(pt_ref, *refs, n_pg, n_ch, past, n_tok, page_rows):
    del pt_ref
    pages = refs[:n_pg]
    sc_ref, qd_ref, newd_ref, near_ref, sub_ref, o_ref, s_sc, v_sc = refs[n_pg:]
    c = pl.program_id(1)
    per_pos = 2 * DIFF_HEADS
    qd = (qd_ref[0] * (HEAD_DIM ** -0.5)).astype(BF16)
    for i in range(n_pg):
        pg = c * n_pg + i
        for h in range(DIFF_HEADS):
            kh = pages[i][0, pl.ds(h, page_rows, stride=per_pos), :]
            vh = pages[i][0, pl.ds(DIFF_HEADS + h, page_rows, stride=per_pos), :]
            s_sc[c, h, :, i * page_rows:(i + 1) * page_rows] = _dot_nt(qd[h], kh.astype(BF16))
            v_sc[h, pl.ds(pl.multiple_of(pg * page_rows, page_rows), page_rows), :] = vh.astype(BF16)

    @pl.when(c == n_ch - 1)
    def _():
        nd = newd_ref[0]
        chunk = n_pg * page_rows
        for h in range(DIFF_HEADS):
            near = near_ref[h]
            c31 = sc_ref[2 + h]
            s_new = _dot_nt(qd[h], _pad_rows(nd[:, h * LANES:(h + 1) * LANES], LANES).astype(BF16)) + near[:, LANES:]
            v_sc[h, pl.ds(past, LANES), :] = _pad_rows(nd[:, (DIFF_HEADS + h) * LANES:(DIFF_HEADS + h + 1) * LANES], LANES).astype(BF16)
            parts = []
            for ci in range(n_ch):
                s = s_sc[ci, h]
                if ci < n_ch - 1:
                    s = s + c31
                else:
                    s = s + jnp.concatenate([jnp.full((s.shape[0], chunk - LANES), c31, F32), near[:, :LANES]], axis=1)
                parts.append(s)
            parts.append(s_new)
            m = parts[0].max(axis=-1, keepdims=True)
            for s in parts[1:]:
                m = jnp.maximum(m, s.max(axis=-1, keepdims=True))
            den = jnp.zeros_like(m)
            acc = jnp.zeros((m.shape[0], LANES), F32)
            for ci, s in enumerate(parts):
                p = jnp.exp(s - m)
                den = den + p.sum(axis=-1, keepdims=True)
                acc = acc + _dot(p.astype(BF16), v_sc[h, ci * chunk:ci * chunk + s.shape[1], :])
            o12 = acc / den
            o = o12[:n_tok] - sc_ref[0] * o12[n_tok:]
            y = (o * lax.rsqrt(jnp.mean(o * o, axis=-1, keepdims=True) + SUBLN_EPS)) * sub_ref[...]
            o_ref[0, h] = y * sc_ref[1]


def diff_decode_tables(tab_d, n_tok):
    near = _toeplitz_bias(tab_d, n_tok, 2 * LANES, LANES, 0, 1 << 30)
    return jnp.concatenate([near, near], axis=1)


def diff_decode(page_table, pool_rows, qd, newd, scal, near, subln, n_tok, page_rows):
    s, n_pages = page_table.shape
    n_pg = min(DIFF_PAGES_PER_STEP, n_pages)
    n_ch = n_pages // n_pg
    past = n_pages * page_rows

    def page_spec(i):
        return pl.BlockSpec((1,) + pool_rows.shape[1:], lambda b, c, pt: (pt[b, c * n_pg + i], 0, 0))

    grid_spec = pltpu.PrefetchScalarGridSpec(
        num_scalar_prefetch=1,
        grid=(s, n_ch),
        in_specs=[page_spec(i) for i in range(n_pg)] + [
            pl.BlockSpec(memory_space=pltpu.SMEM),
            pl.BlockSpec((1,) + qd.shape[1:], lambda b, c, pt: (b, 0, 0, 0)),
            pl.BlockSpec((1,) + newd.shape[1:], lambda b, c, pt: (b, 0, 0)),
            pl.BlockSpec(near.shape, lambda b, c, pt: (0, 0, 0)),
            pl.BlockSpec((1, DIFF_VDIM), lambda b, c, pt: (0, 0))],
        out_specs=pl.BlockSpec((1, DIFF_HEADS, n_tok, LANES), lambda b, c, pt: (b, 0, 0, 0)),
        scratch_shapes=[pltpu.VMEM((n_ch, DIFF_HEADS, 2 * n_tok, n_pg * page_rows), F32),
                        pltpu.VMEM((DIFF_HEADS, past + LANES, LANES), BF16)],
    )
    return pl.pallas_call(
        functools.partial(_diff_dec_body, n_pg=n_pg, n_ch=n_ch, past=past, n_tok=n_tok, page_rows=page_rows),
        out_shape=jax.ShapeDtypeStruct((s, DIFF_HEADS, n_tok, LANES), F32),
        grid_spec=grid_spec,
        compiler_params=_cparams(("parallel", "arbitrary")),
        name="diff_decode",
    )(page_table, *([pool_rows] * n_pg), scal, qd, newd, near, subln.reshape(1, DIFF_VDIM))


def _sink_attention(qzb, kcat, vcat, bias, ok, sink):
    s = _dot_nt(qzb, kcat) + bias
    msk = (bias > 0.5 * NEG) & ok
    sm = jnp.where(msk, s, NEG)
    m = jnp.maximum(jnp.max(sm, axis=-1, keepdims=True), sink)
    p = jnp.where(msk, jnp.exp(sm - m), 0.0)
    den = jnp.sum(p, axis=-1, keepdims=True) + jnp.exp(sink - m)
    p = p / jnp.maximum(den, 1e-30)
    return _dot(p.astype(BF16), vcat)


def _swa_prompt_body(sink_ref, q_ref, kp_ref, kc_ref, vp_ref, vc_ref, bias_ref, o_ref):
    qi = pl.program_id(1)
    tq = q_ref.shape[1]
    q = q_ref[0] * (HEAD_DIM ** -0.5)
    kcat = jnp.concatenate([kp_ref[0], kc_ref[0]], axis=0)
    vcat = jnp.concatenate([vp_ref[0], vc_ref[0]], axis=0)
    col = lax.broadcasted_iota(jnp.int32, (tq, kcat.shape[0]), 1)
    ok = (col >= SWA_WINDOW) | (qi > 0)
    pieces = []
    for g in range(SWA_KV_HEADS):
        for h in range(SWA_GROUP):
            hh = g * SWA_GROUP + h
            qzb = _place_half(q[:, hh * HEAD_DIM:(hh + 1) * HEAD_DIM], g).astype(BF16)
            o = _sink_attention(qzb, kcat, vcat, bias_ref[hh], ok, sink_ref[hh])
            pieces.append(o[:, g * HEAD_DIM:(g + 1) * HEAD_DIM])
    o_ref[0] = jnp.concatenate(pieces, axis=1)


def swa_prompt(q, kvb, sinks, tab_s):
    b, t, _ = q.shape
    bias = _toeplitz_bias(tab_s, TQ, SWA_WINDOW + TQ, SWA_WINDOW, 0, SWA_WINDOW - 1)
    per = TQ // SWA_WINDOW

    def prev_spec(col):
        return pl.BlockSpec((1, SWA_WINDOW, LANES), lambda i, j: (i, jnp.maximum(per * j - 1, 0), col))

    def cur_spec(col):
        return pl.BlockSpec((1, TQ, LANES), lambda i, j: (i, j, col))

    return pl.pallas_call(
        _swa_prompt_body,
        out_shape=jax.ShapeDtypeStruct((b, t, SWA_HEADS * HEAD_DIM), F32),
        grid=(b, t // TQ),
        in_specs=[_smem_spec(), pl.BlockSpec((1, TQ, SWA_HEADS * HEAD_DIM), lambda i, j: (i, j, 0)),
                  prev_spec(0), cur_spec(0), prev_spec(1), cur_spec(1), _const_spec(bias.shape)],
        out_specs=pl.BlockSpec((1, TQ, SWA_HEADS * HEAD_DIM), lambda i, j: (i, j, 0)),
        compiler_params=_cparams(("parallel", "parallel")),
        name="swa_prompt",
    )(sinks.astype(F32), q, kvb, kvb, kvb, kvb, bias)


def _swa_sample_body(qz_ref, buf_ref, new_ref, bias_ref, sink_ref, o_ref):
    qzb = (qz_ref[0] * (HEAD_DIM ** -0.5)).astype(BF16)
    buf = buf_ref[0]
    new = new_ref[0]
    kcat = jnp.concatenate([buf[:, :LANES], _pad_rows(new[:, :LANES], LANES)], axis=0).astype(BF16)
    vcat = jnp.concatenate([buf[:, LANES:], _pad_rows(new[:, LANES:], LANES)], axis=0).astype(BF16)
    o_ref[0] = _sink_attention(qzb, kcat, vcat, bias_ref[...], True, sink_ref[...][:, :1])


def swa_sample(qz, buf, new, sinks, tab_s, n_tok):
    s, rows, _ = qz.shape
    wlen = buf.shape[1]
    g = np.arange(rows) // (n_tok * SWA_GROUP)
    t = (np.arange(rows) // SWA_GROUP) % n_tok
    head = g * SWA_GROUP + np.arange(rows) % SWA_GROUP
    bias = _row_tables(_toeplitz_bias(tab_s, n_tok, wlen + LANES, wlen, 0, SWA_WINDOW - 1), head, t)
    sink_rows = jnp.broadcast_to(sinks.astype(F32)[head][:, None], (rows, LANES))
    return pl.pallas_call(
        _swa_sample_body,
        out_shape=jax.ShapeDtypeStruct((s, rows, LANES), F32),
        grid=(s,),
        in_specs=[pl.BlockSpec((1, rows, LANES), lambda i: (i, 0, 0)),
                  pl.BlockSpec((1, wlen, 2 * LANES), lambda i: (i, 0, 0)),
                  pl.BlockSpec((1, NEW_ROWS, 2 * LANES), lambda i: (i, 0, 0)),
                  _const_spec(bias.shape), _const_spec(sink_rows.shape)],
        out_specs=pl.BlockSpec((1, rows, LANES), lambda i: (i, 0, 0)),
        compiler_params=_cparams(("parallel",)),
        name="swa_sample",
    )(qz, buf, new, bias, sink_rows)


RWKV_CHUNK = 64
RWKV_SEQS_PER_STEP = 4
RWKV_PROMPTS_PER_STEP = 2


def _head_sum(x, hsum_ref):
    return _dot2(_split(x), hsum_ref[...])


def _rwkv_prep_body(ur_ref, uk_ref, uv_ref, ul_ref, pr_ref, pk_ref, pv_ref, pl_ref,
                    mur_ref, muk_ref, muv_ref, mul_ref, w0_ref, w2_ref, a0_ref, a2_ref, g2_ref,
                    kk_ref, ka_ref, hsum_ref, r_o, lw_o, k_o, v_o, na_o, b_o, g_o):
    def mix(u_ref, p_ref, mu_ref):
        u = u_ref[...]
        return u + (p_ref[...] - u) * mu_ref[...]

    r = mix(ur_ref, pr_ref, mur_ref)
    k = mix(uk_ref, pk_ref, muk_ref)
    v = mix(uv_ref, pv_ref, muv_ref)
    lo = mix(ul_ref, pl_ref, mul_ref)
    z = -(w0_ref[...] + _dot3(_split(jnp.tanh(lo)), _split(w2_ref[...])))
    softplus = jnp.maximum(z, 0.0) + jnp.log(1.0 + jnp.exp(-jnp.abs(z)))
    wlog = -softplus - 0.5
    a = jax.nn.sigmoid(a0_ref[...] + _dot3(_split(lo), _split(a2_ref[...])))
    g = _dot3(_split(jax.nn.sigmoid(lo)), _split(g2_ref[...]))
    kk = k * kk_ref[...]
    kk = kk / jnp.maximum(jnp.sqrt(_head_sum(kk * kk, hsum_ref)), 1e-12)
    r_o[...] = r
    lw_o[...] = -jnp.exp(wlog)
    k_o[...] = k * (1.0 + (a - 1.0) * ka_ref[...])
    v_o[...] = v
    na_o[...] = -kk
    b_o[...] = kk * a
    g_o[...] = g


def rwkv_prep(us, prevs, mus, w0, w2p, a0, a2p, g2p, k_k, k_a, hsum):
    m = us[0].shape[0]
    tm = min(ROW_TILE, m)
    row = lambda a: pl.BlockSpec((tm, a.shape[1]), lambda i: (i, 0))
    vec = lambda a: a.reshape(1, -1)
    consts = [vec(x) for x in mus] + [vec(w0), w2p, vec(a0), a2p, g2p, vec(k_k), vec(k_a), hsum]
    return pl.pallas_call(
        _rwkv_prep_body,
        out_shape=[jax.ShapeDtypeStruct((m, RWKV_WIDTH), F32)] * 7,
        grid=(m // tm,),
        in_specs=[row(a) for a in us] + [row(a) for a in prevs] + [_const_spec(c.shape) for c in consts],
        out_specs=[pl.BlockSpec((tm, RWKV_WIDTH), lambda i: (i, 0))] * 7,
        compiler_params=_cparams(("parallel",)),
        name="rwkv_prep",
    )(*us, *prevs, *consts)


BNN = (((2,), (1,)), ((0,), (0,)))
BNT = (((2,), (2,)), ((0,), (0,)))
BTN = (((1,), (1,)), ((0,), (0,)))


def _rwkv_chunk_body(r_ref, lw_ref, k_ref, v_ref, a_ref, b_ref, s0_ref, y_ref, sT_ref, st_sc):
    ci = pl.program_id(1)
    nb, c, _ = r_ref.shape
    n = RWKV_N

    def to_batch(ref):
        x = ref[...]
        return jnp.concatenate([x[:, :, h * n:(h + 1) * n] for h in range(RWKV_HEADS)], axis=0)

    @pl.when(ci == 0)
    def _():
        s0 = s0_ref[...]
        st_sc[...] = jnp.concatenate([s0[:, h] for h in range(RWKV_HEADS)], axis=0)

    row = lax.broadcasted_iota(jnp.int32, (c, c), 0)
    col = lax.broadcasted_iota(jnp.int32, (c, c), 1)
    incl = (row >= col)[None]
    strict = (row > col)[None]
    eye_c = (row == col).astype(F32)[None]
    eye_n = (lax.broadcasted_iota(jnp.int32, (n, n), 0) == lax.broadcasted_iota(jnp.int32, (n, n), 1)).astype(F32)[None]
    r, lw, k, v, a, b = (to_batch(ref) for ref in (r_ref, lw_ref, k_ref, v_ref, a_ref, b_ref))
    nbat = r.shape[0]
    ones_incl = jnp.broadcast_to(incl.astype(BF16), (nbat, c, c))
    lw_hi, lw_lo = _split(lw)
    lw_lo2 = (lw - lw_hi.astype(F32) - lw_lo.astype(F32)).astype(BF16)
    csum = lambda y: lax.dot_general(ones_incl, y, BNN, preferred_element_type=F32)
    cs = csum(lw_hi) + (csum(lw_lo) + csum(lw_lo2))
    gam = jnp.exp(cs)
    ginv = jnp.exp(-cs)
    to_end = jnp.exp(cs[:, c - 1:c, :] - cs)
    at = _split(a * jnp.exp(cs - lw))
    rt_f = r * gam
    rt = _split(rt_f)
    bt = _split(b * ginv)
    kt = _split(k * ginv)
    vs = _split(v)
    lb = jnp.where(strict, _dot3(at, bt, BNT), 0.0)
    lk = jnp.where(strict, _dot3(at, kt, BNT), 0.0)
    pb = _split(jnp.where(incl, _dot3(rt, bt, BNT), 0.0))
    pk = _split(jnp.where(incl, _dot3(rt, kt, BNT), 0.0))
    tinv = eye_c + lb
    lp = _split(lb)
    covered = 2
    while covered < c:
        lp = _split(_dot3(lp, lp, BNN))
        tinv = tinv + _dot3(_split(tinv), lp, BNN)
        covered *= 2
    tinv = _split(tinv)
    w = _split(_dot3(tinv, at, BNN))
    uv = _split(_dot3(tinv, _split(_dot3(_split(lk), vs, BNN)), BNN))
    q = rt_f + _dot3(pb, w, BNN)
    bh = _split(b * to_end)
    kh = _split(k * to_end)
    tm = eye_n * gam[:, c - 1:c, :] + _dot3(bh, w, BTN)
    bm = _dot3(bh, uv, BTN) + _dot3(kh, vs, BTN)
    x = _dot3(_split(jnp.concatenate([q, tm], axis=1)), _split(st_sc[...]), BNN)
    y = x[:, :c] + _dot3(pb, uv, BNN) + _dot3(pk, vs, BNN)
    st_sc[...] = x[:, c:] + bm
    y_ref[...] = jnp.concatenate([y[h * nb:(h + 1) * nb] for h in range(RWKV_HEADS)], axis=2)

    @pl.when(ci == pl.num_programs(1) - 1)
    def _():
        st = st_sc[...]
        for h in range(RWKV_HEADS):
            sT_ref[:, h] = st[h * nb:(h + 1) * nb]


def rwkv_chunk(seqs, s0t, chunk, nb):
    b, t, width = seqs[0].shape
    h, n = s0t.shape[1:3]
    seq_spec = pl.BlockSpec((nb, chunk, width), lambda i, j: (i, j, 0))
    st_spec = pl.BlockSpec((nb, h, n, n), lambda i, j: (i, 0, 0, 0))
    return pl.pallas_call(
        _rwkv_chunk_body,
        out_shape=[jax.ShapeDtypeStruct((b, t, width), F32), jax.ShapeDtypeStruct((b, h, n, n), F32)],
        grid=(b // nb, t // chunk),
        in_specs=[seq_spec] * 6 + [st_spec],
        out_specs=[seq_spec, st_spec],
        scratch_shapes=[pltpu.VMEM((h * nb, n, n), F32)],
        compiler_params=_cparams(("parallel", "arbitrary")),
        name="rwkv_chunk",
    )(*seqs, s0t)


def _rwkv_post_body(y_ref, r_ref, k_ref, v_ref, g_ref, rk_ref, lnw_ref, lnb_ref, hsum_ref, o_ref):
    y = y_ref[...]
    inv_n = 1.0 / RWKV_N
    mean = _head_sum(y, hsum_ref) * inv_n
    yc = y - mean
    var = _head_sum(yc * yc, hsum_ref) * inv_n
    yn = yc * lax.rsqrt(var + RWKV_GN_EPS) * lnw_ref[...] + lnb_ref[...]
    rk = _head_sum(r_ref[...] * k_ref[...] * rk_ref[...], hsum_ref)
    o_ref[...] = (yn + rk * v_ref[...]) * g_ref[...]


def rwkv_post(y, r, k, v, g, r_k, ln_w, ln_b, hsum):
    m = y.shape[0]
    tm = min(ROW_TILE, m)
    row = pl.BlockSpec((tm, RWKV_WIDTH), lambda i: (i, 0))
    vec = lambda a: a.reshape(1, -1)
    return pl.pallas_call(
        _rwkv_post_body,
        out_shape=jax.ShapeDtypeStruct((m, RWKV_WIDTH), F32),
        grid=(m // tm,),
        in_specs=[row] * 5 + [_const_spec((1, RWKV_WIDTH))] * 3 + [_const_spec(hsum.shape)],
        out_specs=row,
        compiler_params=_cparams(("parallel",)),
        name="rwkv_post",
    )(y, r, k, v, g, vec(r_k), vec(ln_w), vec(ln_b), hsum)


RWKV_SIZES = (RWKV_WIDTH, RWKV_WIDTH, RWKV_WIDTH, DECAY_LORA, AAA_LORA, GATE_LORA)
LORA_WIDTH = DECAY_LORA + AAA_LORA + GATE_LORA
ODD_SIZES = (SWA_HEADS * HEAD_DIM, 2 * SWA_KV_HEADS * HEAD_DIM, 3 * RWKV_WIDTH + LORA_WIDTH)
ODD_GROUPS = ((0, 512, (F32,)), (512, 256, (F32, BF16)), (768, 512, (F32,)), (1280, 512, (F32,)),
              (1792, 512, (F32,)), (2304, LORA_PAD, (F32,)))


def _split_cols(w, sizes):
    offs = np.cumsum([0] + list(sizes))
    return [w[..., int(offs[i]):int(offs[i + 1])] for i in range(len(sizes))]


def odd_weights(w_in):
    return jnp.pad(w_in, ((0, 0), (0, LORA_PAD - LORA_WIDTH))).astype(BF16)


def _lora_rows(w, first):
    return jnp.pad(w.astype(F32), ((first, LORA_PAD - first - w.shape[0]), (0, 0)))


def odd_mixer(x, g, w_odd, swa_buf, wkv0, shift0, prompt, sinks, tab, rw, hsum):
    b, t, d = x.shape
    mu, w0, w2, a0, a2, g2, k_k, k_a, r_k, ln_w, ln_b = rw
    q, kv, kvb, ur, uk, uv, ul = norm_proj(x.reshape(b * t, d), g, w_odd, ODD_GROUPS)
    r3 = lambda a: a.reshape(b, t, a.shape[-1])
    tab_s = tab[:, :SWA_HEADS]
    if prompt:
        o_swa = swa_prompt(r3(q), r3(kvb), sinks, tab_s).reshape(b * t, -1)
        ctx = r3(kv)[:, t - min(SWA_WINDOW, t):]
        shift0 = jnp.zeros((b, 3 * RWKV_WIDTH + LORA_WIDTH), F32)
        wkv0 = jnp.zeros((b, RWKV_HEADS, RWKV_N, RWKV_N), F32)
        chunk = RWKV_CHUNK
        t_pad = t
        nb = math.gcd(b, RWKV_PROMPTS_PER_STEP)
    else:
        rows = SWA_KV_HEADS * t * SWA_GROUP
        q5 = q.reshape(b, t, SWA_KV_HEADS, SWA_GROUP, HEAD_DIM).transpose(0, 2, 1, 3, 4)
        z = jnp.zeros_like(q5[:, 0])
        qz = jnp.stack([jnp.concatenate([q5[:, 0], z], -1), jnp.concatenate([z, q5[:, 1]], -1)], axis=1).reshape(b, rows, LANES)
        new = jnp.pad(r3(kv), ((0, 0), (0, NEW_ROWS - t), (0, 0)))
        o_rows = swa_sample(qz, swa_buf, new, sinks, tab_s, t)
        o6 = o_rows.reshape(b, SWA_KV_HEADS, t, SWA_GROUP, 2, HEAD_DIM)
        o_swa = jnp.stack([o6[:, 0, :, :, 0], o6[:, 1, :, :, 1]], axis=2).reshape(b * t, SWA_HEADS * HEAD_DIM)
        ctx = jnp.concatenate([swa_buf, r3(kv)], axis=1)[:, t:]
        chunk = NEW_ROWS
        t_pad = NEW_ROWS
        nb = math.gcd(b, RWKV_SEQS_PER_STEP)

    s_r, s_k, s_v, s_l = _split_cols(shift0.astype(F32), (RWKV_WIDTH,) * 3 + (LORA_WIDTH,))
    s_l = jnp.pad(s_l, ((0, 0), (0, LORA_PAD - LORA_WIDTH)))
    us = [ur, uk, uv, ul]
    prevs = [jnp.concatenate([s[:, None], r3(u)[:, :-1]], axis=1).reshape(b * t, -1) for s, u in zip((s_r, s_k, s_v, s_l), us)]
    mu_r, mu_k, mu_v, mu_l = _split_cols(mu.astype(F32), (RWKV_WIDTH,) * 3 + (LORA_WIDTH,))
    mu_l = jnp.pad(mu_l, (0, LORA_PAD - LORA_WIDTH))
    r, lw, k, v, na, bb, gate = rwkv_prep(us, prevs, [mu_r, mu_k, mu_v, mu_l], w0, _lora_rows(w2, 0), a0,
                                          _lora_rows(a2, DECAY_LORA), _lora_rows(g2, DECAY_LORA + AAA_LORA), k_k, k_a, hsum)

    def seq(a):
        return jnp.pad(a.reshape(b, t, RWKV_WIDTH), ((0, 0), (0, t_pad - t), (0, 0)))

    y, st = rwkv_chunk([seq(a) for a in (r, lw, k, v, na, bb)], jnp.swapaxes(wkv0.astype(F32), -1, -2), chunk, nb)
    y = y[:, :t].reshape(b * t, RWKV_WIDTH)
    o_rwkv = rwkv_post(y, r, k, v, gate, r_k, ln_w, ln_b, hsum)
    shift = jnp.concatenate([r3(ur)[:, -1], r3(uk)[:, -1], r3(uv)[:, -1], r3(ul)[:, -1, :LORA_WIDTH]], axis=-1)
    return o_swa, o_rwkv, ctx, jnp.swapaxes(st, -1, -2), shift


EVEN_SIZES = (NSA_HEADS * HEAD_DIM, 4 * NSA_KV_HEADS * HEAD_DIM, 2 * NSA_KV_HEADS * HEAD_DIM, 3 * NSA_HEADS,
              2 * DIFF_HEADS * HEAD_DIM, 2 * DIFF_HEADS * DIFF_VDIM)
EVEN_GROUPS = ((0, 512, (F32,)), (512, 512, (F32, BF16)), (1024, 256, (F32, BF16)), (1280, 512, (F32,)),
               (1792, 1024, (F32, BF16)), (2816, 128, (F32,)))


def even_weights(w_in):
    wq, wkv, wkvw, wgl, wdq, wdkv = _split_cols(w_in, EVEN_SIZES)
    wgl = jnp.pad(wgl, ((0, 0), (0, LANES - wgl.shape[1])))
    return jnp.concatenate([wq, wkv, wkvw, wdq, wdkv, wgl], axis=1).astype(BF16)


def cmp_lane_weights(cmp_w):
    wt = jax.nn.softmax(cmp_w.astype(F32), axis=-1)
    wl = jnp.repeat(jnp.swapaxes(wt, 1, 2), HEAD_DIM, axis=2)
    return wl[:, :CMP_STRIDE], wl[:, CMP_STRIDE:]


def diff_scalars(lq, layer, tab_d):
    lam_init = 0.8 - 0.6 * math.exp(-0.3 * layer)
    lq = lq.astype(F32)
    lam = jnp.exp(jnp.sum(lq[0] * lq[1])) - jnp.exp(jnp.sum(lq[2] * lq[3])) + lam_init
    return jnp.concatenate([jnp.stack([lam, jnp.asarray(1.0 - lam_init, F32)]), tab_d[N_BUCKETS - 1].astype(F32)])


def even_prompt(x, g, w_even, w1, w2, tab, scal, subln):
    b, t, d = x.shape
    q, kv, kvb, kvw, kvwb, dq, dkv, dkvb, gl = norm_proj(x.reshape(b * t, d), g, w_even, EVEN_GROUPS)
    r3 = lambda a: a.reshape(b, t, a.shape[-1])
    tab_n, tab_d = tab[:, :NSA_HEADS], tab[:, NSA_HEADS:NSA_HEADS + DIFF_HEADS]
    tabs = nsa_prompt_tables(tab_n, t)
    ckv = compress_prompt(r3(kv), w1, w2, tabs["c2s"].shape[0])
    o_nsa = nsa_prompt(r3(q), r3(gl), ckv, r3(kvb), r3(kvwb), tabs)
    o_diff = diff_prompt(r3(dq), r3(dkvb), scal, _prev_diag_bias(tab_d), subln)
    return o_nsa.reshape(b * t, -1), o_diff.reshape(b * t, -1), kv, kvw, dkv


def even_sample(x, g, w_even, nsa_pool, diff_pool, win_buf, page_table, w1, w2, tab, scal, subln):
    s, n_tok, d = x.shape
    q, kv, _, kvw, _, dq, dkv, _, gl = norm_proj(x.reshape(s * n_tok, d), g, w_even, EVEN_GROUPS)
    tab_n, tab_d = tab[:, :NSA_HEADS], tab[:, NSA_HEADS:NSA_HEADS + DIFF_HEADS]
    n_pages = page_table.shape[1]
    n_pool, page_rows = nsa_pool.shape[:2]
    past = n_pages * page_rows
    wlen = win_buf.shape[1]
    n_pg = min(NSA_PAGES_PER_STEP, n_pages)
    rows = NSA_KV_HEADS * n_tok * NSA_GROUP

    q5 = q.reshape(s, n_tok, NSA_KV_HEADS, NSA_GROUP, HEAD_DIM).transpose(0, 2, 1, 3, 4)
    z = jnp.zeros_like(q5[:, 0])
    qz = jnp.stack([jnp.concatenate([q5[:, 0], z], -1), jnp.concatenate([z, q5[:, 1]], -1)], axis=1).reshape(s, rows, LANES)
    glr = gl[:, :3 * NSA_HEADS].reshape(s, n_tok, NSA_KV_HEADS, NSA_GROUP, 3).transpose(0, 2, 1, 3, 4).reshape(s, rows, 3)
    glr = jnp.pad(glr, ((0, 0), (0, 0), (0, LANES - 3)))
    pad_tok = lambda a: jnp.pad(a.reshape(s, n_tok, a.shape[-1]), ((0, 0), (0, NEW_ROWS - n_tok), (0, 0)))
    tabs = nsa_decode_tables(tab_n, past, n_tok, wlen, n_pages // n_pg, n_pg, page_rows)
    pool_t = jnp.transpose(nsa_pool, (0, 2, 3, 4, 1)).reshape(n_pool, -1, page_rows)
    o_rows = nsa_decode(page_table, pool_t, qz, glr, pad_tok(kv), win_buf.reshape(s, wlen, -1), pad_tok(kvw),
                        w1, w2, tabs, n_tok)
    o6 = o_rows.reshape(s, NSA_KV_HEADS, n_tok, NSA_GROUP, 2, HEAD_DIM)
    o_nsa = jnp.stack([o6[:, 0, :, :, 0], o6[:, 1, :, :, 1]], axis=2).reshape(s * n_tok, NSA_HEADS * HEAD_DIM)

    dq5 = dq.reshape(s, n_tok, DIFF_HEADS, 2, HEAD_DIM).transpose(0, 2, 3, 1, 4)
    zd = jnp.zeros_like(dq5[:, :, 0])
    qd = jnp.concatenate([jnp.concatenate([dq5[:, :, 0], zd], -1), jnp.concatenate([zd, dq5[:, :, 1]], -1)], axis=2)
    pool_rows = diff_pool.reshape(n_pool, page_rows * 2 * DIFF_HEADS, DIFF_VDIM)
    o_d = diff_decode(page_table, pool_rows, qd, pad_tok(dkv), scal, diff_decode_tables(tab_d, n_tok), subln,
                      n_tok, page_rows)
    o_diff = o_d.transpose(0, 2, 1, 3).reshape(s * n_tok, DIFF_HEADS * DIFF_VDIM)
    return o_nsa, o_diff, kv, kvw, dkv


def kernel(x_prompt, x_sample, cache_nsa_kv, cache_diff_kv, cache_nsa_win, cache_swa, state_rwkv_wkv, state_rwkv_shift, page_table, rel_bias, norm_mix, norm_ffn, norm_final, w_in_even, w_out_even, nsa_cmp_w, diff_lambda, diff_subln, w_in_odd, w_out_odd, swa_sinks, rwkv_mu, rwkv_w0, rwkv_w2, rwkv_a0, rwkv_a2, rwkv_g2, rwkv_k_k, rwkv_k_a, rwkv_r_k, rwkv_ln_w, rwkv_ln_b, ffn_w_gate, ffn_w_up, ffn_w_down):
    b, t, d = x_prompt.shape
    s, n_tok, _ = x_sample.shape
    depth = norm_mix.shape[0]
    assert NSA_WINDOW == 2 * TQ and t % TQ == 0 and TQ >= MAX_DISTANCE and n_tok <= NEW_ROWS and depth > 0
    tab = rel_bias.astype(F32)
    tab_d = tab[:, NSA_HEADS:NSA_HEADS + DIFF_HEADS]
    head_id = np.arange(RWKV_WIDTH) // RWKV_N
    hsum = jnp.asarray((head_id[:, None] == head_id[None, :]).astype(np.float32), BF16)
    xp = x_prompt.reshape(b * t, d)
    xs = x_sample.reshape(s * n_tok, d)
    outs = {name: [] for name in ("nsa_p", "nsa_s", "diff_p", "diff_s", "win_p", "win_s",
                                  "swa_p", "swa_s", "wkv_p", "wkv_s", "sh_p", "sh_s")}
    for l in range(depth):
        if l % 2 == 0:
            e = l // 2
            w_even = even_weights(w_in_even[e])
            w1, w2 = cmp_lane_weights(nsa_cmp_w[e])
            scal = diff_scalars(diff_lambda[e], l, tab_d)
            pa, pb, kv, kvw, dkv = even_prompt(xp.reshape(b, t, d), norm_mix[l], w_even, w1, w2, tab, scal, diff_subln[e])
            sa, sb, skv, skvw, sdkv = even_sample(xs.reshape(s, n_tok, d), norm_mix[l], w_even, cache_nsa_kv[e],
                                                  cache_diff_kv[e], cache_nsa_win[e], page_table, w1, w2, tab, scal,
                                                  diff_subln[e])
            outs["nsa_p"].append(kv.reshape(b, t, 4, NSA_KV_HEADS, HEAD_DIM))
            outs["nsa_s"].append(skv.reshape(s, n_tok, 4, NSA_KV_HEADS, HEAD_DIM))
            outs["diff_p"].append(dkv.reshape(b, t, 2, DIFF_HEADS, DIFF_VDIM))
            outs["diff_s"].append(sdkv.reshape(s, n_tok, 2, DIFF_HEADS, DIFF_VDIM))
            outs["win_p"].append(kvw.reshape(b, t, 2, NSA_KV_HEADS, HEAD_DIM)[:, t - min(NSA_WINDOW, t):])
            new_win = skvw.reshape(s, n_tok, 2, NSA_KV_HEADS, HEAD_DIM).astype(cache_nsa_win.dtype)
            outs["win_s"].append(jnp.concatenate([cache_nsa_win[e], new_win], axis=1)[:, n_tok:])
            w_out = w_out_even[e].astype(BF16)
        else:
            o = l // 2
            rw = (rwkv_mu[o], rwkv_w0[o], rwkv_w2[o], rwkv_a0[o], rwkv_a2[o], rwkv_g2[o], rwkv_k_k[o],
                  rwkv_k_a[o], rwkv_r_k[o].reshape(-1), rwkv_ln_w[o], rwkv_ln_b[o])
            w_odd = odd_weights(w_in_odd[o])
            pa, pb, ctx_p, wkv_p, sh_p = odd_mixer(xp.reshape(b, t, d), norm_mix[l], w_odd, None, None, None, True,
                                                   swa_sinks[o], tab, rw, hsum)
            swa_buf = cache_swa[o].reshape(s, cache_swa.shape[2], -1).astype(F32)
            sa, sb, ctx_s, wkv_s, sh_s = odd_mixer(xs.reshape(s, n_tok, d), norm_mix[l], w_odd, swa_buf,
                                                   state_rwkv_wkv[o], state_rwkv_shift[o], False, swa_sinks[o], tab,
                                                   rw, hsum)
            kv_shape = (2, SWA_KV_HEADS, HEAD_DIM)
            outs["swa_p"].append(ctx_p.reshape(b, -1, *kv_shape))
            outs["swa_s"].append(ctx_s.reshape(s, -1, *kv_shape))
            outs["wkv_p"].append(wkv_p)
            outs["wkv_s"].append(wkv_s)
            outs["sh_p"].append(sh_p)
            outs["sh_s"].append(sh_s)
            w_out = w_out_odd[o].astype(BF16)
        half = pa.shape[1]
        xp = out_proj(xp, pa, pb, w_out[:half], w_out[half:])
        xs = out_proj(xs, sa, sb, w_out[:half], w_out[half:])
        wg, wu, wd = ffn_w_gate[l].astype(BF16), ffn_w_up[l].astype(BF16), ffn_w_down[l].astype(BF16)
        last = l == depth - 1
        xp = ffn(xp, norm_ffn[l], norm_final, wg, wu, wd, last)
        xs = ffn(xs, norm_ffn[l], norm_final, wg, wu, wd, last)
    st = lambda name: jnp.stack(outs[name])
    return (xp.reshape(b, t, d), xs.reshape(s, n_tok, d), st("nsa_p"), st("nsa_s"), st("diff_p"), st("diff_s"),
            st("win_p"), st("win_s"), st("swa_p"), st("swa_s"), st("wkv_p"), st("wkv_s"), st("sh_p"), st("sh_s"))
```

```python
import functools
import math

import numpy as np
import jax
import jax.numpy as jnp
from jax import lax
from jax.experimental import pallas as pl
from jax.experimental.pallas import tpu as pltpu

F32 = jnp.float32
BF16 = jnp.bfloat16
HI = lax.Precision.HIGHEST

HEAD_DIM = 64
NSA_KV_HEADS = 2
NSA_GROUP = 4
NSA_HEADS = NSA_KV_HEADS * NSA_GROUP
CMP_STRIDE = 16
CMP_LEN = 32
SEL_BLOCK = 64
N_SELECT = 16
NSA_WINDOW = 512
DIFF_HEADS = 4
DIFF_VDIM = 128
SWA_HEADS = 8
SWA_KV_HEADS = 2
SWA_GROUP = 4
SWA_WINDOW = 128
RWKV_N = 64
RWKV_HEADS = 8
RWKV_WIDTH = RWKV_N * RWKV_HEADS
DECAY_LORA = 32
AAA_LORA = 32
GATE_LORA = 96
LORA_PAD = 256
N_BUCKETS = 32
MAX_DISTANCE = 128
NORM_EPS = 1e-6
SUBLN_EPS = 1e-5
RWKV_GN_EPS = 64e-5
NEG = -1e30
FORCE = 1e6
REMOVED = -3e38

LANES = 128
SUBLANES = 8
VMEM_LIMIT_BYTES = 56 * 1024 * 1024

TQ = 256
FAR_TILES = 4
ROW_TILE = 512
CMP_PAD = 16

NN = (((1,), (0,)), ((), ()))
NT = (((1,), (1,)), ((), ()))
TN = (((0,), (0,)), ((), ()))


def _cparams(sem):
    return pltpu.CompilerParams(dimension_semantics=sem, vmem_limit_bytes=VMEM_LIMIT_BYTES)


def _const_spec(shape):
    n = len(shape)
    return pl.BlockSpec(shape, lambda *_: (0,) * n)


def _smem_spec():
    return pl.BlockSpec(memory_space=pltpu.SMEM)


def _dot(a, b, precision=None):
    return jnp.dot(a, b, preferred_element_type=F32, precision=precision)


def _dot_nt(a, b, precision=None):
    return lax.dot_general(a, b, NT, preferred_element_type=F32, precision=precision)


def _split(x):
    hi = x.astype(BF16)
    return hi, (x - hi.astype(F32)).astype(BF16)


def _dot3(a, b, dims=NN):
    f = lambda x, y: lax.dot_general(x, y, dims, preferred_element_type=F32)
    return f(a[0], b[0]) + (f(a[1], b[0]) + f(a[0], b[1]))


def _dot2(a, b, dims=NN):
    f = lambda x, y: lax.dot_general(x, y, dims, preferred_element_type=F32)
    return f(a[0], b) + f(a[1], b)


def _norm_proj_body(x_ref, g_ref, w_ref, *out_refs, groups):
    x = x_ref[...]
    h = (x * lax.rsqrt(jnp.mean(x * x, axis=-1, keepdims=True) + NORM_EPS)) * g_ref[...]
    hb = h.astype(BF16)
    i = 0
    for off, wd, dts in groups:
        r = _dot(hb, w_ref[:, off:off + wd])
        for dt in dts:
            out_refs[i][...] = r.astype(dt)
            i += 1


def norm_proj(x2d, g, w_bf16, groups):
    m, d = x2d.shape
    tm = min(ROW_TILE, m)
    out_shape, out_specs = [], []
    for _, wd, dts in groups:
        for dt in dts:
            out_shape.append(jax.ShapeDtypeStruct((m, wd), dt))
            out_specs.append(pl.BlockSpec((tm, wd), lambda i: (i, 0)))
    return pl.pallas_call(
        functools.partial(_norm_proj_body, groups=groups),
        out_shape=out_shape,
        grid=(m // tm,),
        in_specs=[pl.BlockSpec((tm, d), lambda i: (i, 0)), _const_spec((1, d)), _const_spec(w_bf16.shape)],
        out_specs=out_specs,
        compiler_params=_cparams(("parallel",)),
        name="norm_proj",
    )(x2d, g.reshape(1, d), w_bf16)


def _out_proj_body(x_ref, a_ref, b_ref, wa_ref, wb_ref, o_ref):
    acc = _dot(a_ref[...].astype(BF16), wa_ref[...]) + _dot(b_ref[...].astype(BF16), wb_ref[...])
    o_ref[...] = x_ref[...] + acc


def out_proj(x2d, a, b, wa, wb):
    m, d = x2d.shape
    tm = min(ROW_TILE, m)
    return pl.pallas_call(
        _out_proj_body,
        out_shape=jax.ShapeDtypeStruct((m, d), F32),
        grid=(m // tm,),
        in_specs=[pl.BlockSpec((tm, d), lambda i: (i, 0)),
                  pl.BlockSpec((tm, a.shape[1]), lambda i: (i, 0)),
                  pl.BlockSpec((tm, b.shape[1]), lambda i: (i, 0)),
                  _const_spec(wa.shape), _const_spec(wb.shape)],
        out_specs=pl.BlockSpec((tm, d), lambda i: (i, 0)),
        compiler_params=_cparams(("parallel",)),
        name="out_proj",
    )(x2d, a, b, wa, wb)


def _ffn_body(x_ref, g_ref, gf_ref, wg_ref, wu_ref, wd_ref, o_ref, h_sc, acc_sc, *, final_norm):
    f = pl.program_id(1)

    @pl.when(f == 0)
    def _():
        x = x_ref[...]
        h = (x * lax.rsqrt(jnp.mean(x * x, axis=-1, keepdims=True) + NORM_EPS)) * g_ref[...]
        h_sc[...] = h.astype(BF16)
        acc_sc[...] = jnp.zeros_like(acc_sc)

    hb = h_sc[...]
    gate = _dot(hb, wg_ref[...])
    up = _dot(hb, wu_ref[...])
    act = (gate * jax.nn.sigmoid(gate)) * up
    acc_sc[...] += _dot(act.astype(BF16), wd_ref[...])

    @pl.when(f == pl.num_programs(1) - 1)
    def _():
        y = x_ref[...] + acc_sc[...]
        if final_norm:
            y = (y * lax.rsqrt(jnp.mean(y * y, axis=-1, keepdims=True) + NORM_EPS)) * gf_ref[...]
        o_ref[...] = y


def ffn(x2d, g, g_final, wg, wu, wd, final_norm):
    m, d = x2d.shape
    dff = wg.shape[1]
    tm = min(ROW_TILE, m)
    tf = dff // 2
    return pl.pallas_call(
        functools.partial(_ffn_body, final_norm=final_norm),
        out_shape=jax.ShapeDtypeStruct((m, d), F32),
        grid=(m // tm, dff // tf),
        in_specs=[pl.BlockSpec((tm, d), lambda i, f: (i, 0)), _const_spec((1, d)), _const_spec((1, d)),
                  pl.BlockSpec((d, tf), lambda i, f: (0, f)), pl.BlockSpec((d, tf), lambda i, f: (0, f)),
                  pl.BlockSpec((tf, d), lambda i, f: (f, 0))],
        out_specs=pl.BlockSpec((tm, d), lambda i, f: (i, 0)),
        scratch_shapes=[pltpu.VMEM((tm, d), BF16), pltpu.VMEM((tm, d), F32)],
        compiler_params=_cparams(("parallel", "arbitrary")),
        name="ffn",
    )(x2d, g.reshape(1, d), g_final.reshape(1, d), wg, wu, wd)


def _t5_bucket(dist):
    n = jnp.maximum(dist, 0)
    max_exact = N_BUCKETS // 2
    nf = jnp.maximum(n, 1).astype(F32)
    large = max_exact + (jnp.log(nf / max_exact) / math.log(MAX_DISTANCE / max_exact)
                         * (N_BUCKETS - max_exact)).astype(jnp.int32)
    large = jnp.minimum(large, N_BUCKETS - 1)
    return jnp.where(n < max_exact, n, large)


def _toeplitz_bias(tab, rows, cols, offset, valid_lo, valid_hi):
    length = rows + cols - 1
    d = jnp.arange(length, dtype=jnp.int32) - (cols - 1) + offset
    g = jnp.where((d >= valid_lo) & (d <= valid_hi), tab[_t5_bucket(d)].astype(F32).T, NEG)
    h = g[:, ::-1]
    flat = jnp.tile(h, (1, rows + 1))[:, :rows * (length + 1)].reshape(-1, rows, length + 1)[:, :, :cols]
    return flat[:, ::-1, :]


def _cmp_to_sel(n_cmp, n_sel, rows, cols):
    i = np.arange(n_cmp)[:, None]
    j = np.arange(n_sel)[None, :]
    m = (i * CMP_STRIDE < (j + 1) * SEL_BLOCK) & (i * CMP_STRIDE + CMP_LEN > j * SEL_BLOCK)
    out = np.zeros((rows, cols), np.float32)
    out[CMP_PAD:CMP_PAD + n_cmp, :n_sel] = m
    return out


def _block_expand(n_chunks, n_blocks, chunk_keys):
    c = np.arange(n_chunks)[:, None, None]
    j = np.arange(n_blocks)[None, :, None]
    l = np.arange(chunk_keys)[None, None, :]
    return (j == (c * chunk_keys + l) // SEL_BLOCK).astype(np.float32)


def _place_half(x64, half):
    z = jnp.zeros_like(x64)
    return jnp.concatenate([x64, z], axis=1) if half == 0 else jnp.concatenate([z, x64], axis=1)


def _lane_tile(x, width):
    reps = width // x.shape[1]
    return x if reps == 1 else jnp.concatenate([x] * reps, axis=1)


def _masked_softmax_parts(parts):
    m = None
    for s, msk in parts:
        mm = jnp.max(jnp.where(msk, s, NEG), axis=-1, keepdims=True)
        m = mm if m is None else jnp.maximum(m, mm)
    ps, den = [], None
    for s, msk in parts:
        p = jnp.where(msk, jnp.exp(jnp.where(msk, s, NEG) - m), 0.0)
        ps.append(p)
        d = jnp.sum(p, axis=-1, keepdims=True)
        den = d if den is None else den + d
    inv = 1.0 / jnp.maximum(den, 1e-30)
    return [p * inv for p in ps]


def _select_blocks(score, n_top):
    lane = lax.broadcasted_iota(jnp.int32, score.shape, 1)
    big = score.shape[1]
    sel = jnp.zeros(score.shape, F32)
    sc = score
    for _ in range(n_top):
        m = jnp.max(sc, axis=-1, keepdims=True)
        idx = jnp.min(jnp.where(sc == m, lane, big), axis=-1, keepdims=True)
        hit = lane == idx
        sel = jnp.where(hit & (m > 0.5 * NEG), 1.0, sel)
        sc = jnp.where(hit, REMOVED, sc)
    return sel


def _select_blocks_few_rows(score, n_top):
    r, nb = score.shape
    by_block = jnp.concatenate([score, jnp.zeros((LANES - r, nb), F32)], axis=0).T
    first = lax.broadcasted_iota(jnp.int32, (nb, nb), 0)
    second = lax.broadcasted_iota(jnp.int32, (nb, nb), 1)
    sels = []
    for i in range(r):
        a = by_block[:, i:i + 1]
        b = score[i:i + 1, :]
        precedes = (a > b) | ((a == b) & (first < second))
        rank = jnp.sum(jnp.where(precedes, 1.0, 0.0), axis=0, keepdims=True)
        sels.append(jnp.where((rank < n_top) & (b > 0.5 * NEG), 1.0, 0.0))
    return jnp.concatenate(sels, axis=0)


def _block_scores(imp, q_pos):
    j = lax.broadcasted_iota(jnp.int32, imp.shape, 1)
    cur = q_pos // SEL_BLOCK
    avail = j * SEL_BLOCK <= q_pos
    forced = (j == 0) | (j == cur) | (j == cur - 1)
    return jnp.where(avail, jnp.where(forced, FORCE, imp), NEG)


def _flash_init(h, m_sc, l_sc, acc_sc):
    m_sc[h] = jnp.full(m_sc.shape[1:], REMOVED, F32)
    l_sc[h] = jnp.zeros(l_sc.shape[1:], F32)
    acc_sc[h] = jnp.zeros(acc_sc.shape[1:], F32)


def _flash_update(h, s, vblk, m_sc, l_sc, acc_sc, shift=None):
    m_prev = m_sc[h]
    m_cur = jnp.max(s, axis=-1, keepdims=True)
    if shift is not None:
        m_cur = m_cur + shift
    m_next = jnp.maximum(m_prev, m_cur)
    alpha = jnp.exp(m_prev - m_next)
    sub = m_next if shift is None else m_next - shift
    p = jnp.exp(s - _lane_tile(sub, s.shape[1]))
    l_sc[h] = alpha * l_sc[h] + jnp.sum(p, axis=-1, keepdims=True)
    acc_sc[h] = alpha * acc_sc[h] + _dot(p.astype(BF16), vblk)
    m_sc[h] = m_next


def _causal_tiles(qi, step):
    n_far = jnp.maximum(qi - 1, 0)
    n_big = n_far // FAR_TILES

    def big_body(kb, carry):
        step(pl.multiple_of(kb * (FAR_TILES * TQ), FAR_TILES * TQ), FAR_TILES * TQ, None, kb)
        return carry

    def far_body(kt, carry):
        step(pl.multiple_of(kt * TQ, TQ), TQ, None, kt)
        return carry

    lax.fori_loop(0, n_big, big_body, 0)
    lax.fori_loop(n_big * FAR_TILES, n_far, far_body, 0)

    @pl.when(qi >= 1)
    def _():
        step(pl.multiple_of((qi - 1) * TQ, TQ), TQ, 0, qi - 1)

    step(pl.multiple_of(qi * TQ, TQ), TQ, 1, qi)


def _compress_body(x_ref, w1_ref, w2_ref, o_ref, *, nblk):
    x = x_ref[0].reshape(nblk, CMP_STRIDE, LANES)
    p1 = (x * w1_ref[0][None]).sum(axis=1)
    p2 = (x * w2_ref[0][None]).sum(axis=1)
    ck = p1 + pltpu.roll(p2, nblk - 1, 0)
    row = lax.broadcasted_iota(jnp.int32, ck.shape, 0)
    o_ref[0, 0] = jnp.zeros(o_ref.shape[2:], F32)
    o_ref[0, 0, CMP_PAD:CMP_PAD + nblk, :] = jnp.where(row < nblk - 1, ck, 0.0)


def compress_prompt(kv, w1, w2, cp):
    b, t, _ = kv.shape
    nblk = t // CMP_STRIDE
    return pl.pallas_call(
        functools.partial(_compress_body, nblk=nblk),
        out_shape=jax.ShapeDtypeStruct((b, 2, cp, LANES), F32),
        grid=(b, 2),
        in_specs=[pl.BlockSpec((1, t, LANES), lambda i, k: (i, 0, k)),
                  pl.BlockSpec((1, CMP_STRIDE, LANES), lambda i, k: (k, 0, 0)),
                  pl.BlockSpec((1, CMP_STRIDE, LANES), lambda i, k: (k, 0, 0))],
        out_specs=pl.BlockSpec((1, 1, cp, LANES), lambda i, k: (i, k, 0, 0)),
        compiler_params=_cparams(("parallel", "parallel")),
        name="compress_prompt",
    )(kv, w1, w2)


def _nsa_prompt_body(c31_ref, q_ref, gl_ref, ck_ref, cv_ref, c2s_ref, selk_ref, selv_ref,
                     wk0_ref, wk1_ref, wk2_ref, wv0_ref, wv1_ref, wv2_ref,
                     selb_ref, winb_ref, cnear_ref, e3_ref, e3big_ref, o_ref, m_sc, l_sc, acc_sc, *, cp):
    qi = pl.program_id(1)
    tq = q_ref.shape[1]
    near = tq // CMP_STRIDE + CMP_PAD
    near0 = pl.multiple_of(qi * (tq // CMP_STRIDE), tq // CMP_STRIDE)
    q = q_ref[0] * (HEAD_DIM ** -0.5)
    gates = jax.nn.sigmoid(gl_ref[0])
    n_far = cp - LANES
    ck = _split(ck_ref[0, 0, CMP_PAD:CMP_PAD + n_far, :])
    cv = cv_ref[0, 0, CMP_PAD:CMP_PAD + n_far, :].astype(BF16)
    ck_near = _split(ck_ref[0, 0, pl.ds(near0, near), :])
    cv_near = cv_ref[0, 0, pl.ds(near0, near), :].astype(BF16)
    c2s = c2s_ref[CMP_PAD:CMP_PAD + n_far, :].astype(BF16)
    c2s_near = c2s_ref[pl.ds(near0, near), :].astype(BF16)
    kcat = jnp.concatenate([wk0_ref[0], wk1_ref[0], wk2_ref[0]], axis=0)
    vcat = jnp.concatenate([wv0_ref[0], wv1_ref[0], wv2_ref[0]], axis=0)
    q_pos = qi * tq + lax.broadcasted_iota(jnp.int32, (tq, 1), 0)
    far_mask = lax.broadcasted_iota(jnp.int32, (tq, n_far), 1) < near0 - CMP_PAD
    near_lane = lax.broadcasted_iota(jnp.int32, (tq, near), 1)
    near_ok = (near_lane >= CMP_PAD) | (qi > 0)
    wcol = lax.broadcasted_iota(jnp.int32, (tq, 3 * tq), 1)
    win_ok = wcol >= (2 - qi) * tq

    qzb_all, o_cmp_all, imps = [], [], []
    for g in range(NSA_KV_HEADS):
        psum_far = jnp.zeros((tq, n_far), F32)
        psum_near = jnp.zeros((tq, near), F32)
        for h in range(NSA_GROUP):
            hh = g * NSA_GROUP + h
            qz = _split(_place_half(q[:, hh * HEAD_DIM:(hh + 1) * HEAD_DIM], g))
            qzb_all.append(qz[0])
            s_far = _dot3(qz, ck, NT) + c31_ref[hh]
            nb = cnear_ref[hh]
            s_near = _dot3(qz, ck_near, NT) + nb
            p_far, p_near = _masked_softmax_parts([(s_far, far_mask), (s_near, (nb > 0.5 * NEG) & near_ok)])
            o_cmp_all.append(_dot(p_far.astype(BF16), cv) + _dot(p_near.astype(BF16), cv_near))
            psum_far = psum_far + p_far
            psum_near = psum_near + p_near
        imps.append(_dot2(_split(psum_far), c2s) + _dot2(_split(psum_near), c2s_near))
    sel_all = _select_blocks(_block_scores(jnp.concatenate(imps, axis=0), jnp.concatenate([q_pos] * len(imps), axis=0)),
                             N_SELECT).astype(BF16)

    pieces = []
    for g in range(NSA_KV_HEADS):
        qzb = qzb_all[g * NSA_GROUP:(g + 1) * NSA_GROUP]
        o_cmp = o_cmp_all[g * NSA_GROUP:(g + 1) * NSA_GROUP]
        sel = sel_all[g * tq:(g + 1) * tq]

        for h in range(NSA_GROUP):
            _flash_init(h, m_sc, l_sc, acc_sc)

        def sel_step(k0, size, kind, tile, g=g, qzb=qzb, sel=sel):
            kblk = selk_ref[0, pl.ds(k0, size), :]
            vblk = selv_ref[0, pl.ds(k0, size), :]
            expand = e3big_ref[tile] if size != tq else e3_ref[tile]
            madd = (_dot(sel, expand) - 1.0) * 1e30
            for h in range(NSA_GROUP):
                hh = g * NSA_GROUP + h
                s = _dot_nt(qzb[h], kblk)
                if kind is None:
                    _flash_update(h, s + madd, vblk, m_sc, l_sc, acc_sc, shift=c31_ref[hh])
                else:
                    _flash_update(h, s + (madd + selb_ref[hh, kind]), vblk, m_sc, l_sc, acc_sc)

        _causal_tiles(qi, sel_step)

        for h in range(NSA_GROUP):
            hh = g * NSA_GROUP + h
            wb = winb_ref[hh]
            s = _dot_nt(qzb[h], kcat) + wb
            msk = (wb > 0.5 * NEG) & win_ok
            (p,) = _masked_softmax_parts([(s, msk)])
            o_win = _dot(p.astype(BF16), vcat)
            o_sel = acc_sc[h] / l_sc[h]
            c = g * NSA_GROUP * 3 + h * 3
            o = gates[:, c:c + 1] * o_cmp[h] + gates[:, c + 1:c + 2] * o_sel + gates[:, c + 2:c + 3] * o_win
            pieces.append(o[:, g * HEAD_DIM:(g + 1) * HEAD_DIM])
    o_ref[0] = jnp.concatenate(pieces, axis=1)


def nsa_prompt(q, gl, ckv, kvb, kvwb, tabs):
    b, t, _ = q.shape
    nq = t // TQ
    cp = ckv.shape[2]

    def win_spec(back, col):
        return pl.BlockSpec((1, TQ, LANES), lambda i, j: (i, jnp.maximum(j - back, 0), col))

    return pl.pallas_call(
        functools.partial(_nsa_prompt_body, cp=cp),
        out_shape=jax.ShapeDtypeStruct((b, t, NSA_HEADS * HEAD_DIM), F32),
        grid=(b, nq),
        in_specs=[_smem_spec(),
                  pl.BlockSpec((1, TQ, NSA_HEADS * HEAD_DIM), lambda i, j: (i, j, 0)),
                  pl.BlockSpec((1, TQ, LANES), lambda i, j: (i, j, 0)),
                  pl.BlockSpec((1, 1, cp, LANES), lambda i, j: (i, 0, 0, 0)),
                  pl.BlockSpec((1, 1, cp, LANES), lambda i, j: (i, 1, 0, 0)),
                  _const_spec(tabs["c2s"].shape),
                  pl.BlockSpec((1, t, LANES), lambda i, j: (i, 0, 2)),
                  pl.BlockSpec((1, t, LANES), lambda i, j: (i, 0, 3)),
                  win_spec(2, 0), win_spec(1, 0), win_spec(0, 0),
                  win_spec(2, 1), win_spec(1, 1), win_spec(0, 1),
                  _const_spec(tabs["selb"].shape), _const_spec(tabs["winb"].shape),
                  _const_spec(tabs["cnear"].shape), _const_spec(tabs["e3"].shape),
                  _const_spec(tabs["e3big"].shape)],
        out_specs=pl.BlockSpec((1, TQ, NSA_HEADS * HEAD_DIM), lambda i, j: (i, j, 0)),
        scratch_shapes=[pltpu.VMEM((NSA_GROUP, TQ, LANES), F32), pltpu.VMEM((NSA_GROUP, TQ, LANES), F32),
                        pltpu.VMEM((NSA_GROUP, TQ, LANES), F32)],
        compiler_params=_cparams(("parallel", "parallel")),
        name="nsa_prompt",
    )(tabs["c31"], q, gl, ckv, ckv, tabs["c2s"], kvb, kvb, kvwb, kvwb, kvwb, kvwb, kvwb, kvwb,
      tabs["selb"], tabs["winb"], tabs["cnear"], tabs["e3"], tabs["e3big"])


def _prev_diag_bias(tab):
    big = 1 << 30
    return jnp.stack([_toeplitz_bias(tab, TQ, TQ, TQ, 0, big), _toeplitz_bias(tab, TQ, TQ, 0, 0, big)], axis=1)


def nsa_prompt_tables(tab_n, t):
    nq = t // TQ
    cp = t // CMP_STRIDE + LANES
    near = TQ // CMP_STRIDE + CMP_PAD
    cnear = _toeplitz_bias(tab_n, TQ, near * CMP_STRIDE, CMP_PAD * CMP_STRIDE - (CMP_LEN - 1), 0, 1 << 30)
    return dict(
        c31=tab_n[N_BUCKETS - 1].astype(F32),
        selb=_prev_diag_bias(tab_n),
        winb=_toeplitz_bias(tab_n, TQ, 3 * TQ, 2 * TQ, 0, NSA_WINDOW - 1),
        cnear=cnear[:, :, ::CMP_STRIDE],
        c2s=jnp.asarray(_cmp_to_sel(t // CMP_STRIDE - 1, t // SEL_BLOCK, cp, LANES)),
        e3=jnp.asarray(_block_expand(nq, LANES, TQ), BF16),
        e3big=jnp.asarray(_block_expand(max(nq // FAR_TILES, 1), LANES, FAR_TILES * TQ), BF16),
    )


DIFF_HEADS_PER_STEP = 4


def _diff_prompt_body(sc_ref, q_ref, k_ref, v_ref, bias_ref, sub_ref, o_ref, m_sc, l_sc, acc_sc):
    hp = pl.program_id(1)
    qi = pl.program_id(2)
    tq = q_ref.shape[1]
    nh = DIFF_HEADS_PER_STEP
    q = q_ref[0] * (HEAD_DIM ** -0.5)
    q2, c31 = [], []
    for h in range(nh):
        qh = q[:, h * LANES:(h + 1) * LANES]
        lane = lax.broadcasted_iota(jnp.int32, qh.shape, 1)
        q2.append(jnp.concatenate([jnp.where(lane < HEAD_DIM, qh, 0.0), jnp.where(lane >= HEAD_DIM, qh, 0.0)],
                                  axis=0).astype(BF16))
        c31.append(sc_ref[2 + hp * nh + h])
        _flash_init(h, m_sc, l_sc, acc_sc)

    def step(k0, size, kind, tile):
        del tile
        for h in range(nh):
            s = _dot_nt(q2[h], k_ref[0, pl.ds(k0, size), h * LANES:(h + 1) * LANES])
            vblk = v_ref[0, pl.ds(k0, size), h * LANES:(h + 1) * LANES]
            if kind is None:
                _flash_update(h, s, vblk, m_sc, l_sc, acc_sc, shift=c31[h])
            else:
                bt = bias_ref[h, kind]
                _flash_update(h, s + jnp.concatenate([bt, bt], axis=0), vblk, m_sc, l_sc, acc_sc)

    _causal_tiles(qi, step)
    outs = []
    for h in range(nh):
        o12 = acc_sc[h] / l_sc[h]
        o = o12[:tq] - sc_ref[0] * o12[tq:]
        y = (o * lax.rsqrt(jnp.mean(o * o, axis=-1, keepdims=True) + SUBLN_EPS)) * sub_ref[...]
        outs.append(y * sc_ref[1])
    o_ref[0] = jnp.concatenate(outs, axis=1)


def diff_prompt(dq, dkvb, scal, bias, subln):
    b, t, _ = dq.shape
    nq = t // TQ
    nh = DIFF_HEADS_PER_STEP
    wide = nh * LANES
    return pl.pallas_call(
        _diff_prompt_body,
        out_shape=jax.ShapeDtypeStruct((b, t, DIFF_HEADS * DIFF_VDIM), F32),
        grid=(b, DIFF_HEADS // nh, nq),
        in_specs=[_smem_spec(),
                  pl.BlockSpec((1, TQ, wide), lambda i, h, j: (i, j, h)),
                  pl.BlockSpec((1, t, wide), lambda i, h, j: (i, 0, h)),
                  pl.BlockSpec((1, t, wide), lambda i, h, j: (i, 0, DIFF_HEADS // nh + h)),
                  pl.BlockSpec((nh, 2, TQ, TQ), lambda i, h, j: (h, 0, 0, 0)),
                  _const_spec((1, DIFF_VDIM))],
        out_specs=pl.BlockSpec((1, TQ, wide), lambda i, h, j: (i, j, h)),
        scratch_shapes=[pltpu.VMEM((nh, 2 * TQ, LANES), F32), pltpu.VMEM((nh, 2 * TQ, LANES), F32),
                        pltpu.VMEM((nh, 2 * TQ, LANES), F32)],
        compiler_params=_cparams(("parallel", "parallel", "parallel")),
        name="diff_prompt",
    )(scal, dq, dkvb, dkvb, bias, subln.reshape(1, DIFF_VDIM))


NSA_PAGES_PER_STEP = 32
DIFF_PAGES_PER_STEP = 16
NEW_ROWS = 8


def _pad_rows(x, rows):
    return jnp.concatenate([x, jnp.zeros((rows - x.shape[0], x.shape[1]), x.dtype)], axis=0)


def _nsa_dec_body(pt_ref, *refs, n_pg, n_ch, past, cpd, n_tok):
    del pt_ref
    pages = refs[:n_pg]
    (qz_ref, glr_ref, newkv_ref, wbuf_ref, neww_ref, w1_ref, w2_ref, c2s_ref, cmpb_ref, selnear_ref,
     winb_ref, c31_ref, e3_ref, rsum_ref, rexp_ref, o_ref, p1k, p2k, p1v, p2v, sc_sc, vt_sc) = refs[n_pg:]
    c = pl.program_id(1)
    page_rows = pages[0].shape[2]
    blk_per_page = page_rows // CMP_STRIDE
    qz = qz_ref[0] * (HEAD_DIM ** -0.5)
    qzb = qz.astype(BF16)

    @pl.when(c == 0)
    def _():
        for ref in (p1k, p2k, p1v, p2v):
            ref[...] = jnp.zeros(ref.shape, F32)

    for i in range(n_pg):
        page = pages[i][0]
        pg = c * n_pg + i
        blk0 = pl.multiple_of(CMP_PAD + pg * blk_per_page, SUBLANES)
        xk = page[0:LANES, :].T.reshape(blk_per_page, CMP_STRIDE, LANES)
        xv = page[LANES:2 * LANES, :].T.reshape(blk_per_page, CMP_STRIDE, LANES)
        p1k[pl.ds(blk0, blk_per_page), :] = (xk * w1_ref[0][None]).sum(axis=1)
        p2k[pl.ds(blk0, blk_per_page), :] = (xk * w2_ref[0][None]).sum(axis=1)
        p1v[pl.ds(blk0, blk_per_page), :] = (xv * w1_ref[1][None]).sum(axis=1)
        p2v[pl.ds(blk0, blk_per_page), :] = (xv * w2_ref[1][None]).sum(axis=1)
        sc_sc[c, :, i * page_rows:(i + 1) * page_rows] = _dot(qzb, page[2 * LANES:3 * LANES, :].astype(BF16))
        vt_sc[c, :, i * page_rows:(i + 1) * page_rows] = page[3 * LANES:4 * LANES, :].astype(BF16)

    @pl.when(c == n_ch - 1)
    def _():
        nk = newkv_ref[0]
        s_new = _dot_nt(qzb, _pad_rows(nk[:, 2 * LANES:3 * LANES], LANES).astype(BF16))
        v_new = _pad_rows(nk[:, 3 * LANES:4 * LANES], LANES).astype(BF16)

        ck = p1k[...] + pltpu.roll(p2k[...], cpd - 1, 0)
        cv = p1v[...] + pltpu.roll(p2v[...], cpd - 1, 0)
        cb = cmpb_ref[...]
        (p_cmp,) = _masked_softmax_parts([(_dot3(_split(qz), _split(ck), NT) + cb, cb > 0.5 * NEG)])
        o_cmp = _dot(p_cmp.astype(BF16), cv.astype(BF16))
        p_hi, p_lo = _split(p_cmp)
        head_sum = rsum_ref[...].astype(BF16)
        imp = _dot2(_split(_dot(head_sum, p_hi) + _dot(head_sum, p_lo)), c2s_ref[...].astype(BF16))
        rows = lax.broadcasted_iota(jnp.int32, (imp.shape[0], 1), 0)
        sel = _select_blocks_few_rows(_block_scores(imp, past + rows % n_tok), N_SELECT)
        sel = _dot(rexp_ref[...], sel).astype(BF16)

        c31 = c31_ref[...][:, :1]
        selnear = selnear_ref[...]
        chunk = n_pg * page_rows
        bpc = chunk // SEL_BLOCK
        expand = e3_ref[...]
        parts = []
        for ci in range(n_ch):
            s = sc_sc[ci] + (_dot(sel[:, ci * bpc:(ci + 1) * bpc], expand) - 1.0) * 1e30
            if ci < n_ch - 1:
                s = s + c31
            else:
                s = s + jnp.concatenate([jnp.broadcast_to(c31, (s.shape[0], chunk - LANES)), selnear[:, :LANES]], axis=1)
            parts.append(s)
        parts.append(s_new + (_dot(sel[:, n_ch * bpc:(n_ch + 1) * bpc], expand[:, :LANES]) - 1.0) * 1e30 + selnear[:, LANES:])
        m = parts[0].max(axis=-1, keepdims=True)
        for s in parts[1:]:
            m = jnp.maximum(m, s.max(axis=-1, keepdims=True))
        den = jnp.zeros_like(m)
        acc = jnp.zeros((m.shape[0], LANES), F32)
        for ci, s in enumerate(parts):
            p = jnp.exp(s - m)
            den = den + p.sum(axis=-1, keepdims=True)
            if ci < n_ch:
                acc = acc + _dot_nt(p.astype(BF16), vt_sc[ci])
            else:
                acc = acc + _dot(p.astype(BF16), v_new)
        o_sel = acc / den

        wb = wbuf_ref[0]
        nw = neww_ref[0]
        kcat = jnp.concatenate([wb[:, :LANES], _pad_rows(nw[:, :LANES], LANES)], axis=0).astype(BF16)
        vcat = jnp.concatenate([wb[:, LANES:], _pad_rows(nw[:, LANES:], LANES)], axis=0).astype(BF16)
        wbias = winb_ref[...]
        (p_win,) = _masked_softmax_parts([(_dot_nt(qzb, kcat) + wbias, wbias > 0.5 * NEG)])
        o_win = _dot(p_win.astype(BF16), vcat)
        gates = jax.nn.sigmoid(glr_ref[0])
        o_ref[0] = gates[:, 0:1] * o_cmp + gates[:, 1:2] * o_sel + gates[:, 2:3] * o_win


def _row_tables(table, head, tok):
    return table[head, tok]


def nsa_decode_tables(tab_n, past, n_tok, wlen, n_ch, n_pg, page_rows):
    rows = NSA_KV_HEADS * n_tok * NSA_GROUP
    g = np.arange(rows) // (n_tok * NSA_GROUP)
    t = (np.arange(rows) // NSA_GROUP) % n_tok
    head = g * NSA_GROUP + np.arange(rows) % NSA_GROUP
    cpd = past // CMP_STRIDE + LANES
    n_sel_pad = 2 * LANES
    big = 1 << 30
    cmpb = _toeplitz_bias(tab_n, n_tok, cpd * CMP_STRIDE, past + CMP_PAD * CMP_STRIDE - (CMP_LEN - 1), 0, big)
    cmpb = _row_tables(cmpb[:, :, ::CMP_STRIDE], head, t)
    valid_cp = (np.arange(cpd) >= CMP_PAD)[None, :]
    cmpb = jnp.where(valid_cp, cmpb, NEG)
    selnear = _row_tables(_toeplitz_bias(tab_n, n_tok, 2 * LANES, LANES, 0, big), head, t)
    winb = _row_tables(_toeplitz_bias(tab_n, n_tok, wlen + LANES, wlen, 0, NSA_WINDOW - 1), head, t)
    c31 = jnp.broadcast_to(tab_n[N_BUCKETS - 1, head].astype(F32)[:, None], (rows, LANES))
    n_cmp = (past + SEL_BLOCK) // CMP_STRIDE - 1
    n_sel = (past + SEL_BLOCK) // SEL_BLOCK
    rsum = (np.arange(rows)[None, :] // NSA_GROUP == np.arange(rows // NSA_GROUP)[:, None]).astype(np.float32)
    return dict(cmpb=cmpb, selnear=selnear, winb=winb, c31=c31,
                c2s=jnp.asarray(_cmp_to_sel(min(n_cmp, cpd - CMP_PAD), n_sel, cpd, n_sel_pad)),
                e3=jnp.asarray(_block_expand(1, n_pg * page_rows // SEL_BLOCK, n_pg * page_rows)[0], BF16),
                rsum=jnp.asarray(rsum), rexp=jnp.asarray(rsum.T))


def nsa_decode(page_table, pool_t, qz, glr, newkv, wbuf, neww, w1, w2, tabs, n_tok):
    s, n_pages = page_table.shape
    page_rows = pool_t.shape[2]
    n_pg = min(NSA_PAGES_PER_STEP, n_pages)
    n_ch = n_pages // n_pg
    past = n_pages * page_rows
    cpd = tabs["cmpb"].shape[1]
    rows = qz.shape[1]

    def page_spec(i):
        return pl.BlockSpec((1, pool_t.shape[1], page_rows), lambda b, c, pt: (pt[b, c * n_pg + i], 0, 0))

    def seq_spec(a):
        return pl.BlockSpec((1,) + a.shape[1:], lambda b, c, pt: (b,) + (0,) * (a.ndim - 1))

    def const(a):
        n = a.ndim
        return pl.BlockSpec(a.shape, lambda b, c, pt: (0,) * n)

    consts = [w1, w2, tabs["c2s"], tabs["cmpb"], tabs["selnear"], tabs["winb"], tabs["c31"], tabs["e3"],
              tabs["rsum"], tabs["rexp"]]
    seqs = [qz, glr, newkv, wbuf, neww]
    grid_spec = pltpu.PrefetchScalarGridSpec(
        num_scalar_prefetch=1,
        grid=(s, n_ch),
        in_specs=[page_spec(i) for i in range(n_pg)] + [seq_spec(a) for a in seqs] + [const(a) for a in consts],
        out_specs=pl.BlockSpec((1, rows, LANES), lambda b, c, pt: (b, 0, 0)),
        scratch_shapes=[pltpu.VMEM((cpd, LANES), F32)] * 4
        + [pltpu.VMEM((n_ch, rows, n_pg * page_rows), F32), pltpu.VMEM((n_ch, LANES, n_pg * page_rows), BF16)],
    )
    return pl.pallas_call(
        functools.partial(_nsa_dec_body, n_pg=n_pg, n_ch=n_ch, past=past, cpd=cpd, n_tok=n_tok),
        out_shape=jax.ShapeDtypeStruct((s, rows, LANES), F32),
        grid_spec=grid_spec,
        compiler_params=_cparams(("parallel", "arbitrary")),
        name="nsa_decode",
    )(page_table, *([pool_t] * n_pg), *seqs, *consts)


def _diff_dec_body(pt_ref, *refs, n_pg, n_ch, past, n_tok, page_rows):
    del pt_ref
    pages = refs[:n_pg]
    sc_ref, qd_ref, newd_ref, near_ref, sub_ref, o_ref, s_sc, v_sc = refs[n_pg:]
    c = pl.program_id(1)
    per_pos = 2 * DIFF_HEADS
    qd = (qd_ref[0] * (HEAD_DIM ** -0.5)).astype(BF16)
    for i in range(n_pg):
        pg = c * n_pg + i
        for h in range(DIFF_HEADS):
            kh = pages[i][0, pl.ds(h, page_rows, stride=per_pos), :]
            vh = pages[i][0, pl.ds(DIFF_HEADS + h, page_rows, stride=per_pos), :]
            s_sc[c, h, :, i * page_rows:(i + 1) * page_rows] = _dot_nt(qd[h], kh.astype(BF16))
            v_sc[h, pl.ds(pl.multiple_of(pg * page_rows, page_rows), page_rows), :] = vh.astype(BF16)

    @pl.when(c == n_ch - 1)
    def _():
        nd = newd_ref[0]
        chunk = n_pg * page_rows
        for h in range(DIFF_HEADS):
            near = near_ref[h]
            c31 = sc_ref[2 + h]
            s_new = _dot_nt(qd[h], _pad_rows(nd[:, h * LANES:(h + 1) * LANES], LANES).astype(BF16)) + near[:, LANES:]
            v_sc[h, pl.ds(past, LANES), :] = _pad_rows(nd[:, (DIFF_HEADS + h) * LANES:(DIFF_HEADS + h + 1) * LANES], LANES).astype(BF16)
            parts = []
            for ci in range(n_ch):
                s = s_sc[ci, h]
                if ci < n_ch - 1:
                    s = s + c31
                else:
                    s = s + jnp.concatenate([jnp.full((s.shape[0], chunk - LANES), c31, F32), near[:, :LANES]], axis=1)
                parts.append(s)
            parts.append(s_new)
            m = parts[0].max(axis=-1, keepdims=True)
            for s in parts[1:]:
                m = jnp.maximum(m, s.max(axis=-1, keepdims=True))
            den = jnp.zeros_like(m)
            acc = jnp.zeros((m.shape[0], LANES), F32)
            for ci, s in enumerate(parts):
                p = jnp.exp(s - m)
                den = den + p.sum(axis=-1, keepdims=True)
                acc = acc + _dot(p.astype(BF16), v_sc[h, ci * chunk:ci * chunk + s.shape[1], :])
            o12 = acc / den
            o = o12[:n_tok] - sc_ref[0] * o12[n_tok:]
            y = (o * lax.rsqrt(jnp.mean(o * o, axis=-1, keepdims=True) + SUBLN_EPS)) * sub_ref[...]
            o_ref[0, h] = y * sc_ref[1]


def diff_decode_tables(tab_d, n_tok):
    near = _toeplitz_bias(tab_d, n_tok, 2 * LANES, LANES, 0, 1 << 30)
    return jnp.concatenate([near, near], axis=1)


def diff_decode(page_table, pool_rows, qd, newd, scal, near, subln, n_tok, page_rows):
    s, n_pages = page_table.shape
    n_pg = min(DIFF_PAGES_PER_STEP, n_pages)
    n_ch = n_pages // n_pg
    past = n_pages * page_rows

    def page_spec(i):
        return pl.BlockSpec((1,) + pool_rows.shape[1:], lambda b, c, pt: (pt[b, c * n_pg + i], 0, 0))

    grid_spec = pltpu.PrefetchScalarGridSpec(
        num_scalar_prefetch=1,
        grid=(s, n_ch),
        in_specs=[page_spec(i) for i in range(n_pg)] + [
            pl.BlockSpec(memory_space=pltpu.SMEM),
            pl.BlockSpec((1,) + qd.shape[1:], lambda b, c, pt: (b, 0, 0, 0)),
            pl.BlockSpec((1,) + newd.shape[1:], lambda b, c, pt: (b, 0, 0)),
            pl.BlockSpec(near.shape, lambda b, c, pt: (0, 0, 0)),
            pl.BlockSpec((1, DIFF_VDIM), lambda b, c, pt: (0, 0))],
        out_specs=pl.BlockSpec((1, DIFF_HEADS, n_tok, LANES), lambda b, c, pt: (b, 0, 0, 0)),
        scratch_shapes=[pltpu.VMEM((n_ch, DIFF_HEADS, 2 * n_tok, n_pg * page_rows), F32),
                        pltpu.VMEM((DIFF_HEADS, past + LANES, LANES), BF16)],
    )
    return pl.pallas_call(
        functools.partial(_diff_dec_body, n_pg=n_pg, n_ch=n_ch, past=past, n_tok=n_tok, page_rows=page_rows),
        out_shape=jax.ShapeDtypeStruct((s, DIFF_HEADS, n_tok, LANES), F32),
        grid_spec=grid_spec,
        compiler_params=_cparams(("parallel", "arbitrary")),
        name="diff_decode",
    )(page_table, *([pool_rows] * n_pg), scal, qd, newd, near, subln.reshape(1, DIFF_VDIM))


def _sink_attention(qzb, kcat, vcat, bias, ok, sink):
    s = _dot_nt(qzb, kcat) + bias
    msk = (bias > 0.5 * NEG) & ok
    sm = jnp.where(msk, s, NEG)
    m = jnp.maximum(jnp.max(sm, axis=-1, keepdims=True), sink)
    p = jnp.where(msk, jnp.exp(sm - m), 0.0)
    den = jnp.sum(p, axis=-1, keepdims=True) + jnp.exp(sink - m)
    p = p / jnp.maximum(den, 1e-30)
    return _dot(p.astype(BF16), vcat)


def _swa_prompt_body(sink_ref, q_ref, kp_ref, kc_ref, vp_ref, vc_ref, bias_ref, o_ref):
    qi = pl.program_id(1)
    tq = q_ref.shape[1]
    q = q_ref[0] * (HEAD_DIM ** -0.5)
    kcat = jnp.concatenate([kp_ref[0], kc_ref[0]], axis=0)
    vcat = jnp.concatenate([vp_ref[0], vc_ref[0]], axis=0)
    col = lax.broadcasted_iota(jnp.int32, (tq, kcat.shape[0]), 1)
    ok = (col >= SWA_WINDOW) | (qi > 0)
    pieces = []
    for g in range(SWA_KV_HEADS):
        for h in range(SWA_GROUP):
            hh = g * SWA_GROUP + h
            qzb = _place_half(q[:, hh * HEAD_DIM:(hh + 1) * HEAD_DIM], g).astype(BF16)
            o = _sink_attention(qzb, kcat, vcat, bias_ref[hh], ok, sink_ref[hh])
            pieces.append(o[:, g * HEAD_DIM:(g + 1) * HEAD_DIM])
    o_ref[0] = jnp.concatenate(pieces, axis=1)


def swa_prompt(q, kvb, sinks, tab_s):
    b, t, _ = q.shape
    bias = _toeplitz_bias(tab_s, TQ, SWA_WINDOW + TQ, SWA_WINDOW, 0, SWA_WINDOW - 1)
    per = TQ // SWA_WINDOW

    def prev_spec(col):
        return pl.BlockSpec((1, SWA_WINDOW, LANES), lambda i, j: (i, jnp.maximum(per * j - 1, 0), col))

    def cur_spec(col):
        return pl.BlockSpec((1, TQ, LANES), lambda i, j: (i, j, col))

    return pl.pallas_call(
        _swa_prompt_body,
        out_shape=jax.ShapeDtypeStruct((b, t, SWA_HEADS * HEAD_DIM), F32),
        grid=(b, t // TQ),
        in_specs=[_smem_spec(), pl.BlockSpec((1, TQ, SWA_HEADS * HEAD_DIM), lambda i, j: (i, j, 0)),
                  prev_spec(0), cur_spec(0), prev_spec(1), cur_spec(1), _const_spec(bias.shape)],
        out_specs=pl.BlockSpec((1, TQ, SWA_HEADS * HEAD_DIM), lambda i, j: (i, j, 0)),
        compiler_params=_cparams(("parallel", "parallel")),
        name="swa_prompt",
    )(sinks.astype(F32), q, kvb, kvb, kvb, kvb, bias)


def _swa_sample_body(qz_ref, buf_ref, new_ref, bias_ref, sink_ref, o_ref):
    qzb = (qz_ref[0] * (HEAD_DIM ** -0.5)).astype(BF16)
    buf = buf_ref[0]
    new = new_ref[0]
    kcat = jnp.concatenate([buf[:, :LANES], _pad_rows(new[:, :LANES], LANES)], axis=0).astype(BF16)
    vcat = jnp.concatenate([buf[:, LANES:], _pad_rows(new[:, LANES:], LANES)], axis=0).astype(BF16)
    o_ref[0] = _sink_attention(qzb, kcat, vcat, bias_ref[...], True, sink_ref[...][:, :1])


def swa_sample(qz, buf, new, sinks, tab_s, n_tok):
    s, rows, _ = qz.shape
    wlen = buf.shape[1]
    g = np.arange(rows) // (n_tok * SWA_GROUP)
    t = (np.arange(rows) // SWA_GROUP) % n_tok
    head = g * SWA_GROUP + np.arange(rows) % SWA_GROUP
    bias = _row_tables(_toeplitz_bias(tab_s, n_tok, wlen + LANES, wlen, 0, SWA_WINDOW - 1), head, t)
    sink_rows = jnp.broadcast_to(sinks.astype(F32)[head][:, None], (rows, LANES))
    return pl.pallas_call(
        _swa_sample_body,
        out_shape=jax.ShapeDtypeStruct((s, rows, LANES), F32),
        grid=(s,),
        in_specs=[pl.BlockSpec((1, rows, LANES), lambda i: (i, 0, 0)),
                  pl.BlockSpec((1, wlen, 2 * LANES), lambda i: (i, 0, 0)),
                  pl.BlockSpec((1, NEW_ROWS, 2 * LANES), lambda i: (i, 0, 0)),
                  _const_spec(bias.shape), _const_spec(sink_rows.shape)],
        out_specs=pl.BlockSpec((1, rows, LANES), lambda i: (i, 0, 0)),
        compiler_params=_cparams(("parallel",)),
        name="swa_sample",
    )(qz, buf, new, bias, sink_rows)


RWKV_CHUNK = 64
RWKV_SEQS_PER_STEP = 4
RWKV_PROMPTS_PER_STEP = 2


def _head_sum(x, hsum_ref):
    return _dot2(_split(x), hsum_ref[...])


def _rwkv_prep_body(ur_ref, uk_ref, uv_ref, ul_ref, pr_ref, pk_ref, pv_ref, pl_ref,
                    mur_ref, muk_ref, muv_ref, mul_ref, w0_ref, w2_ref, a0_ref, a2_ref, g2_ref,
                    kk_ref, ka_ref, hsum_ref, r_o, lw_o, k_o, v_o, na_o, b_o, g_o):
    def mix(u_ref, p_ref, mu_ref):
        u = u_ref[...]
        return u + (p_ref[...] - u) * mu_ref[...]

    r = mix(ur_ref, pr_ref, mur_ref)
    k = mix(uk_ref, pk_ref, muk_ref)
    v = mix(uv_ref, pv_ref, muv_ref)
    lo = mix(ul_ref, pl_ref, mul_ref)
    z = -(w0_ref[...] + _dot3(_split(jnp.tanh(lo)), _split(w2_ref[...])))
    softplus = jnp.maximum(z, 0.0) + jnp.log(1.0 + jnp.exp(-jnp.abs(z)))
    wlog = -softplus - 0.5
    a = jax.nn.sigmoid(a0_ref[...] + _dot3(_split(lo), _split(a2_ref[...])))
    g = _dot3(_split(jax.nn.sigmoid(lo)), _split(g2_ref[...]))
    kk = k * kk_ref[...]
    kk = kk / jnp.maximum(jnp.sqrt(_head_sum(kk * kk, hsum_ref)), 1e-12)
    r_o[...] = r
    lw_o[...] = -jnp.exp(wlog)
    k_o[...] = k * (1.0 + (a - 1.0) * ka_ref[...])
    v_o[...] = v
    na_o[...] = -kk
    b_o[...] = kk * a
    g_o[...] = g


def rwkv_prep(us, prevs, mus, w0, w2p, a0, a2p, g2p, k_k, k_a, hsum):
    m = us[0].shape[0]
    tm = min(ROW_TILE, m)
    row = lambda a: pl.BlockSpec((tm, a.shape[1]), lambda i: (i, 0))
    vec = lambda a: a.reshape(1, -1)
    consts = [vec(x) for x in mus] + [vec(w0), w2p, vec(a0), a2p, g2p, vec(k_k), vec(k_a), hsum]
    return pl.pallas_call(
        _rwkv_prep_body,
        out_shape=[jax.ShapeDtypeStruct((m, RWKV_WIDTH), F32)] * 7,
        grid=(m // tm,),
        in_specs=[row(a) for a in us] + [row(a) for a in prevs] + [_const_spec(c.shape) for c in consts],
        out_specs=[pl.BlockSpec((tm, RWKV_WIDTH), lambda i: (i, 0))] * 7,
        compiler_params=_cparams(("parallel",)),
        name="rwkv_prep",
    )(*us, *prevs, *consts)


BNN = (((2,), (1,)), ((0,), (0,)))
BNT = (((2,), (2,)), ((0,), (0,)))
BTN = (((1,), (1,)), ((0,), (0,)))


def _rwkv_chunk_body(r_ref, lw_ref, k_ref, v_ref, a_ref, b_ref, s0_ref, y_ref, sT_ref, st_sc):
    ci = pl.program_id(1)
    nb, c, _ = r_ref.shape
    n = RWKV_N

    def to_batch(ref):
        x = ref[...]
        return jnp.concatenate([x[:, :, h * n:(h + 1) * n] for h in range(RWKV_HEADS)], axis=0)

    @pl.when(ci == 0)
    def _():
        s0 = s0_ref[...]
        st_sc[...] = jnp.concatenate([s0[:, h] for h in range(RWKV_HEADS)], axis=0)

    row = lax.broadcasted_iota(jnp.int32, (c, c), 0)
    col = lax.broadcasted_iota(jnp.int32, (c, c), 1)
    incl = (row >= col)[None]
    strict = (row > col)[None]
    eye_c = (row == col).astype(F32)[None]
    eye_n = (lax.broadcasted_iota(jnp.int32, (n, n), 0) == lax.broadcasted_iota(jnp.int32, (n, n), 1)).astype(F32)[None]
    r, lw, k, v, a, b = (to_batch(ref) for ref in (r_ref, lw_ref, k_ref, v_ref, a_ref, b_ref))
    nbat = r.shape[0]
    ones_incl = jnp.broadcast_to(incl.astype(BF16), (nbat, c, c))
    lw_hi, lw_lo = _split(lw)
    lw_lo2 = (lw - lw_hi.astype(F32) - lw_lo.astype(F32)).astype(BF16)
    csum = lambda y: lax.dot_general(ones_incl, y, BNN, preferred_element_type=F32)
    cs = csum(lw_hi) + (csum(lw_lo) + csum(lw_lo2))
    gam = jnp.exp(cs)
    ginv = jnp.exp(-cs)
    to_end = jnp.exp(cs[:, c - 1:c, :] - cs)
    at = _split(a * jnp.exp(cs - lw))
    rt_f = r * gam
    rt = _split(rt_f)
    bt = _split(b * ginv)
    kt = _split(k * ginv)
    vs = _split(v)
    lb = jnp.where(strict, _dot3(at, bt, BNT), 0.0)
    lk = jnp.where(strict, _dot3(at, kt, BNT), 0.0)
    pb = _split(jnp.where(incl, _dot3(rt, bt, BNT), 0.0))
    pk = _split(jnp.where(incl, _dot3(rt, kt, BNT), 0.0))
    tinv = eye_c + lb
    lp = _split(lb)
    covered = 2
    while covered < c:
        lp = _split(_dot3(lp, lp, BNN))
        tinv = tinv + _dot3(_split(tinv), lp, BNN)
        covered *= 2
    tinv = _split(tinv)
    w = _split(_dot3(tinv, at, BNN))
    uv = _split(_dot3(tinv, _split(_dot3(_split(lk), vs, BNN)), BNN))
    q = rt_f + _dot3(pb, w, BNN)
    bh = _split(b * to_end)
    kh = _split(k * to_end)
    tm = eye_n * gam[:, c - 1:c, :] + _dot3(bh, w, BTN)
    bm = _dot3(bh, uv, BTN) + _dot3(kh, vs, BTN)
    x = _dot3(_split(jnp.concatenate([q, tm], axis=1)), _split(st_sc[...]), BNN)
    y = x[:, :c] + _dot3(pb, uv, BNN) + _dot3(pk, vs, BNN)
    st_sc[...] = x[:, c:] + bm
    y_ref[...] = jnp.concatenate([y[h * nb:(h + 1) * nb] for h in range(RWKV_HEADS)], axis=2)

    @pl.when(ci == pl.num_programs(1) - 1)
    def _():
        st = st_sc[...]
        for h in range(RWKV_HEADS):
            sT_ref[:, h] = st[h * nb:(h + 1) * nb]


def rwkv_chunk(seqs, s0t, chunk, nb):
    b, t, width = seqs[0].shape
    h, n = s0t.shape[1:3]
    seq_spec = pl.BlockSpec((nb, chunk, width), lambda i, j: (i, j, 0))
    st_spec = pl.BlockSpec((nb, h, n, n), lambda i, j: (i, 0, 0, 0))
    return pl.pallas_call(
        _rwkv_chunk_body,
        out_shape=[jax.ShapeDtypeStruct((b, t, width), F32), jax.ShapeDtypeStruct((b, h, n, n), F32)],
        grid=(b // nb, t // chunk),
        in_specs=[seq_spec] * 6 + [st_spec],
        out_specs=[seq_spec, st_spec],
        scratch_shapes=[pltpu.VMEM((h * nb, n, n), F32)],
        compiler_params=_cparams(("parallel", "arbitrary")),
        name="rwkv_chunk",
    )(*seqs, s0t)


def _rwkv_post_body(y_ref, r_ref, k_ref, v_ref, g_ref, rk_ref, lnw_ref, lnb_ref, hsum_ref, o_ref):
    y = y_ref[...]
    inv_n = 1.0 / RWKV_N
    mean = _head_sum(y, hsum_ref) * inv_n
    yc = y - mean
    var = _head_sum(yc * yc, hsum_ref) * inv_n
    yn = yc * lax.rsqrt(var + RWKV_GN_EPS) * lnw_ref[...] + lnb_ref[...]
    rk = _head_sum(r_ref[...] * k_ref[...] * rk_ref[...], hsum_ref)
    o_ref[...] = (yn + rk * v_ref[...]) * g_ref[...]


def rwkv_post(y, r, k, v, g, r_k, ln_w, ln_b, hsum):
    m = y.shape[0]
    tm = min(ROW_TILE, m)
    row = pl.BlockSpec((tm, RWKV_WIDTH), lambda i: (i, 0))
    vec = lambda a: a.reshape(1, -1)
    return pl.pallas_call(
        _rwkv_post_body,
        out_shape=jax.ShapeDtypeStruct((m, RWKV_WIDTH), F32),
        grid=(m // tm,),
        in_specs=[row] * 5 + [_const_spec((1, RWKV_WIDTH))] * 3 + [_const_spec(hsum.shape)],
        out_specs=row,
        compiler_params=_cparams(("parallel",)),
        name="rwkv_post",
    )(y, r, k, v, g, vec(r_k), vec(ln_w), vec(ln_b), hsum)


RWKV_SIZES = (RWKV_WIDTH, RWKV_WIDTH, RWKV_WIDTH, DECAY_LORA, AAA_LORA, GATE_LORA)
LORA_WIDTH = DECAY_LORA + AAA_LORA + GATE_LORA
ODD_SIZES = (SWA_HEADS * HEAD_DIM, 2 * SWA_KV_HEADS * HEAD_DIM, 3 * RWKV_WIDTH + LORA_WIDTH)
ODD_GROUPS = ((0, 512, (F32,)), (512, 256, (F32, BF16)), (768, 512, (F32,)), (1280, 512, (F32,)),
              (1792, 512, (F32,)), (2304, LORA_PAD, (F32,)))


def _split_cols(w, sizes):
    offs = np.cumsum([0] + list(sizes))
    return [w[..., int(offs[i]):int(offs[i + 1])] for i in range(len(sizes))]


def odd_weights(w_in):
    return jnp.pad(w_in, ((0, 0), (0, LORA_PAD - LORA_WIDTH))).astype(BF16)


def _lora_rows(w, first):
    return jnp.pad(w.astype(F32), ((first, LORA_PAD - first - w.shape[0]), (0, 0)))


def odd_mixer(x, g, w_odd, swa_buf, wkv0, shift0, prompt, sinks, tab, rw, hsum):
    b, t, d = x.shape
    mu, w0, w2, a0, a2, g2, k_k, k_a, r_k, ln_w, ln_b = rw
    q, kv, kvb, ur, uk, uv, ul = norm_proj(x.reshape(b * t, d), g, w_odd, ODD_GROUPS)
    r3 = lambda a: a.reshape(b, t, a.shape[-1])
    tab_s = tab[:, :SWA_HEADS]
    if prompt:
        o_swa = swa_prompt(r3(q), r3(kvb), sinks, tab_s).reshape(b * t, -1)
        ctx = r3(kv)[:, t - min(SWA_WINDOW, t):]
        shift0 = jnp.zeros((b, 3 * RWKV_WIDTH + LORA_WIDTH), F32)
        wkv0 = jnp.zeros((b, RWKV_HEADS, RWKV_N, RWKV_N), F32)
        chunk = RWKV_CHUNK
        t_pad = t
        nb = math.gcd(b, RWKV_PROMPTS_PER_STEP)
    else:
        rows = SWA_KV_HEADS * t * SWA_GROUP
        q5 = q.reshape(b, t, SWA_KV_HEADS, SWA_GROUP, HEAD_DIM).transpose(0, 2, 1, 3, 4)
        z = jnp.zeros_like(q5[:, 0])
        qz = jnp.stack([jnp.concatenate([q5[:, 0], z], -1), jnp.concatenate([z, q5[:, 1]], -1)], axis=1).reshape(b, rows, LANES)
        new = jnp.pad(r3(kv), ((0, 0), (0, NEW_ROWS - t), (0, 0)))
        o_rows = swa_sample(qz, swa_buf, new, sinks, tab_s, t)
        o6 = o_rows.reshape(b, SWA_KV_HEADS, t, SWA_GROUP, 2, HEAD_DIM)
        o_swa = jnp.stack([o6[:, 0, :, :, 0], o6[:, 1, :, :, 1]], axis=2).reshape(b * t, SWA_HEADS * HEAD_DIM)
        ctx = jnp.concatenate([swa_buf, r3(kv)], axis=1)[:, t:]
        chunk = NEW_ROWS
        t_pad = NEW_ROWS
        nb = math.gcd(b, RWKV_SEQS_PER_STEP)

    s_r, s_k, s_v, s_l = _split_cols(shift0.astype(F32), (RWKV_WIDTH,) * 3 + (LORA_WIDTH,))
    s_l = jnp.pad(s_l, ((0, 0), (0, LORA_PAD - LORA_WIDTH)))
    us = [ur, uk, uv, ul]
    prevs = [jnp.concatenate([s[:, None], r3(u)[:, :-1]], axis=1).reshape(b * t, -1) for s, u in zip((s_r, s_k, s_v, s_l), us)]
    mu_r, mu_k, mu_v, mu_l = _split_cols(mu.astype(F32), (RWKV_WIDTH,) * 3 + (LORA_WIDTH,))
    mu_l = jnp.pad(mu_l, (0, LORA_PAD - LORA_WIDTH))
    r, lw, k, v, na, bb, gate = rwkv_prep(us, prevs, [mu_r, mu_k, mu_v, mu_l], w0, _lora_rows(w2, 0), a0,
                                          _lora_rows(a2, DECAY_LORA), _lora_rows(g2, DECAY_LORA + AAA_LORA), k_k, k_a, hsum)

    def seq(a):
        return jnp.pad(a.reshape(b, t, RWKV_WIDTH), ((0, 0), (0, t_pad - t), (0, 0)))

    y, st = rwkv_chunk([seq(a) for a in (r, lw, k, v, na, bb)], jnp.swapaxes(wkv0.astype(F32), -1, -2), chunk, nb)
    y = y[:, :t].reshape(b * t, RWKV_WIDTH)
    o_rwkv = rwkv_post(y, r, k, v, gate, r_k, ln_w, ln_b, hsum)
    shift = jnp.concatenate([r3(ur)[:, -1], r3(uk)[:, -1], r3(uv)[:, -1], r3(ul)[:, -1, :LORA_WIDTH]], axis=-1)
    return o_swa, o_rwkv, ctx, jnp.swapaxes(st, -1, -2), shift


EVEN_SIZES = (NSA_HEADS * HEAD_DIM, 4 * NSA_KV_HEADS * HEAD_DIM, 2 * NSA_KV_HEADS * HEAD_DIM, 3 * NSA_HEADS,
              2 * DIFF_HEADS * HEAD_DIM, 2 * DIFF_HEADS * DIFF_VDIM)
EVEN_GROUPS = ((0, 512, (F32,)), (512, 512, (F32, BF16)), (1024, 256, (F32, BF16)), (1280, 512, (F32,)),
               (1792, 1024, (F32, BF16)), (2816, 128, (F32,)))


def even_weights(w_in):
    wq, wkv, wkvw, wgl, wdq, wdkv = _split_cols(w_in, EVEN_SIZES)
    wgl = jnp.pad(wgl, ((0, 0), (0, LANES - wgl.shape[1])))
    return jnp.concatenate([wq, wkv, wkvw, wdq, wdkv, wgl], axis=1).astype(BF16)


def cmp_lane_weights(cmp_w):
    wt = jax.nn.softmax(cmp_w.astype(F32), axis=-1)
    wl = jnp.repeat(jnp.swapaxes(wt, 1, 2), HEAD_DIM, axis=2)
    return wl[:, :CMP_STRIDE], wl[:, CMP_STRIDE:]


def diff_scalars(lq, layer, tab_d):
    lam_init = 0.8 - 0.6 * math.exp(-0.3 * layer)
    lq = lq.astype(F32)
    lam = jnp.exp(jnp.sum(lq[0] * lq[1])) - jnp.exp(jnp.sum(lq[2] * lq[3])) + lam_init
    return jnp.concatenate([jnp.stack([lam, jnp.asarray(1.0 - lam_init, F32)]), tab_d[N_BUCKETS - 1].astype(F32)])


def even_prompt(x, g, w_even, w1, w2, tab, scal, subln):
    b, t, d = x.shape
    q, kv, kvb, kvw, kvwb, dq, dkv, dkvb, gl = norm_proj(x.reshape(b * t, d), g, w_even, EVEN_GROUPS)
    r3 = lambda a: a.reshape(b, t, a.shape[-1])
    tab_n, tab_d = tab[:, :NSA_HEADS], tab[:, NSA_HEADS:NSA_HEADS + DIFF_HEADS]
    tabs = nsa_prompt_tables(tab_n, t)
    ckv = compress_prompt(r3(kv), w1, w2, tabs["c2s"].shape[0])
    o_nsa = nsa_prompt(r3(q), r3(gl), ckv, r3(kvb), r3(kvwb), tabs)
    o_diff = diff_prompt(r3(dq), r3(dkvb), scal, _prev_diag_bias(tab_d), subln)
    return o_nsa.reshape(b * t, -1), o_diff.reshape(b * t, -1), kv, kvw, dkv


def even_sample(x, g, w_even, nsa_pool, diff_pool, win_buf, page_table, w1, w2, tab, scal, subln):
    s, n_tok, d = x.shape
    q, kv, _, kvw, _, dq, dkv, _, gl = norm_proj(x.reshape(s * n_tok, d), g, w_even, EVEN_GROUPS)
    tab_n, tab_d = tab[:, :NSA_HEADS], tab[:, NSA_HEADS:NSA_HEADS + DIFF_HEADS]
    n_pages = page_table.shape[1]
    n_pool, page_rows = nsa_pool.shape[:2]
    past = n_pages * page_rows
    wlen = win_buf.shape[1]
    n_pg = min(NSA_PAGES_PER_STEP, n_pages)
    rows = NSA_KV_HEADS * n_tok * NSA_GROUP

    q5 = q.reshape(s, n_tok, NSA_KV_HEADS, NSA_GROUP, HEAD_DIM).transpose(0, 2, 1, 3, 4)
    z = jnp.zeros_like(q5[:, 0])
    qz = jnp.stack([jnp.concatenate([q5[:, 0], z], -1), jnp.concatenate([z, q5[:, 1]], -1)], axis=1).reshape(s, rows, LANES)
    glr = gl[:, :3 * NSA_HEADS].reshape(s, n_tok, NSA_KV_HEADS, NSA_GROUP, 3).transpose(0, 2, 1, 3, 4).reshape(s, rows, 3)
    glr = jnp.pad(glr, ((0, 0), (0, 0), (0, LANES - 3)))
    pad_tok = lambda a: jnp.pad(a.reshape(s, n_tok, a.shape[-1]), ((0, 0), (0, NEW_ROWS - n_tok), (0, 0)))
    tabs = nsa_decode_tables(tab_n, past, n_tok, wlen, n_pages // n_pg, n_pg, page_rows)
    pool_t = jnp.transpose(nsa_pool, (0, 2, 3, 4, 1)).reshape(n_pool, -1, page_rows)
    o_rows = nsa_decode(page_table, pool_t, qz, glr, pad_tok(kv), win_buf.reshape(s, wlen, -1), pad_tok(kvw),
                        w1, w2, tabs, n_tok)
    o6 = o_rows.reshape(s, NSA_KV_HEADS, n_tok, NSA_GROUP, 2, HEAD_DIM)
    o_nsa = jnp.stack([o6[:, 0, :, :, 0], o6[:, 1, :, :, 1]], axis=2).reshape(s * n_tok, NSA_HEADS * HEAD_DIM)

    dq5 = dq.reshape(s, n_tok, DIFF_HEADS, 2, HEAD_DIM).transpose(0, 2, 3, 1, 4)
    zd = jnp.zeros_like(dq5[:, :, 0])
    qd = jnp.concatenate([jnp.concatenate([dq5[:, :, 0], zd], -1), jnp.concatenate([zd, dq5[:, :, 1]], -1)], axis=2)
    pool_rows = diff_pool.reshape(n_pool, page_rows * 2 * DIFF_HEADS, DIFF_VDIM)
    o_d = diff_decode(page_table, pool_rows, qd, pad_tok(dkv), scal, diff_decode_tables(tab_d, n_tok), subln,
                      n_tok, page_rows)
    o_diff = o_d.transpose(0, 2, 1, 3).reshape(s * n_tok, DIFF_HEADS * DIFF_VDIM)
    return o_nsa, o_diff, kv, kvw, dkv


def kernel(x_prompt, x_sample, cache_nsa_kv, cache_diff_kv, cache_nsa_win, cache_swa, state_rwkv_wkv, state_rwkv_shift, page_table, rel_bias, norm_mix, norm_ffn, norm_final, w_in_even, w_out_even, nsa_cmp_w, diff_lambda, diff_subln, w_in_odd, w_out_odd, swa_sinks, rwkv_mu, rwkv_w0, rwkv_w2, rwkv_a0, rwkv_a2, rwkv_g2, rwkv_k_k, rwkv_k_a, rwkv_r_k, rwkv_ln_w, rwkv_ln_b, ffn_w_gate, ffn_w_up, ffn_w_down):
    b, t, d = x_prompt.shape
    s, n_tok, _ = x_sample.shape
    depth = norm_mix.shape[0]
    assert NSA_WINDOW == 2 * TQ and t % TQ == 0 and TQ >= MAX_DISTANCE and n_tok <= NEW_ROWS and depth > 0
    tab = rel_bias.astype(F32)
    tab_d = tab[:, NSA_HEADS:NSA_HEADS + DIFF_HEADS]
    head_id = np.arange(RWKV_WIDTH) // RWKV_N
    hsum = jnp.asarray((head_id[:, None] == head_id[None, :]).astype(np.float32), BF16)
    xp = x_prompt.reshape(b * t, d)
    xs = x_sample.reshape(s * n_tok, d)
    outs = {name: [] for name in ("nsa_p", "nsa_s", "diff_p", "diff_s", "win_p", "win_s",
                                  "swa_p", "swa_s", "wkv_p", "wkv_s", "sh_p", "sh_s")}
    for l in range(depth):
        if l % 2 == 0:
            e = l // 2
            w_even = even_weights(w_in_even[e])
            w1, w2 = cmp_lane_weights(nsa_cmp_w[e])
            scal = diff_scalars(diff_lambda[e], l, tab_d)
            pa, pb, kv, kvw, dkv = even_prompt(xp.reshape(b, t, d), norm_mix[l], w_even, w1, w2, tab, scal, diff_subln[e])
            sa, sb, skv, skvw, sdkv = even_sample(xs.reshape(s, n_tok, d), norm_mix[l], w_even, cache_nsa_kv[e],
                                                  cache_diff_kv[e], cache_nsa_win[e], page_table, w1, w2, tab, scal,
                                                  diff_subln[e])
            outs["nsa_p"].append(kv.reshape(b, t, 4, NSA_KV_HEADS, HEAD_DIM))
            outs["nsa_s"].append(skv.reshape(s, n_tok, 4, NSA_KV_HEADS, HEAD_DIM))
            outs["diff_p"].append(dkv.reshape(b, t, 2, DIFF_HEADS, DIFF_VDIM))
            outs["diff_s"].append(sdkv.reshape(s, n_tok, 2, DIFF_HEADS, DIFF_VDIM))
            outs["win_p"].append(kvw.reshape(b, t, 2, NSA_KV_HEADS, HEAD_DIM)[:, t - min(NSA_WINDOW, t):])
            new_win = skvw.reshape(s, n_tok, 2, NSA_KV_HEADS, HEAD_DIM).astype(cache_nsa_win.dtype)
            outs["win_s"].append(jnp.concatenate([cache_nsa_win[e], new_win], axis=1)[:, n_tok:])
            w_out = w_out_even[e].astype(BF16)
        else:
            o = l // 2
            rw = (rwkv_mu[o], rwkv_w0[o], rwkv_w2[o], rwkv_a0[o], rwkv_a2[o], rwkv_g2[o], rwkv_k_k[o],
                  rwkv_k_a[o], rwkv_r_k[o].reshape(-1), rwkv_ln_w[o], rwkv_ln_b[o])
            w_odd = odd_weights(w_in_odd[o])
            pa, pb, ctx_p, wkv_p, sh_p = odd_mixer(xp.reshape(b, t, d), norm_mix[l], w_odd, None, None, None, True,
                                                   swa_sinks[o], tab, rw, hsum)
            swa_buf = cache_swa[o].reshape(s, cache_swa.shape[2], -1).astype(F32)
            sa, sb, ctx_s, wkv_s, sh_s = odd_mixer(xs.reshape(s, n_tok, d), norm_mix[l], w_odd, swa_buf,
                                                   state_rwkv_wkv[o], state_rwkv_shift[o], False, swa_sinks[o], tab,
                                                   rw, hsum)
            kv_shape = (2, SWA_KV_HEADS, HEAD_DIM)
            outs["swa_p"].append(ctx_p.reshape(b, -1, *kv_shape))
            outs["swa_s"].append(ctx_s.reshape(s, -1, *kv_shape))
            outs["wkv_p"].append(wkv_p)
            outs["wkv_s"].append(wkv_s)
            outs["sh_p"].append(sh_p)
            outs["sh_s"].append(sh_s)
            w_out = w_out_odd[o].astype(BF16)
        half = pa.shape[1]
        xp = out_proj(xp, pa, pb, w_out[:half], w_out[half:])
        xs = out_proj(xs, sa, sb, w_out[:half], w_out[half:])
        wg, wu, wd = ffn_w_gate[l].astype(BF16), ffn_w_up[l].astype(BF16), ffn_w_down[l].astype(BF16)
        last = l == depth - 1
        xp = ffn(xp, norm_ffn[l], norm_final, wg, wu, wd, last)
        xs = ffn(xs, norm_ffn[l], norm_final, wg, wu, wd, last)
    st = lambda name: jnp.stack(outs[name])
    return (xp.reshape(b, t, d), xs.reshape(s, n_tok, d), st("nsa_p"), st("nsa_s"), st("diff_p"), st("diff_s"),
            st("win_p"), st("win_s"), st("swa_p"), st("swa_s"), st("wkv_p"), st("wkv_s"), st("sh_p"), st("sh_s"))
```
